```python
import math
import jax, jax.numpy as jnp
from jax import lax
import numpy as np

D_MODEL = 1024
BATCH = 2
SEQ = 8192
DEPTH = 1
DEC_BATCH = 32
DEC_SEQ = 1
PAST_LEN = 16384
PAGE_SIZE = 128

HEAD_DIM = 64
HEADS_PER_GROUP = 4
ATTN_GROUPS = ((128, 1), (512, 4), (2048, 16))
N_ATTN_GROUPS = len(ATTN_GROUPS)
ATTN_WIDTH = N_ATTN_GROUPS * HEADS_PER_GROUP * HEAD_DIM
ATTN_OUT_WIDTH = HEADS_PER_GROUP * HEAD_DIM
SSM_CH_PER_GROUP = 16
SSM_STATE = 64
SSM_WIDTH = D_MODEL // 2
SSM_GROUPS = SSM_WIDTH // SSM_CH_PER_GROUP
D_FF = 2816
DT_MIN = 0.01
DT_MAX = 0.1
RMS_EPS = 1e-6
Q_BLOCK = 128
COL_Q = SSM_WIDTH
COL_K = COL_Q + ATTN_WIDTH
COL_V = COL_K + ATTN_WIDTH
COL_GATE_SSM = COL_V + ATTN_WIDTH
COL_GATE_ATTN = COL_GATE_SSM + D_MODEL
IN_WIDTH = COL_GATE_ATTN + D_MODEL

kernel_name = "gated_s5_dilated_attn_macaron_step"


def rms_norm(x, g):
    xf = x.astype(jnp.float32)
    y = xf * lax.rsqrt(jnp.mean(xf * xf, axis=-1, keepdims=True) + RMS_EPS)
    return (y * g.astype(jnp.float32)).astype(x.dtype)


def head_rms_norm(t, g):
    tf = t.astype(jnp.float32)
    y = tf * lax.rsqrt(jnp.mean(tf * tf, axis=-1, keepdims=True) + RMS_EPS)
    return (y * g.astype(jnp.float32)[:, None, :]).astype(t.dtype)


def swiglu(x, w_gate, w_up, w_down):
    return (jax.nn.silu(x @ w_gate) * (x @ w_up)) @ w_down


def ssm_discretize(a_re, a_im, log_dt, b_re, b_im):
    f32 = jnp.float32
    a_re = a_re.astype(f32)
    a_im = a_im.astype(f32)
    dt = jnp.exp(log_dt.astype(f32))[:, None]
    mag = jnp.exp(a_re * dt)
    ab_re = mag * jnp.cos(a_im * dt)
    ab_im = mag * jnp.sin(a_im * dt)
    inv = 1.0 / (a_re * a_re + a_im * a_im)
    f_re = ((ab_re - 1.0) * a_re + ab_im * a_im) * inv
    f_im = (ab_im * a_re - (ab_re - 1.0) * a_im) * inv
    b_re = b_re.astype(f32)
    b_im = b_im.astype(f32)
    bb_re = f_re[..., None] * b_re - f_im[..., None] * b_im
    bb_im = f_re[..., None] * b_im + f_im[..., None] * b_re
    return ab_re, ab_im, bb_re, bb_im


def complex_affine_combine(e1, e2):
    a1r, a1i, b1r, b1i = e1
    a2r, a2i, b2r, b2i = e2
    return (a2r * a1r - a2i * a1i,
            a2r * a1i + a2i * a1r,
            a2r * b1r - a2i * b1i + b2r,
            a2r * b1i + a2i * b1r + b2i)


def s5_branch(u, h0_re, h0_im, a_re, a_im, log_dt, b_re, b_im, c_re, c_im, d_skip, w_glu, b_glu):
    f32 = jnp.float32
    nb, seq_len, _ = u.shape
    uf = u.astype(f32)
    ab_re, ab_im, bb_re, bb_im = ssm_discretize(a_re, a_im, log_dt, b_re, b_im)
    ug = uf.reshape(nb, seq_len, SSM_GROUPS, SSM_CH_PER_GROUP)
    x_re = jnp.einsum("blgc,gpc->blgp", ug, bb_re)
    x_im = jnp.einsum("blgc,gpc->blgp", ug, bb_im)
    h0_re = h0_re.astype(f32)
    h0_im = h0_im.astype(f32)
    x_re = x_re.at[:, 0].add(ab_re * h0_re - ab_im * h0_im)
    x_im = x_im.at[:, 0].add(ab_re * h0_im + ab_im * h0_re)
    a_re_t = jnp.broadcast_to(ab_re, x_re.shape)
    a_im_t = jnp.broadcast_to(ab_im, x_im.shape)
    _, _, s_re, s_im = lax.associative_scan(
        complex_affine_combine, (a_re_t, a_im_t, x_re, x_im), axis=1)
    y = (jnp.einsum("blgp,gcp->blgc", s_re, c_re.astype(f32))
         - jnp.einsum("blgp,gcp->blgc", s_im, c_im.astype(f32)))
    y = y.reshape(nb, seq_len, SSM_WIDTH) + d_skip.astype(f32) * uf
    y = jax.nn.gelu(y)
    y = y * jax.nn.sigmoid(y @ w_glu.astype(f32) + b_glu.astype(f32))
    return y.astype(u.dtype), s_re[:, -1], s_im[:, -1]


def dilated_group_attention(q, k_ext, v_ext, pos0, window, dilation):
    nb, lq, nh, hd = q.shape
    qb = Q_BLOCK if lq % Q_BLOCK == 0 else lq
    n_blk = lq // qb
    n_keys = window // dilation + 1
    rel = jnp.arange(qb)[:, None] + window - dilation * jnp.arange(n_keys)[None, :]
    scale = HEAD_DIM ** -0.5
    qf = q.astype(jnp.float32)

    def one_block(blk):
        start = blk * qb
        qs = lax.dynamic_slice_in_dim(qf, start, qb, axis=1)
        ks = lax.dynamic_slice_in_dim(k_ext, start, qb + window, axis=1)
        vs = lax.dynamic_slice_in_dim(v_ext, start, qb + window, axis=1)
        kg = ks[:, rel].astype(jnp.float32)
        vg = vs[:, rel].astype(jnp.float32)
        s = jnp.einsum("bqhd,bqjhd->bqhj", qs, kg) * scale
        valid = (pos0 - window + start + rel) >= 0
        s = jnp.where(valid[None, :, None, :], s, -jnp.inf)
        m = jnp.max(s, axis=-1)
        p = jnp.exp(s - m[..., None])
        den = jnp.sum(p, axis=-1)
        o = jnp.einsum("bqhj,bqjhd->bqhd", p, vg) / den[..., None]
        return o, m, den

    o, m, den = lax.map(one_block, jnp.arange(n_blk))
    o = jnp.moveaxis(o, 0, 1).reshape(nb, lq, nh, hd)
    m = jnp.moveaxis(m, 0, 1).reshape(nb, lq, nh)
    den = jnp.moveaxis(den, 0, 1).reshape(nb, lq, nh)
    return o, m, den


def dilated_attention_branch(q, k, v, kv_prev, pos0):
    nb, lq = q.shape[0], q.shape[1]
    outs, maxes, dens, new_bufs = [], [], [], []
    for g, (window, dilation) in enumerate(ATTN_GROUPS):
        prev = kv_prev[g].astype(k.dtype)
        n_prev = prev.shape[1]
        pad = jnp.zeros((nb, window - n_prev, HEADS_PER_GROUP, HEAD_DIM), k.dtype)
        kg = k[:, :, g]
        vg = v[:, :, g]
        k_ext = jnp.concatenate([pad, prev[:, :, 0], kg], axis=1)
        v_ext = jnp.concatenate([pad, prev[:, :, 1], vg], axis=1)
        o, m, den = dilated_group_attention(q[:, :, g], k_ext, v_ext, pos0, window, dilation)
        outs.append(o)
        maxes.append(m)
        dens.append(den)
        kv_all = jnp.concatenate([prev, jnp.stack([kg, vg], axis=2)], axis=1)
        keep = min(window, pos0 + lq)
        new_bufs.append(kv_all[:, kv_all.shape[1] - keep:])
    m_all = jnp.stack(maxes)
    den_all = jnp.stack(dens)
    o_all = jnp.stack(outs)
    m_top = jnp.max(m_all, axis=0)
    wts = den_all * jnp.exp(m_all - m_top[None])
    out = jnp.sum(wts[..., None] * o_all, axis=0) / jnp.sum(wts, axis=0)[..., None]
    return out.reshape(nb, lq, ATTN_OUT_WIDTH).astype(q.dtype), tuple(new_bufs)


def trunk_layer(x, pos0, h0_re, h0_im, kv_prev, w):
    nb, seq_len, _ = x.shape
    x = x + 0.5 * swiglu(rms_norm(x, w["g_ffn1"]), w["w1_gate"], w["w1_up"], w["w1_down"])
    h = rms_norm(x, w["g_mix"])
    proj = h @ w["w_in"]
    heads = (nb, seq_len, N_ATTN_GROUPS, HEADS_PER_GROUP, HEAD_DIM)
    u = proj[..., :COL_Q]
    q = head_rms_norm(proj[..., COL_Q:COL_K].reshape(heads), w["g_q"])
    k = head_rms_norm(proj[..., COL_K:COL_V].reshape(heads), w["g_k"])
    v = proj[..., COL_V:COL_GATE_SSM].reshape(heads)
    gate_ssm = jax.nn.sigmoid(proj[..., COL_GATE_SSM:COL_GATE_ATTN])
    gate_attn = jax.nn.sigmoid(proj[..., COL_GATE_ATTN:])
    y_ssm, hT_re, hT_im = s5_branch(u, h0_re, h0_im, w["ssm_a_re"], w["ssm_a_im"], w["ssm_log_dt"],
                                    w["ssm_b_re"], w["ssm_b_im"], w["ssm_c_re"], w["ssm_c_im"],
                                    w["ssm_d"], w["w_glu"], w["b_glu"])
    y_attn, kv_new = dilated_attention_branch(q, k, v, kv_prev, pos0)
    mixed = gate_ssm * (y_ssm @ w["w_ssm_proj"]) + gate_attn * (y_attn @ w["w_attn_proj"])
    x = x + mixed @ w["w_o"]
    x = x + 0.5 * swiglu(rms_norm(x, w["g_ffn2"]), w["w2_gate"], w["w2_up"], w["w2_down"])
    return x, hT_re, hT_im, kv_new


def setup_inputs(seed: int = 0) -> dict:
    key = jax.random.key(seed)
    keys = iter(jax.random.split(key, 40))
    f32 = jnp.float32

    def nrm(shape, scale=1.0):
        return jax.random.normal(next(keys), shape, f32) * scale

    def gain(shape):
        return 1.0 + nrm(shape, 0.02)

    L = DEPTH
    inp = {}
    inp["x_prompt"] = nrm((BATCH, SEQ, D_MODEL))
    inp["x_sample"] = nrm((DEC_BATCH, DEC_SEQ, D_MODEL))
    for window, _ in ATTN_GROUPS:
        inp["cache_kv_w%d" % window] = nrm(
            (L, DEC_BATCH, min(window, PAST_LEN), 2, HEADS_PER_GROUP, HEAD_DIM))
    inp["state_ssm_re"] = nrm((L, DEC_BATCH, SSM_GROUPS, SSM_STATE), 0.1)
    inp["state_ssm_im"] = nrm((L, DEC_BATCH, SSM_GROUPS, SSM_STATE), 0.1)
    inp["g_ffn1"] = gain((L, D_MODEL))
    inp["w1_gate"] = nrm((L, D_MODEL, D_FF), D_MODEL ** -0.5)
    inp["w1_up"] = nrm((L, D_MODEL, D_FF), D_MODEL ** -0.5)
    inp["w1_down"] = nrm((L, D_FF, D_MODEL), D_FF ** -0.5)
    inp["g_mix"] = gain((L, D_MODEL))
    inp["w_in"] = nrm((L, D_MODEL, IN_WIDTH), D_MODEL ** -0.5)
    inp["g_q"] = gain((L, N_ATTN_GROUPS, HEAD_DIM))
    inp["g_k"] = gain((L, N_ATTN_GROUPS, HEAD_DIM))
    inp["ssm_a_re"] = -0.5 + nrm((L, SSM_GROUPS, SSM_STATE), 0.01)
    inp["ssm_a_im"] = (jnp.pi * jnp.arange(SSM_STATE, dtype=f32)
                       + nrm((L, SSM_GROUPS, SSM_STATE), 0.01))
    inp["ssm_log_dt"] = jax.random.uniform(next(keys), (L, SSM_GROUPS), f32,
                                           math.log(DT_MIN), math.log(DT_MAX))
    inp["ssm_b_re"] = nrm((L, SSM_GROUPS, SSM_STATE, SSM_CH_PER_GROUP), (2 * SSM_CH_PER_GROUP) ** -0.5)
    inp["ssm_b_im"] = nrm((L, SSM_GROUPS, SSM_STATE, SSM_CH_PER_GROUP), (2 * SSM_CH_PER_GROUP) ** -0.5)
    inp["ssm_c_re"] = nrm((L, SSM_GROUPS, SSM_CH_PER_GROUP, SSM_STATE), SSM_STATE ** -0.5)
    inp["ssm_c_im"] = nrm((L, SSM_GROUPS, SSM_CH_PER_GROUP, SSM_STATE), SSM_STATE ** -0.5)
    inp["ssm_d"] = nrm((L, SSM_WIDTH))
    inp["w_glu"] = nrm((L, SSM_WIDTH, SSM_WIDTH), SSM_WIDTH ** -0.5)
    inp["b_glu"] = nrm((L, SSM_WIDTH), 0.02)
    inp["w_ssm_proj"] = nrm((L, SSM_WIDTH, D_MODEL), SSM_WIDTH ** -0.5)
    inp["w_attn_proj"] = nrm((L, ATTN_OUT_WIDTH, D_MODEL), ATTN_OUT_WIDTH ** -0.5)
    inp["w_o"] = nrm((L, D_MODEL, D_MODEL), D_MODEL ** -0.5)
    inp["g_ffn2"] = gain((L, D_MODEL))
    inp["w2_gate"] = nrm((L, D_MODEL, D_FF), D_MODEL ** -0.5)
    inp["w2_up"] = nrm((L, D_MODEL, D_FF), D_MODEL ** -0.5)
    inp["w2_down"] = nrm((L, D_FF, D_MODEL), D_FF ** -0.5)
    return inp


def reference(x_prompt, x_sample, cache_kv_w128, cache_kv_w512, cache_kv_w2048,
              state_ssm_re, state_ssm_im,
              g_ffn1, w1_gate, w1_up, w1_down, g_mix, w_in, g_q, g_k,
              ssm_a_re, ssm_a_im, ssm_log_dt, ssm_b_re, ssm_b_im, ssm_c_re, ssm_c_im,
              ssm_d, w_glu, b_glu, w_ssm_proj, w_attn_proj, w_o,
              g_ffn2, w2_gate, w2_up, w2_down):
    caches = (cache_kv_w128, cache_kv_w512, cache_kv_w2048)
    nb_p = x_prompt.shape[0]
    y_p = x_prompt
    y_s = x_sample
    p_kv = ([], [], [])
    s_kv = ([], [], [])
    p_re, p_im, s_re, s_im = [], [], [], []
    for layer in range(DEPTH):
        w = {
            "g_ffn1": g_ffn1[layer], "w1_gate": w1_gate[layer], "w1_up": w1_up[layer],
            "w1_down": w1_down[layer], "g_mix": g_mix[layer], "w_in": w_in[layer],
            "g_q": g_q[layer], "g_k": g_k[layer],
            "ssm_a_re": ssm_a_re[layer], "ssm_a_im": ssm_a_im[layer],
            "ssm_log_dt": ssm_log_dt[layer], "ssm_b_re": ssm_b_re[layer],
            "ssm_b_im": ssm_b_im[layer], "ssm_c_re": ssm_c_re[layer], "ssm_c_im": ssm_c_im[layer],
            "ssm_d": ssm_d[layer], "w_glu": w_glu[layer], "b_glu": b_glu[layer],
            "w_ssm_proj": w_ssm_proj[layer], "w_attn_proj": w_attn_proj[layer], "w_o": w_o[layer],
            "g_ffn2": g_ffn2[layer], "w2_gate": w2_gate[layer], "w2_up": w2_up[layer],
            "w2_down": w2_down[layer],
        }
        empty_kv = tuple(jnp.zeros((nb_p, 0, 2, HEADS_PER_GROUP, HEAD_DIM), x_prompt.dtype)
                         for _ in ATTN_GROUPS)
        h0 = jnp.zeros((nb_p, SSM_GROUPS, SSM_STATE), jnp.float32)
        y_p, hp_re, hp_im, kvp = trunk_layer(y_p, 0, h0, h0, empty_kv, w)
        y_s, hs_re, hs_im, kvs = trunk_layer(y_s, PAST_LEN, state_ssm_re[layer], state_ssm_im[layer],
                                             tuple(c[layer] for c in caches), w)
        for g in range(N_ATTN_GROUPS):
            p_kv[g].append(kvp[g])
            s_kv[g].append(kvs[g])
        p_re.append(hp_re)
        p_im.append(hp_im)
        s_re.append(hs_re)
        s_im.append(hs_im)
    sdt = state_ssm_re.dtype
    return (y_p, y_s,
            jnp.stack(p_kv[0]), jnp.stack(p_kv[1]), jnp.stack(p_kv[2]),
            jnp.stack(p_re).astype(sdt), jnp.stack(p_im).astype(sdt),
            jnp.stack(s_kv[0]), jnp.stack(s_kv[1]), jnp.stack(s_kv[2]),
            jnp.stack(s_re).astype(sdt), jnp.stack(s_im).astype(sdt))
```

```python
import functools

import jax
import jax.numpy as jnp
from jax import lax
from jax.experimental import pallas as pl
from jax.experimental.pallas import tpu as pltpu

F32 = jnp.float32
BF16 = jnp.bfloat16

D_MODEL = 1024
D_FF = 2816
HEAD_DIM = 64
HEADS = 4
GROUPS = ((128, 1), (512, 4), (2048, 16))
KEYS_PER_QUERY = 128
GROUP_W = HEADS * HEAD_DIM
ATTN_W = len(GROUPS) * GROUP_W
SSM_W = 512
SSM_GROUPS = 32
SSM_CH = 16
SSM_P = 64
N_STATE = SSM_GROUPS * SSM_P
IN_A = SSM_W + 3 * ATTN_W
RMS_EPS = 1e-6
NEG = -1e30

FF_CHUNKS = ((0, 1024), (1024, 1024), (2048, 768))
SCAN_CHUNK = 512
VMEM_LIMIT = 56 * 1024 * 1024


def _const_spec(shape):
    nd = len(shape)
    return pl.BlockSpec(shape, lambda *_: (0,) * nd, pipeline_mode=pl.Buffered(1))


def _params(n_grid):
    return pltpu.CompilerParams(dimension_semantics=("arbitrary",) * n_grid,
                                vmem_limit_bytes=VMEM_LIMIT)


def _rms(x, g):
    ms = jnp.mean(x * x, axis=-1, keepdims=True)
    return x * lax.rsqrt(ms + RMS_EPS) * g


def _dot(a, b):
    return jnp.dot(a, b, preferred_element_type=F32)


def _split_dot(x, m):
    hi = x.astype(BF16)
    lo = (x - hi.astype(F32)).astype(BF16)
    return _dot(hi, m) + _dot(lo, m)


def _ffn(x, g_ref, wg_ref, wu_ref, wd_ref):
    xn = _rms(x, g_ref[...]).astype(BF16)
    acc = None
    for f0, fw in FF_CHUNKS:
        hg = _dot(xn, wg_ref[:, f0:f0 + fw])
        hu = _dot(xn, wu_ref[:, f0:f0 + fw])
        a = (hg * jax.nn.sigmoid(hg) * hu).astype(BF16)
        d = _dot(a, wd_ref[f0:f0 + fw, :])
        acc = d if acc is None else acc + d
    return x + 0.5 * acc


def _prep_kernel(are_ref, aim_ref, ldt_ref, bre_ref, bim_ref, abre_ref, abim_ref, bbre_ref, bbim_ref):
    a_re = are_ref[...]
    a_im = aim_ref[...]
    dt = jnp.exp(ldt_ref[...])
    mag = jnp.exp(a_re * dt)
    ab_re = mag * jnp.cos(a_im * dt)
    ab_im = mag * jnp.sin(a_im * dt)
    inv = 1.0 / (a_re * a_re + a_im * a_im)
    f_re = ((ab_re - 1.0) * a_re + ab_im * a_im) * inv
    f_im = (ab_im * a_re - (ab_re - 1.0) * a_im) * inv
    abre_ref[...] = ab_re
    abim_ref[...] = ab_im
    for c in range(SSM_CH):
        b_re = bre_ref[c]
        b_im = bim_ref[c]
        bbre_ref[c] = f_re * b_re - f_im * b_im
        bbim_ref[c] = f_re * b_im + f_im * b_re


def _ssm_prep(a_re, a_im, log_dt, b_re, b_im):
    b_re_t = jnp.transpose(b_re, (2, 0, 1))
    b_im_t = jnp.transpose(b_im, (2, 0, 1))
    gp = jax.ShapeDtypeStruct((SSM_GROUPS, SSM_P), F32)
    cgp = jax.ShapeDtypeStruct((SSM_CH, SSM_GROUPS, SSM_P), F32)
    return pl.pallas_call(_prep_kernel, out_shape=(gp, gp, cgp, cgp), name="ssm_prep")(
        a_re, a_im, log_dt.reshape(SSM_GROUPS, 1), b_re_t, b_im_t)


def _head_norm(t, g, seg):
    sq = t * t
    ms = jnp.concatenate(
        [_split_dot(sq[:, s * GROUP_W:(s + 1) * GROUP_W], seg) for s in range(len(GROUPS))], axis=1)
    return t * lax.rsqrt(ms + RMS_EPS) * g


def _ffn_in_kernel(x_ref, g1_ref, wg_ref, wu_ref, wd_ref, gm_ref, win_ref, gq_ref, gk_ref, seg_ref,
                   x1_ref, u_ref, q_ref, k_ref, v_ref, kvt_ref):
    x1 = _ffn(x_ref[...], g1_ref, wg_ref, wu_ref, wd_ref)
    x1_ref[...] = x1
    h = _rms(x1, gm_ref[...]).astype(BF16)
    proj = _dot(h, win_ref[...])
    u_ref[...] = proj[:, :SSM_W]
    seg = seg_ref[...]
    q = _head_norm(proj[:, SSM_W:SSM_W + ATTN_W], gq_ref[...], seg)
    k = _head_norm(proj[:, SSM_W + ATTN_W:SSM_W + 2 * ATTN_W], gk_ref[...], seg)
    v = proj[:, SSM_W + 2 * ATTN_W:]
    q_ref[...] = (q * HEAD_DIM ** -0.5).astype(BF16)
    k_ref[...] = k.astype(BF16)
    v_ref[...] = v.astype(BF16)
    kvt_ref[:, :ATTN_W] = k
    kvt_ref[:, ATTN_W:] = v


def _ffn_in(x, w, tm, tiles_per_seq, tail_tiles):
    rows = x.shape[0]
    n_tiles = rows // tm
    n_seq = n_tiles // tiles_per_seq
    row_spec = lambda width: pl.BlockSpec((tm, width), lambda i: (i, 0))

    def tail_map(i):
        b = i // tiles_per_seq
        j = i % tiles_per_seq
        return (b * tail_tiles + jnp.maximum(j - (tiles_per_seq - tail_tiles), 0), 0)

    out_shape = (jax.ShapeDtypeStruct((rows, D_MODEL), F32),
                 jax.ShapeDtypeStruct((rows, SSM_W), F32),
                 jax.ShapeDtypeStruct((rows, ATTN_W), BF16),
                 jax.ShapeDtypeStruct((rows, ATTN_W), BF16),
                 jax.ShapeDtypeStruct((rows, ATTN_W), BF16),
                 jax.ShapeDtypeStruct((n_seq * tail_tiles * tm, 2 * ATTN_W), F32))
    return pl.pallas_call(
        _ffn_in_kernel,
        grid=(n_tiles,),
        in_specs=[row_spec(D_MODEL), _const_spec((1, D_MODEL)),
                  _const_spec((D_MODEL, D_FF)), _const_spec((D_MODEL, D_FF)), _const_spec((D_FF, D_MODEL)),
                  _const_spec((1, D_MODEL)), _const_spec((D_MODEL, IN_A)),
                  _const_spec((1, ATTN_W)), _const_spec((1, ATTN_W)), _const_spec((GROUP_W, GROUP_W))],
        out_specs=(row_spec(D_MODEL), row_spec(SSM_W), row_spec(ATTN_W), row_spec(ATTN_W), row_spec(ATTN_W),
                   pl.BlockSpec((tm, 2 * ATTN_W), tail_map)),
        out_shape=out_shape,
        compiler_params=_params(1),
        name="ffn_in",
    )(x, w["g_ffn1"], w["w1_gate"], w["w1_up"], w["w1_down"], w["g_mix"], w["w_in_a"],
      w["g_q"], w["g_k"], w["seg_mean"])


def _s5_kernel(u_ref, h0_ref, are_ref, aim_ref, bbd_ref, cbd_ref, dsk_ref, wglu_ref, bglu_ref,
               y_ref, ht_ref, xs_ref, st_ref, *, n_rows):
    @pl.when(pl.program_id(1) == 0)
    def _():
        st_ref[...] = h0_ref[0]

    u = u_ref[0]
    ub = u.astype(BF16)
    half = N_STATE
    for n in range(2):
        xs_ref[:, n * half:(n + 1) * half] = _dot(ub[:, n * 256:(n + 1) * 256], bbd_ref[n])

    for cc in range(N_STATE // SCAN_CHUNK):
        n, sub = divmod(cc, 2)
        c_re = n * half + sub * SCAN_CHUNK
        c_im = c_re + half // 2
        ch = slice(cc * SCAN_CHUNK, (cc + 1) * SCAN_CHUNK)
        a_re = are_ref[:, ch]
        a_im = aim_ref[:, ch]

        def body(t, carry, c_re=c_re, c_im=c_im, a_re=a_re, a_im=a_im):
            s_re, s_im = carry
            row = pl.ds(t, 1)
            n_re = a_re * s_re - a_im * s_im + xs_ref[row, c_re:c_re + SCAN_CHUNK]
            n_im = a_re * s_im + a_im * s_re + xs_ref[row, c_im:c_im + SCAN_CHUNK]
            xs_ref[row, c_re:c_re + SCAN_CHUNK] = n_re
            xs_ref[row, c_im:c_im + SCAN_CHUNK] = n_im
            return n_re, n_im

        s_re, s_im = lax.fori_loop(0, n_rows, body, (st_ref[0:1, ch], st_ref[1:2, ch]),
                                   unroll=min(8, n_rows))
        st_ref[0:1, ch] = s_re
        st_ref[1:2, ch] = s_im

    y = jnp.concatenate(
        [_dot(xs_ref[:, n * half:(n + 1) * half].astype(BF16), cbd_ref[n]) for n in range(2)], axis=1)
    y = jax.nn.gelu(y + dsk_ref[...] * u)
    z = _dot(y.astype(BF16), wglu_ref[...]) + bglu_ref[...]
    y_ref[0] = y * jax.nn.sigmoid(z)
    ht_ref[0] = st_ref[...]


def _s5(u, h0, w, t_blk, n_rows):
    nb, seq, _ = u.shape
    kern = functools.partial(_s5_kernel, n_rows=n_rows)
    return pl.pallas_call(
        kern,
        grid=(nb, seq // t_blk),
        in_specs=[pl.BlockSpec((1, t_blk, SSM_W), lambda b, c: (b, c, 0)),
                  pl.BlockSpec((1, 2, N_STATE), lambda b, c: (b, 0, 0)),
                  _const_spec((1, N_STATE)), _const_spec((1, N_STATE)),
                  _const_spec((2, 256, N_STATE)), _const_spec((2, N_STATE, 256)),
                  _const_spec((1, SSM_W)), _const_spec((SSM_W, SSM_W)), _const_spec((1, SSM_W))],
        out_specs=(pl.BlockSpec((1, t_blk, SSM_W), lambda b, c: (b, c, 0)),
                   pl.BlockSpec((1, 2, N_STATE), lambda b, c: (b, 0, 0))),
        out_shape=(jax.ShapeDtypeStruct((nb, seq, SSM_W), F32),
                   jax.ShapeDtypeStruct((nb, 2, N_STATE), F32)),
        scratch_shapes=[pltpu.VMEM((t_blk, 2 * N_STATE), F32), pltpu.VMEM((2, N_STATE), F32)],
        compiler_params=_params(2),
        name="s5",
    )(u, h0, w["ab_re"], w["ab_im"], w["b_bd"], w["c_bd"], w["ssm_d"], w["w_glu"], w["b_glu"])


def _swa_kernel(q_ref, k_ref, v_ref, kp_ref, vp_ref, o_ref, l_ref, *, qb):
    first = pl.program_id(2) == 0
    q = q_ref[0]
    kk = jnp.concatenate([kp_ref[0], k_ref[0]], axis=0)
    vv = jnp.concatenate([vp_ref[0], v_ref[0]], axis=0)
    nk = 2 * KEYS_PER_QUERY
    row = lax.broadcasted_iota(jnp.int32, (128, nk), 0)
    col = lax.broadcasted_iota(jnp.int32, (128, nk), 1)
    band = jnp.where(col >= row, jnp.where(col <= row + KEYS_PER_QUERY, 0.0, NEG), NEG)
    band_first = band + jnp.where(col < KEYS_PER_QUERY, jnp.where(first, NEG, 0.0), 0.0)
    lane = lax.broadcasted_iota(jnp.int32, (1, GROUP_W), 1) // HEAD_DIM
    lane_l = lax.broadcasted_iota(jnp.int32, (1, 128), 1)
    for j in range(qb // 128):
        qj = q[128 * j:128 * (j + 1)]
        kj = kk[128 * j:128 * j + nk]
        vj = vv[128 * j:128 * j + nk]
        bias = band_first if j == 0 else band
        o_acc = jnp.zeros((128, GROUP_W), F32)
        l_acc = jnp.zeros((128, 128), F32)
        for h in range(HEADS):
            hm = lane == h
            qh = jnp.where(hm, qj, jnp.zeros_like(qj))
            s = lax.dot_general(qh, kj, (((1,), (1,)), ((), ())), preferred_element_type=F32)
            s = s + bias
            m = jnp.max(s, axis=-1, keepdims=True)
            p = jnp.exp(s - m)
            den = jnp.sum(p, axis=-1, keepdims=True)
            oh = _dot(p.astype(BF16), vj)
            o_acc = jnp.where(hm, oh / den, o_acc)
            l_acc = jnp.where(lane_l == h, m + jnp.log(den), l_acc)
        o_ref[0, 128 * j:128 * (j + 1), :] = o_acc.astype(BF16)
        l_ref[0, 128 * j:128 * (j + 1), :] = l_acc


def _swa(q, k, v, g):
    _, dil = GROUPS[g]
    nb, seq, _ = q.shape
    n = seq // dil
    qb = min(512, n)
    qv = q.reshape(nb, n, dil * ATTN_W)
    kv = k.reshape(nb, n, dil * ATTN_W)
    vv = v.reshape(nb, n, dil * ATTN_W)
    n_col = ATTN_W // GROUP_W
    cur = pl.BlockSpec((1, qb, GROUP_W), lambda b, r, i: (b, i, r * n_col + g))
    prev = pl.BlockSpec((1, 128, GROUP_W),
                        lambda b, r, i: (b, jnp.maximum(i * (qb // 128) - 1, 0), r * n_col + g))
    o, lse = pl.pallas_call(
        functools.partial(_swa_kernel, qb=qb),
        grid=(nb, dil, n // qb),
        in_specs=[cur, cur, cur, prev, prev],
        out_specs=(pl.BlockSpec((1, qb, GROUP_W), lambda b, r, i: (b, i, r)),
                   pl.BlockSpec((1, qb, 128), lambda b, r, i: (b, i, r))),
        out_shape=(jax.ShapeDtypeStruct((nb, n, dil * GROUP_W), BF16),
                   jax.ShapeDtypeStruct((nb, n, dil * 128), F32)),
        compiler_params=_params(3),
        name="swa_g%d" % g,
    )(qv, kv, vv, kv, vv)
    return o.reshape(nb * seq, GROUP_W), lse.reshape(nb * seq, 128)


def _swa_step_kernel(q_ref, kv_ref, c0_ref, c1_ref, c2_ref, sege_ref, seget_ref,
                     n0_ref, n1_ref, n2_ref, o_ref, l_ref):
    q = q_ref[0].astype(F32)
    kv = kv_ref[0]
    sege = sege_ref[...]
    seget = seget_ref[...]
    n_ext = KEYS_PER_QUERY + 8
    row = lax.broadcasted_iota(jnp.int32, (n_ext, 128), 0)
    for g, (window, dil) in enumerate(GROUPS):
        c_ref = (c0_ref, c1_ref, c2_ref)[g]
        n_ref = (n0_ref, n1_ref, n2_ref)[g]
        lanes = slice(g * GROUP_W, (g + 1) * GROUP_W)
        qg = q[:, lanes]
        kn = kv[:, lanes]
        vn = kv[:, ATTN_W + g * GROUP_W:ATTN_W + (g + 1) * GROUP_W]
        part = lambda i: c_ref[0, pl.ds(i, KEYS_PER_QUERY, stride=4 * dil), :]
        kc = jnp.concatenate([part(0), part(1)], axis=1)
        vc = jnp.concatenate([part(2), part(3)], axis=1)
        keys = jnp.concatenate([kc, jnp.broadcast_to(kn, (8, GROUP_W))], axis=0)
        vals = jnp.concatenate([vc, jnp.broadcast_to(vn, (8, GROUP_W))], axis=0)
        s = _split_dot(keys * qg, sege)
        s = jnp.where(row <= KEYS_PER_QUERY, s, NEG)
        m = jnp.max(s, axis=0, keepdims=True)
        p = jnp.exp(s - m)
        den = jnp.sum(p, axis=0, keepdims=True)
        pb = _split_dot(p / den, seget)
        o = jnp.sum(pb * vals, axis=0, keepdims=True)
        o_ref[0, :, lanes] = o.astype(BF16)
        l_ref[0, :, g * 128:(g + 1) * 128] = m + jnp.log(den)
        n_ref[0, 0:4 * (window - 1), :] = c_ref[0, 4:4 * window, :]
        n_ref[0, 4 * (window - 1):4 * window, :] = jnp.concatenate(
            [kn[:, :128], kn[:, 128:], vn[:, :128], vn[:, 128:]], axis=0)


def _swa_step(q, kvt, caches, w):
    nb = q.shape[0]
    row3 = lambda width: pl.BlockSpec((1, 1, width), lambda b: (b, 0, 0))
    cache_specs = [pl.BlockSpec((1, 4 * win, 128), lambda b: (b, 0, 0)) for win, _ in GROUPS]
    caches = [c.reshape(nb, 4 * win, 128) for c, (win, _) in zip(caches, GROUPS)]
    outs = pl.pallas_call(
        _swa_step_kernel,
        grid=(nb,),
        in_specs=[row3(ATTN_W), row3(2 * ATTN_W)] + cache_specs
                 + [_const_spec((GROUP_W, 128)), _const_spec((128, GROUP_W))],
        out_specs=tuple(cache_specs) + (row3(ATTN_W), row3(3 * 128)),
        out_shape=tuple(jax.ShapeDtypeStruct(c.shape, F32) for c in caches)
                  + (jax.ShapeDtypeStruct((nb, 1, ATTN_W), BF16),
                     jax.ShapeDtypeStruct((nb, 1, 3 * 128), F32)),
        compiler_params=_params(1),
        name="swa_step",
    )(q.reshape(nb, 1, ATTN_W), kvt.reshape(nb, 1, 2 * ATTN_W), *caches, w["seg_e"], w["seg_et"])
    new_caches = [c.reshape(nb, win, 2 * GROUP_W) for c, (win, _) in zip(outs[:3], GROUPS)]
    return new_caches, outs[3].reshape(nb, ATTN_W), outs[4].reshape(nb, 3 * 128)


def _mix_ffn_kernel(x1_ref, ys_ref, o0_ref, o1_ref, o2_ref, l0_ref, l1_ref, l2_ref,
                    gm_ref, wgt_ref, wsp_ref, wap_ref, wo_ref, seget_ref,
                    g2_ref, wg_ref, wu_ref, wd_ref, y_ref):
    x1 = x1_ref[...]
    h = _rms(x1, gm_ref[...]).astype(BF16)
    gates = jax.nn.sigmoid(_dot(h, wgt_ref[...]))
    ls = (l0_ref[...], l1_ref[...], l2_ref[...])
    l_top = jnp.maximum(jnp.maximum(ls[0], ls[1]), ls[2])
    es = [jnp.exp(l - l_top) for l in ls]
    inv = 1.0 / (es[0] + es[1] + es[2])
    seget = seget_ref[...]
    y_attn = None
    for e, o_ref in zip(es, (o0_ref, o1_ref, o2_ref)):
        t = _split_dot(e * inv, seget) * o_ref[...].astype(F32)
        y_attn = t if y_attn is None else y_attn + t
    mixed = (gates[:, :D_MODEL] * _dot(ys_ref[...].astype(BF16), wsp_ref[...])
             + gates[:, D_MODEL:] * _dot(y_attn.astype(BF16), wap_ref[...]))
    x2 = x1 + _dot(mixed.astype(BF16), wo_ref[...])
    y_ref[...] = _ffn(x2, g2_ref, wg_ref, wu_ref, wd_ref)


def _mix_ffn(x1, y_ssm, os_, ls_, w, tm):
    rows = x1.shape[0]
    row_spec = lambda width: pl.BlockSpec((tm, width), lambda i: (i, 0))
    return pl.pallas_call(
        _mix_ffn_kernel,
        grid=(rows // tm,),
        in_specs=[row_spec(D_MODEL), row_spec(SSM_W)] + [row_spec(GROUP_W)] * 3 + [row_spec(128)] * 3
                 + [_const_spec((1, D_MODEL)), _const_spec((D_MODEL, 2 * D_MODEL)),
                    _const_spec((SSM_W, D_MODEL)), _const_spec((GROUP_W, D_MODEL)),
                    _const_spec((D_MODEL, D_MODEL)), _const_spec((128, GROUP_W)),
                    _const_spec((1, D_MODEL)), _const_spec((D_MODEL, D_FF)), _const_spec((D_MODEL, D_FF)),
                    _const_spec((D_FF, D_MODEL))],
        out_specs=row_spec(D_MODEL),
        out_shape=jax.ShapeDtypeStruct((rows, D_MODEL), F32),
        compiler_params=_params(1),
        name="mix_ffn",
    )(x1, y_ssm, *os_, *ls_, w["g_mix"], w["w_gates"], w["w_ssm_proj"], w["w_attn_proj"], w["w_o"],
      w["seg_et"], w["g_ffn2"], w["w2_gate"], w["w2_up"], w["w2_down"])


def _prepare_weights(g_ffn1, w1_gate, w1_up, w1_down, g_mix, w_in, g_q, g_k,
                     ssm_a_re, ssm_a_im, ssm_log_dt, ssm_b_re, ssm_b_im, ssm_c_re, ssm_c_im,
                     ssm_d, w_glu, b_glu, w_ssm_proj, w_attn_proj, w_o, g_ffn2, w2_gate, w2_up, w2_down):
    w = {}
    row = lambda a: a.reshape(1, -1).astype(F32)
    w["g_ffn1"], w["g_mix"], w["g_ffn2"] = row(g_ffn1), row(g_mix), row(g_ffn2)
    for name, a in (("w1_gate", w1_gate), ("w1_up", w1_up), ("w1_down", w1_down),
                    ("w2_gate", w2_gate), ("w2_up", w2_up), ("w2_down", w2_down),
                    ("w_glu", w_glu), ("w_ssm_proj", w_ssm_proj), ("w_attn_proj", w_attn_proj),
                    ("w_o", w_o)):
        w[name] = a.astype(BF16)
    w["w_in_a"] = w_in[:, :IN_A].astype(BF16)
    w["w_gates"] = w_in[:, IN_A:].astype(BF16)
    per_head = lambda g: jnp.broadcast_to(g[:, None, :], (len(GROUPS), HEADS, HEAD_DIM)).reshape(1, ATTN_W)
    w["g_q"], w["g_k"] = per_head(g_q.astype(F32)), per_head(g_k.astype(F32))
    head_of_lane = jnp.arange(GROUP_W) // HEAD_DIM
    w["seg_mean"] = ((head_of_lane[:, None] == head_of_lane[None, :]) / HEAD_DIM).astype(BF16)
    w["seg_e"] = (head_of_lane[:, None] == jnp.arange(128)[None, :]).astype(BF16)
    w["seg_et"] = w["seg_e"].T
    w["ssm_d"], w["b_glu"] = row(ssm_d), row(b_glu)

    ab_re, ab_im, bb_re, bb_im = _ssm_prep(ssm_a_re.astype(F32), ssm_a_im.astype(F32),
                                           ssm_log_dt.astype(F32), ssm_b_re.astype(F32),
                                           ssm_b_im.astype(F32))
    w["ab_re"], w["ab_im"] = ab_re.reshape(1, N_STATE), ab_im.reshape(1, N_STATE)
    eye = jnp.eye(16, dtype=F32)

    def b_half(bb, n):
        blk = bb[:, 16 * n:16 * (n + 1)]
        return jnp.einsum("cgp,gh->gchp", blk, eye).reshape(256, 1024)

    def c_half(cc, n):
        blk = cc[16 * n:16 * (n + 1)].astype(F32)
        return jnp.einsum("gcp,gh->gphc", blk, eye).reshape(1024, 256)

    w["b_bd"] = jnp.stack([jnp.concatenate([b_half(bb_re, n), b_half(bb_im, n)], axis=1)
                           for n in range(2)]).astype(BF16)
    w["c_bd"] = jnp.stack([jnp.concatenate([c_half(ssm_c_re, n), -c_half(ssm_c_im, n)], axis=0)
                           for n in range(2)]).astype(BF16)
    return w


def _layer(x, h0, caches, w):
    nb, seq, _ = x.shape
    rows = nb * seq
    xf = x.reshape(rows, D_MODEL)
    if caches is None:
        tm = min(512, seq)
        tail = min(GROUPS[-1][0], seq)
        x1, u, q, k, v, kvt = _ffn_in(xf, w, tm, seq // tm, tail // tm)
        t_blk = min(512, seq)
        h0 = jnp.zeros((nb, 2, N_STATE), F32)
        y_ssm, h_t = _s5(u.reshape(nb, seq, SSM_W), h0, w, t_blk, t_blk)
        y_ssm = y_ssm.reshape(rows, SSM_W)
        q3, k3, v3 = (a.reshape(nb, seq, ATTN_W) for a in (q, k, v))
        os_, ls_ = zip(*[_swa(q3, k3, v3, g) for g in range(len(GROUPS))])
        kvt = kvt.reshape(nb, tail, 2 * ATTN_W)
        new_caches = []
        for g, (window, _) in enumerate(GROUPS):
            keep = min(window, seq)
            kg = kvt[:, tail - keep:, g * GROUP_W:(g + 1) * GROUP_W]
            vg = kvt[:, tail - keep:, ATTN_W + g * GROUP_W:ATTN_W + (g + 1) * GROUP_W]
            new_caches.append(jnp.concatenate([kg, vg], axis=-1))
    else:
        assert seq == 1
        tm = rows
        x1, u, q, k, v, kvt = _ffn_in(xf, w, tm, 1, 1)
        u_pad = jnp.pad(u.reshape(nb, 1, SSM_W), ((0, 0), (0, 7), (0, 0)))
        y_ssm, h_t = _s5(u_pad, h0, w, 8, 1)
        y_ssm = y_ssm[:, 0]
        new_caches, o_all, l_all = _swa_step(q, kvt, caches, w)
        os_ = [o_all[:, g * GROUP_W:(g + 1) * GROUP_W] for g in range(len(GROUPS))]
        ls_ = [l_all[:, g * 128:(g + 1) * 128] for g in range(len(GROUPS))]
    y = _mix_ffn(x1, y_ssm, os_, ls_, w, tm)
    return y.reshape(nb, seq, D_MODEL), h_t, new_caches


def kernel(x_prompt, x_sample, cache_kv_w128, cache_kv_w512, cache_kv_w2048, state_ssm_re, state_ssm_im, g_ffn1, w1_gate, w1_up, w1_down, g_mix, w_in, g_q, g_k, ssm_a_re, ssm_a_im, ssm_log_dt, ssm_b_re, ssm_b_im, ssm_c_re, ssm_c_im, ssm_d, w_glu, b_glu, w_ssm_proj, w_attn_proj, w_o, g_ffn2, w2_gate, w2_up, w2_down):
    layer_weights = (g_ffn1, w1_gate, w1_up, w1_down, g_mix, w_in, g_q, g_k,
                     ssm_a_re, ssm_a_im, ssm_log_dt, ssm_b_re, ssm_b_im, ssm_c_re, ssm_c_im,
                     ssm_d, w_glu, b_glu, w_ssm_proj, w_attn_proj, w_o, g_ffn2, w2_gate, w2_up, w2_down)
    depth = g_ffn1.shape[0]
    assert depth == 1, "window caches of deeper layers would need the previous layer's outputs"
    w = _prepare_weights(*(a[0] for a in layer_weights))
    nb_s = x_sample.shape[0]
    sdt = state_ssm_re.dtype

    def split_state(h_t):
        return (h_t[:, 0].reshape(1, -1, SSM_GROUPS, SSM_P).astype(sdt),
                h_t[:, 1].reshape(1, -1, SSM_GROUPS, SSM_P).astype(sdt))

    def as_windows(bufs):
        return tuple(b.reshape(1, b.shape[0], b.shape[1], 2, HEADS, HEAD_DIM) for b in bufs)

    y_p, h_p, kv_p = _layer(x_prompt, None, None, w)

    caches = [c[0].reshape(nb_s, c.shape[2], 2 * GROUP_W)
              for c in (cache_kv_w128, cache_kv_w512, cache_kv_w2048)]
    for c, (window, _) in zip(caches, GROUPS):
        assert c.shape[1] == window, "the sample group's caches must hold a full window"
    h0_s = jnp.stack([state_ssm_re[0].reshape(nb_s, N_STATE),
                      state_ssm_im[0].reshape(nb_s, N_STATE)], axis=1).astype(F32)
    y_s, h_s, kv_s = _layer(x_sample, h0_s, caches, w)

    return (y_p, y_s) + as_windows(kv_p) + split_state(h_p) + as_windows(kv_s) + split_state(h_s)
```

```python
import functools

import jax
import jax.numpy as jnp
from jax import lax
from jax.experimental import pallas as pl
from jax.experimental.pallas import tpu as pltpu

F32 = jnp.float32
BF16 = jnp.bfloat16

D_MODEL = 1024
D_FF = 2816
HEAD_DIM = 64
HEADS = 4
GROUPS = ((128, 1), (512, 4), (2048, 16))
KEYS_PER_QUERY = 128
GROUP_W = HEADS * HEAD_DIM
ATTN_W = len(GROUPS) * GROUP_W
SSM_W = 512
SSM_GROUPS = 32
SSM_CH = 16
SSM_P = 64
N_STATE = SSM_GROUPS * SSM_P
IN_A = SSM_W + 3 * ATTN_W
RMS_EPS = 1e-6
NEG = -1e30
LANES = 128

TILE = 512
FF_CHUNKS = ((0, 1024), (1024, 1024), (2048, 768))
SCAN_CHUNK = 512
VMEM_LIMIT = 56 * 1024 * 1024


def _const_spec(shape):
    nd = len(shape)
    return pl.BlockSpec(shape, lambda *_: (0,) * nd, pipeline_mode=pl.Buffered(1))


def _params(n_grid):
    return pltpu.CompilerParams(dimension_semantics=("arbitrary",) * n_grid,
                                vmem_limit_bytes=VMEM_LIMIT)


def _rms(x, g):
    ms = jnp.mean(x * x, axis=-1, keepdims=True)
    return x * lax.rsqrt(ms + RMS_EPS) * g


def _dot(a, b):
    return jnp.dot(a, b, preferred_element_type=F32)


def _dot_nt(a, b):
    return lax.dot_general(a, b, (((1,), (1,)), ((), ())), preferred_element_type=F32)


def _split_dot(x, m):
    hi = x.astype(BF16)
    lo = (x - hi.astype(F32)).astype(BF16)
    return _dot(hi, m) + _dot(lo, m)


def _ffn(x, g_ref, wg_ref, wu_ref, wd_ref):
    xn = _rms(x, g_ref[...]).astype(BF16)
    acc = None
    for f0, fw in FF_CHUNKS:
        hg = _dot(xn, wg_ref[:, f0:f0 + fw])
        hu = _dot(xn, wu_ref[:, f0:f0 + fw])
        a = (hg * jax.nn.sigmoid(hg) * hu).astype(BF16)
        d = _dot(a, wd_ref[f0:f0 + fw, :])
        acc = d if acc is None else acc + d
    return x + 0.5 * acc


def _prep_kernel(are_ref, aim_ref, ldt_ref, bre_ref, bim_ref, abre_ref, abim_ref, bbre_ref, bbim_ref):
    a_re = are_ref[...]
    a_im = aim_ref[...]
    dt = jnp.exp(ldt_ref[...])
    mag = jnp.exp(a_re * dt)
    ab_re = mag * jnp.cos(a_im * dt)
    ab_im = mag * jnp.sin(a_im * dt)
    inv = 1.0 / (a_re * a_re + a_im * a_im)
    f_re = ((ab_re - 1.0) * a_re + ab_im * a_im) * inv
    f_im = (ab_im * a_re - (ab_re - 1.0) * a_im) * inv
    abre_ref[...] = ab_re
    abim_ref[...] = ab_im
    for c in range(SSM_CH):
        b_re = bre_ref[c]
        b_im = bim_ref[c]
        bbre_ref[c] = f_re * b_re - f_im * b_im
        bbim_ref[c] = f_re * b_im + f_im * b_re


def _ssm_prep(a_re, a_im, log_dt, b_re, b_im):
    b_re_t = jnp.transpose(b_re, (2, 0, 1))
    b_im_t = jnp.transpose(b_im, (2, 0, 1))
    gp = jax.ShapeDtypeStruct((SSM_GROUPS, SSM_P), F32)
    cgp = jax.ShapeDtypeStruct((SSM_CH, SSM_GROUPS, SSM_P), F32)
    return pl.pallas_call(_prep_kernel, out_shape=(gp, gp, cgp, cgp), name="ssm_prep")(
        a_re, a_im, log_dt.reshape(SSM_GROUPS, 1), b_re_t, b_im_t)


def _head_norm(t, g, seg):
    sq = t * t
    ms = jnp.concatenate(
        [_split_dot(sq[:, s * GROUP_W:(s + 1) * GROUP_W], seg) for s in range(len(GROUPS))], axis=1)
    return t * lax.rsqrt(ms + RMS_EPS) * g


def _store_attn_rows(val, out_ref, perm_ref, slab0):
    tm = val.shape[0]
    out_ref[:, :GROUP_W] = val[:, :GROUP_W].astype(BF16)
    for g in range(1, len(GROUPS)):
        dil = GROUPS[g][1]
        n = tm // dil
        for half in range(GROUP_W // LANES):
            c0 = g * GROUP_W + half * LANES
            col = val[:, c0:c0 + LANES]
            if perm_ref is None:
                out_ref[:, c0:c0 + LANES] = col.astype(BF16)
                continue
            slab = slab0 + (g - 1) * (GROUP_W // LANES) + half
            perm_ref[slab] = col
            for r in range(dil):
                out_ref[r * n:(r + 1) * n, c0:c0 + LANES] = (
                    perm_ref[slab, pl.ds(r, n, stride=dil), :].astype(BF16))


def _ffn_in_kernel(x_ref, g1_ref, wg_ref, wu_ref, wd_ref, gm_ref, win_ref, gq_ref, gk_ref, seg_ref,
                   x1_ref, u_ref, q_ref, k_ref, v_ref, kvt_ref, *scratch, tiles_per_seq, tail_tiles):
    perm_ref = scratch[0] if scratch else None
    x1 = _ffn(x_ref[...], g1_ref, wg_ref, wu_ref, wd_ref)
    x1_ref[...] = x1
    h = _rms(x1, gm_ref[...]).astype(BF16)
    proj = _dot(h, win_ref[...])
    u_ref[...] = proj[:, :SSM_W]
    seg = seg_ref[...]
    q = _head_norm(proj[:, SSM_W:SSM_W + ATTN_W], gq_ref[...], seg)
    k = _head_norm(proj[:, SSM_W + ATTN_W:SSM_W + 2 * ATTN_W], gk_ref[...], seg)
    v = proj[:, SSM_W + 2 * ATTN_W:]
    slabs = (len(GROUPS) - 1) * (GROUP_W // LANES)
    _store_attn_rows(q * HEAD_DIM ** -0.5, q_ref, perm_ref, 0)
    _store_attn_rows(k, k_ref, perm_ref, slabs)
    _store_attn_rows(v, v_ref, perm_ref, 2 * slabs)
    if perm_ref is None:
        kvt_ref[:, :ATTN_W] = k
        kvt_ref[:, ATTN_W:] = v
    else:
        @pl.when(pl.program_id(0) % tiles_per_seq >= tiles_per_seq - tail_tiles)
        def _():
            kvt_ref[0, :ATTN_W, :] = k.T
            kvt_ref[0, ATTN_W:, :] = v.T


def _ffn_in(x, w, tm, tiles_per_seq, tail_tiles, prompt_form):
    rows = x.shape[0]
    n_tiles = rows // tm
    n_seq = n_tiles // tiles_per_seq
    row_spec = lambda width: pl.BlockSpec((tm, width), lambda i: (i, 0))

    def tail_block(i):
        return jnp.maximum(i % tiles_per_seq - (tiles_per_seq - tail_tiles), 0)

    if prompt_form:
        tail_shape = jax.ShapeDtypeStruct((n_seq, 2 * ATTN_W, tail_tiles * tm), F32)
        tail_spec = pl.BlockSpec((1, 2 * ATTN_W, tm), lambda i: (i // tiles_per_seq, 0, tail_block(i)))
        scratch = [pltpu.VMEM((3 * (len(GROUPS) - 1) * (GROUP_W // LANES), tm, LANES), F32)]
    else:
        assert tiles_per_seq == tail_tiles == 1
        tail_shape = jax.ShapeDtypeStruct((rows, 2 * ATTN_W), F32)
        tail_spec = row_spec(2 * ATTN_W)
        scratch = []
    out_shape = (jax.ShapeDtypeStruct((rows, D_MODEL), F32),
                 jax.ShapeDtypeStruct((rows, SSM_W), F32),
                 jax.ShapeDtypeStruct((rows, ATTN_W), BF16),
                 jax.ShapeDtypeStruct((rows, ATTN_W), BF16),
                 jax.ShapeDtypeStruct((rows, ATTN_W), BF16),
                 tail_shape)
    return pl.pallas_call(
        functools.partial(_ffn_in_kernel, tiles_per_seq=tiles_per_seq, tail_tiles=tail_tiles),
        grid=(n_tiles,),
        in_specs=[row_spec(D_MODEL), _const_spec((1, D_MODEL)),
                  _const_spec((D_MODEL, D_FF)), _const_spec((D_MODEL, D_FF)), _const_spec((D_FF, D_MODEL)),
                  _const_spec((1, D_MODEL)), _const_spec((D_MODEL, IN_A)),
                  _const_spec((1, ATTN_W)), _const_spec((1, ATTN_W)), _const_spec((GROUP_W, GROUP_W))],
        out_specs=(row_spec(D_MODEL), row_spec(SSM_W), row_spec(ATTN_W), row_spec(ATTN_W), row_spec(ATTN_W),
                   tail_spec),
        out_shape=out_shape,
        scratch_shapes=scratch,
        compiler_params=_params(1),
        name="ffn_in",
    )(x, w["g_ffn1"], w["w1_gate"], w["w1_up"], w["w1_down"], w["g_mix"], w["w_in_a"],
      w["g_q"], w["g_k"], w["seg_mean"])


def _s5_kernel(u_ref, h0_ref, are_ref, aim_ref, bbd_ref, cbd_ref, dsk_ref, wglu_ref, bglu_ref,
               y_ref, ht_ref, xs_ref, st_ref, *, n_rows):
    @pl.when(pl.program_id(1) == 0)
    def _():
        st_ref[...] = h0_ref[0]

    u = u_ref[0]
    ub = u.astype(BF16)
    half = N_STATE
    for n in range(2):
        xs_ref[:, n * half:(n + 1) * half] = _dot(ub[:, n * 256:(n + 1) * 256], bbd_ref[n])

    for cc in range(N_STATE // SCAN_CHUNK):
        n, sub = divmod(cc, 2)
        c_re = n * half + sub * SCAN_CHUNK
        c_im = c_re + half // 2
        ch = slice(cc * SCAN_CHUNK, (cc + 1) * SCAN_CHUNK)
        a_re = are_ref[:, ch]
        a_im = aim_ref[:, ch]

        def body(t, carry, c_re=c_re, c_im=c_im, a_re=a_re, a_im=a_im):
            s_re, s_im = carry
            row = pl.ds(t, 1)
            n_re = a_re * s_re - a_im * s_im + xs_ref[row, c_re:c_re + SCAN_CHUNK]
            n_im = a_re * s_im + a_im * s_re + xs_ref[row, c_im:c_im + SCAN_CHUNK]
            xs_ref[row, c_re:c_re + SCAN_CHUNK] = n_re
            xs_ref[row, c_im:c_im + SCAN_CHUNK] = n_im
            return n_re, n_im

        s_re, s_im = lax.fori_loop(0, n_rows, body, (st_ref[0:1, ch], st_ref[1:2, ch]),
                                   unroll=min(8, n_rows))
        st_ref[0:1, ch] = s_re
        st_ref[1:2, ch] = s_im

    y = jnp.concatenate(
        [_dot(xs_ref[:, n * half:(n + 1) * half].astype(BF16), cbd_ref[n]) for n in range(2)], axis=1)
    y = jax.nn.gelu(y + dsk_ref[...] * u)
    z = _dot(y.astype(BF16), wglu_ref[...]) + bglu_ref[...]
    y_ref[0] = y * jax.nn.sigmoid(z)
    ht_ref[0] = st_ref[...]


def _s5(u, h0, w, t_blk, n_rows):
    nb, seq, _ = u.shape
    kern = functools.partial(_s5_kernel, n_rows=n_rows)
    return pl.pallas_call(
        kern,
        grid=(nb, seq // t_blk),
        in_specs=[pl.BlockSpec((1, t_blk, SSM_W), lambda b, c: (b, c, 0)),
                  pl.BlockSpec((1, 2, N_STATE), lambda b, c: (b, 0, 0)),
                  _const_spec((1, N_STATE)), _const_spec((1, N_STATE)),
                  _const_spec((2, 256, N_STATE)), _const_spec((2, N_STATE, 256)),
                  _const_spec((1, SSM_W)), _const_spec((SSM_W, SSM_W)), _const_spec((1, SSM_W))],
        out_specs=(pl.BlockSpec((1, t_blk, SSM_W), lambda b, c: (b, c, 0)),
                   pl.BlockSpec((1, 2, N_STATE), lambda b, c: (b, 0, 0))),
        out_shape=(jax.ShapeDtypeStruct((nb, seq, SSM_W), F32),
                   jax.ShapeDtypeStruct((nb, 2, N_STATE), F32)),
        scratch_shapes=[pltpu.VMEM((t_blk, 2 * N_STATE), F32), pltpu.VMEM((2, N_STATE), F32)],
        compiler_params=_params(2),
        name="s5",
    )(u, h0, w["ab_re"], w["ab_im"], w["b_bd"], w["c_bd"], w["ssm_d"], w["w_glu"], w["b_glu"])


def _swa_kernel(q_ref, k_ref, v_ref, kp_ref, vp_ref, o_ref, l_ref, *, qb, chunk):
    first = pl.program_id(2) == 0

    def rows(ref):
        return jnp.concatenate([ref[t] for t in range(ref.shape[0])], axis=0)

    q = rows(q_ref)
    kk = jnp.concatenate([rows(kp_ref), rows(k_ref)], axis=0)
    vv = jnp.concatenate([rows(vp_ref), rows(v_ref)], axis=0)
    nk = 2 * KEYS_PER_QUERY
    row = lax.broadcasted_iota(jnp.int32, (128, nk), 0)
    col = lax.broadcasted_iota(jnp.int32, (128, nk), 1)
    band = jnp.where(col >= row, jnp.where(col <= row + KEYS_PER_QUERY, 0.0, NEG), NEG)
    band_first = band + jnp.where(col < KEYS_PER_QUERY, jnp.where(first, NEG, 0.0), 0.0)
    lane = lax.broadcasted_iota(jnp.int32, (1, GROUP_W), 1) // HEAD_DIM
    lane_l = lax.broadcasted_iota(jnp.int32, (1, LANES), 1)
    for j in range(qb // 128):
        qj = q[128 * j:128 * (j + 1)]
        kj = kk[128 * j:128 * j + nk]
        vj = vv[128 * j:128 * j + nk]
        bias = band_first if j == 0 else band
        o_acc = jnp.zeros((128, GROUP_W), F32)
        l_acc = jnp.zeros((128, LANES), F32)
        for h in range(HEADS):
            hm = lane == h
            qh = jnp.where(hm, qj, jnp.zeros_like(qj))
            s = _dot_nt(qh, kj) + bias
            m = jnp.max(s, axis=-1, keepdims=True)
            p = jnp.exp(s - m)
            den = jnp.sum(p, axis=-1, keepdims=True)
            oh = _dot(p.astype(BF16), vj)
            o_acc = jnp.where(hm, oh / den, o_acc)
            l_acc = jnp.where(lane_l == h, m + jnp.log(den), l_acc)
        o_out = o_acc.astype(BF16)
        if chunk >= 128:
            t, r0 = divmod(128 * j, chunk)
            o_ref[t, r0:r0 + 128, :] = o_out
            l_ref[t, r0:r0 + 128, :] = l_acc
        else:
            for s_ in range(128 // chunk):
                t = (128 * j) // chunk + s_
                o_ref[t] = o_out[s_ * chunk:(s_ + 1) * chunk]
                l_ref[t] = l_acc[s_ * chunk:(s_ + 1) * chunk]


def _swa(q, k, v, g, tile):
    _, dil = GROUPS[g]
    nb, seq, _ = q.shape
    n_tiles = seq // tile
    chunk = tile // dil
    n = seq // dil
    qb = min(512, n)
    tpb = qb // chunk
    n_col = ATTN_W // GROUP_W
    view = lambda a: a.reshape(nb, n_tiles, dil, chunk, a.shape[-1])
    cur = pl.BlockSpec((None, tpb, None, chunk, GROUP_W), lambda b, r, i: (b, i, r, 0, g))
    if chunk >= 128:
        prev = pl.BlockSpec((None, 1, None, 128, GROUP_W),
                            lambda b, r, i: (b, jnp.maximum(i * tpb - 1, 0), r, chunk // 128 - 1, g))
    else:
        ptiles = 128 // chunk
        prev = pl.BlockSpec((None, ptiles, None, chunk, GROUP_W),
                            lambda b, r, i: (b, jnp.maximum(i * (tpb // ptiles) - 1, 0), r, 0, g))
    out_block = lambda width: pl.BlockSpec((None, tpb, None, chunk, width), lambda b, r, i: (b, i, r, 0, 0))
    o, lse = pl.pallas_call(
        functools.partial(_swa_kernel, qb=qb, chunk=chunk),
        grid=(nb, dil, n // qb),
        in_specs=[cur, cur, cur, prev, prev],
        out_specs=(out_block(GROUP_W), out_block(LANES)),
        out_shape=(jax.ShapeDtypeStruct((nb, n_tiles, dil, chunk, GROUP_W), BF16),
                   jax.ShapeDtypeStruct((nb, n_tiles, dil, chunk, LANES), F32)),
        compiler_params=_params(3),
        name="swa_g%d" % g,
    )(view(q), view(k), view(v), view(k), view(v))
    return o.reshape(nb * seq, GROUP_W), lse.reshape(nb * seq, LANES)


def _swa_step_kernel(q_ref, kv_ref, c0_ref, c1_ref, c2_ref, n0_ref, n1_ref, n2_ref, o_ref, l_ref):
    q = q_ref[0].astype(F32)
    kv = kv_ref[0]
    row8 = lax.broadcasted_iota(jnp.int32, (8, GROUP_W), 0)
    head8 = lax.broadcasted_iota(jnp.int32, (8, GROUP_W), 1) // HEAD_DIM == row8
    diag8 = (lax.broadcasted_iota(jnp.int32, (8, LANES), 0)
             == lax.broadcasted_iota(jnp.int32, (8, LANES), 1))
    n_feat = 2 * GROUP_W
    eye = (lax.broadcasted_iota(jnp.int32, (n_feat, n_feat), 0)
           == lax.broadcasted_iota(jnp.int32, (n_feat, n_feat), 1))
    for g, (window, dil) in enumerate(GROUPS):
        c_ref = (c0_ref, c1_ref, c2_ref)[g]
        n_ref = (n0_ref, n1_ref, n2_ref)[g]
        lanes = slice(g * GROUP_W, (g + 1) * GROUP_W)
        kn = kv[:, lanes]
        vn = kv[:, ATTN_W + g * GROUP_W:ATTN_W + (g + 1) * GROUP_W]
        q8 = jnp.where(head8, jnp.broadcast_to(q[:, lanes], (8, GROUP_W)), 0.0)
        cache = c_ref[0]
        pos = lax.broadcasted_iota(jnp.int32, (8, window), 1)
        s = _dot(q8.astype(BF16), cache[:GROUP_W].astype(BF16))
        s = jnp.where(jnp.bitwise_and(pos, dil - 1) == 0, s, NEG)
        s_new = jnp.sum(q8 * kn, axis=-1, keepdims=True)
        m = jnp.maximum(jnp.max(s, axis=-1, keepdims=True), s_new)
        p = jnp.exp(s - m)
        p_new = jnp.exp(s_new - m)
        den = jnp.sum(p, axis=-1, keepdims=True) + p_new
        o8 = (_dot_nt(p.astype(BF16), cache[GROUP_W:].astype(BF16)) + p_new * vn) / den
        o_ref[0, :, lanes] = jnp.sum(jnp.where(head8, o8, 0.0), axis=0, keepdims=True).astype(BF16)
        lse = jnp.where(diag8, m + jnp.log(den), 0.0)
        l_ref[0, :, g * LANES:(g + 1) * LANES] = jnp.sum(lse, axis=0, keepdims=True)
        new_row = jnp.concatenate([kn, vn], axis=1)
        new_col = jnp.sum(jnp.where(eye, new_row, 0.0), axis=-1, keepdims=True)
        rolled = pltpu.roll(cache, window - 1, axis=1)
        last = lax.broadcasted_iota(jnp.int32, (n_feat, window), 1) == window - 1
        n_ref[0] = jnp.where(last, new_col, rolled)


def _swa_step(q, kvt, caches):
    nb = q.shape[0]
    row3 = lambda width: pl.BlockSpec((1, 1, width), lambda b: (b, 0, 0))
    cache_specs = [pl.BlockSpec((1, 2 * GROUP_W, win), lambda b: (b, 0, 0)) for win, _ in GROUPS]
    outs = pl.pallas_call(
        _swa_step_kernel,
        grid=(nb,),
        in_specs=[row3(ATTN_W), row3(2 * ATTN_W)] + cache_specs,
        out_specs=tuple(cache_specs) + (row3(ATTN_W), row3(3 * LANES)),
        out_shape=tuple(jax.ShapeDtypeStruct(c.shape, F32) for c in caches)
                  + (jax.ShapeDtypeStruct((nb, 1, ATTN_W), BF16),
                     jax.ShapeDtypeStruct((nb, 1, 3 * LANES), F32)),
        compiler_params=_params(1),
        name="swa_step",
    )(q.reshape(nb, 1, ATTN_W), kvt.reshape(nb, 1, 2 * ATTN_W), *caches)
    return list(outs[:3]), outs[3].reshape(nb, ATTN_W), outs[4].reshape(nb, 3 * LANES)


def _position_order(o_ref, l_ref, dil, stage_ref, nat_ref, slab0):
    tm = o_ref.shape[0]
    n = tm // dil
    stage_ref[slab0] = o_ref[:, :LANES].astype(F32)
    stage_ref[slab0 + 1] = o_ref[:, LANES:].astype(F32)
    stage_ref[slab0 + 2] = l_ref[...]
    for s in range(slab0, slab0 + 3):
        for a in range(n):
            nat_ref[s, dil * a:dil * (a + 1), :] = stage_ref[s, pl.ds(a, dil, stride=n), :]
    o = jnp.concatenate([nat_ref[slab0], nat_ref[slab0 + 1]], axis=1)
    return o, nat_ref[slab0 + 2]


def _mix_ffn_kernel(x1_ref, ys_ref, o0_ref, o1_ref, o2_ref, l0_ref, l1_ref, l2_ref,
                    gm_ref, wgt_ref, wsp_ref, wap_ref, wo_ref, seget_ref,
                    g2_ref, wg_ref, wu_ref, wd_ref, y_ref, *scratch):
    x1 = x1_ref[...]
    h = _rms(x1, gm_ref[...]).astype(BF16)
    gates = jax.nn.sigmoid(_dot(h, wgt_ref[...]))
    os_ = [o0_ref[...].astype(F32)]
    ls = [l0_ref[...]]
    for g, (o_ref, l_ref) in enumerate(((o1_ref, l1_ref), (o2_ref, l2_ref)), start=1):
        if scratch:
            o, l = _position_order(o_ref, l_ref, GROUPS[g][1], scratch[0], scratch[1], 3 * (g - 1))
        else:
            o, l = o_ref[...].astype(F32), l_ref[...]
        os_.append(o)
        ls.append(l)
    l_top = jnp.maximum(jnp.maximum(ls[0], ls[1]), ls[2])
    es = [jnp.exp(l - l_top) for l in ls]
    inv = 1.0 / (es[0] + es[1] + es[2])
    seget = seget_ref[...]
    y_attn = None
    for e, o in zip(es, os_):
        t = _split_dot(e * inv, seget) * o
        y_attn = t if y_attn is None else y_attn + t
    mixed = (gates[:, :D_MODEL] * _dot(ys_ref[...].astype(BF16), wsp_ref[...])
             + gates[:, D_MODEL:] * _dot(y_attn.astype(BF16), wap_ref[...]))
    x2 = x1 + _dot(mixed.astype(BF16), wo_ref[...])
    y_ref[...] = _ffn(x2, g2_ref, wg_ref, wu_ref, wd_ref)


def _mix_ffn(x1, y_ssm, os_, ls_, w, tm, residue_major):
    rows = x1.shape[0]
    row_spec = lambda width: pl.BlockSpec((tm, width), lambda i: (i, 0))
    n_slabs = 3 * (len(GROUPS) - 1)
    scratch = [pltpu.VMEM((n_slabs, tm, LANES), F32)] * 2 if residue_major else []
    return pl.pallas_call(
        _mix_ffn_kernel,
        grid=(rows // tm,),
        in_specs=[row_spec(D_MODEL), row_spec(SSM_W)] + [row_spec(GROUP_W)] * 3 + [row_spec(LANES)] * 3
                 + [_const_spec((1, D_MODEL)), _const_spec((D_MODEL, 2 * D_MODEL)),
                    _const_spec((SSM_W, D_MODEL)), _const_spec((GROUP_W, D_MODEL)),
                    _const_spec((D_MODEL, D_MODEL)), _const_spec((LANES, GROUP_W)),
                    _const_spec((1, D_MODEL)), _const_spec((D_MODEL, D_FF)), _const_spec((D_MODEL, D_FF)),
                    _const_spec((D_FF, D_MODEL))],
        out_specs=row_spec(D_MODEL),
        out_shape=jax.ShapeDtypeStruct((rows, D_MODEL), F32),
        scratch_shapes=scratch,
        compiler_params=_params(1),
        name="mix_ffn",
    )(x1, y_ssm, *os_, *ls_, w["g_mix"], w["w_gates"], w["w_ssm_proj"], w["w_attn_proj"], w["w_o"],
      w["seg_et"], w["g_ffn2"], w["w2_gate"], w["w2_up"], w["w2_down"])


def _prepare_weights(g_ffn1, w1_gate, w1_up, w1_down, g_mix, w_in, g_q, g_k,
                     ssm_a_re, ssm_a_im, ssm_log_dt, ssm_b_re, ssm_b_im, ssm_c_re, ssm_c_im,
                     ssm_d, w_glu, b_glu, w_ssm_proj, w_attn_proj, w_o, g_ffn2, w2_gate, w2_up, w2_down):
    w = {}
    row = lambda a: a.reshape(1, -1).astype(F32)
    w["g_ffn1"], w["g_mix"], w["g_ffn2"] = row(g_ffn1), row(g_mix), row(g_ffn2)
    for name, a in (("w1_gate", w1_gate), ("w1_up", w1_up), ("w1_down", w1_down),
                    ("w2_gate", w2_gate), ("w2_up", w2_up), ("w2_down", w2_down),
                    ("w_glu", w_glu), ("w_ssm_proj", w_ssm_proj), ("w_attn_proj", w_attn_proj),
                    ("w_o", w_o)):
        w[name] = a.astype(BF16)
    w["w_in_a"] = w_in[:, :IN_A].astype(BF16)
    w["w_gates"] = w_in[:, IN_A:].astype(BF16)
    per_head = lambda g: jnp.broadcast_to(g[:, None, :], (len(GROUPS), HEADS, HEAD_DIM)).reshape(1, ATTN_W)
    w["g_q"], w["g_k"] = per_head(g_q.astype(F32)), per_head(g_k.astype(F32))
    head_of_lane = jnp.arange(GROUP_W) // HEAD_DIM
    w["seg_mean"] = ((head_of_lane[:, None] == head_of_lane[None, :]) / HEAD_DIM).astype(BF16)
    w["seg_et"] = (jnp.arange(LANES)[:, None] == head_of_lane[None, :]).astype(BF16)
    w["ssm_d"], w["b_glu"] = row(ssm_d), row(b_glu)

    ab_re, ab_im, bb_re, bb_im = _ssm_prep(ssm_a_re.astype(F32), ssm_a_im.astype(F32),
                                           ssm_log_dt.astype(F32), ssm_b_re.astype(F32),
                                           ssm_b_im.astype(F32))
    w["ab_re"], w["ab_im"] = ab_re.reshape(1, N_STATE), ab_im.reshape(1, N_STATE)
    eye = jnp.eye(16, dtype=F32)

    def b_half(bb, n):
        blk = bb[:, 16 * n:16 * (n + 1)]
        return jnp.einsum("cgp,gh->gchp", blk, eye).reshape(256, 1024)

    def c_half(cc, n):
        blk = cc[16 * n:16 * (n + 1)].astype(F32)
        return jnp.einsum("gcp,gh->gphc", blk, eye).reshape(1024, 256)

    w["b_bd"] = jnp.stack([jnp.concatenate([b_half(bb_re, n), b_half(bb_im, n)], axis=1)
                           for n in range(2)]).astype(BF16)
    w["c_bd"] = jnp.stack([jnp.concatenate([c_half(ssm_c_re, n), -c_half(ssm_c_im, n)], axis=0)
                           for n in range(2)]).astype(BF16)
    return w


def _layer(x, h0, caches, w):
    nb, seq, _ = x.shape
    rows = nb * seq
    xf = x.reshape(rows, D_MODEL)
    if caches is None:
        tm = TILE
        assert seq % tm == 0
        tail = min(GROUPS[-1][0], seq)
        x1, u, q, k, v, kvt = _ffn_in(xf, w, tm, seq // tm, tail // tm, True)
        h0 = jnp.zeros((nb, 2, N_STATE), F32)
        y_ssm, h_t = _s5(u.reshape(nb, seq, SSM_W), h0, w, tm, tm)
        y_ssm = y_ssm.reshape(rows, SSM_W)
        q3, k3, v3 = (a.reshape(nb, seq, ATTN_W) for a in (q, k, v))
        os_, ls_ = zip(*[_swa(q3, k3, v3, g, tm) for g in range(len(GROUPS))])
        new_caches = []
        for g, (window, _) in enumerate(GROUPS):
            keep = min(window, seq)
            kg = kvt[:, g * GROUP_W:(g + 1) * GROUP_W, tail - keep:]
            vg = kvt[:, ATTN_W + g * GROUP_W:ATTN_W + (g + 1) * GROUP_W, tail - keep:]
            new_caches.append(jnp.concatenate([kg, vg], axis=1))
        y = _mix_ffn(x1, y_ssm, os_, ls_, w, tm, True)
    else:
        assert seq == 1
        x1, u, q, k, v, kvt = _ffn_in(xf, w, rows, 1, 1, False)
        u_pad = jnp.pad(u.reshape(nb, 1, SSM_W), ((0, 0), (0, 7), (0, 0)))
        y_ssm, h_t = _s5(u_pad, h0, w, 8, 1)
        y_ssm = y_ssm[:, 0]
        new_caches, o_all, l_all = _swa_step(q, kvt, caches)
        os_ = [o_all[:, g * GROUP_W:(g + 1) * GROUP_W] for g in range(len(GROUPS))]
        ls_ = [l_all[:, g * LANES:(g + 1) * LANES] for g in range(len(GROUPS))]
        y = _mix_ffn(x1, y_ssm, os_, ls_, w, rows, False)
    return y.reshape(nb, seq, D_MODEL), h_t, new_caches


def _to_features_major(c):
    nb, win = c.shape[:2]
    return jnp.transpose(c, (0, 2, 3, 4, 1)).reshape(nb, 2 * GROUP_W, win)


def _to_window_buffer(c):
    nb, _, win = c.shape
    return jnp.transpose(c.reshape(nb, 2, HEADS, HEAD_DIM, win), (0, 4, 1, 2, 3))[None]


def kernel(x_prompt, x_sample, cache_kv_w128, cache_kv_w512, cache_kv_w2048, state_ssm_re, state_ssm_im, g_ffn1, w1_gate, w1_up, w1_down, g_mix, w_in, g_q, g_k, ssm_a_re, ssm_a_im, ssm_log_dt, ssm_b_re, ssm_b_im, ssm_c_re, ssm_c_im, ssm_d, w_glu, b_glu, w_ssm_proj, w_attn_proj, w_o, g_ffn2, w2_gate, w2_up, w2_down):
    layer_weights = (g_ffn1, w1_gate, w1_up, w1_down, g_mix, w_in, g_q, g_k,
                     ssm_a_re, ssm_a_im, ssm_log_dt, ssm_b_re, ssm_b_im, ssm_c_re, ssm_c_im,
                     ssm_d, w_glu, b_glu, w_ssm_proj, w_attn_proj, w_o, g_ffn2, w2_gate, w2_up, w2_down)
    depth = g_ffn1.shape[0]
    assert depth == 1, "window caches of deeper layers would need the previous layer's outputs"
    w = _prepare_weights(*(a[0] for a in layer_weights))
    nb_s = x_sample.shape[0]
    sdt = state_ssm_re.dtype

    def split_state(h_t):
        return (h_t[:, 0].reshape(1, -1, SSM_GROUPS, SSM_P).astype(sdt),
                h_t[:, 1].reshape(1, -1, SSM_GROUPS, SSM_P).astype(sdt))

    y_p, h_p, kv_p = _layer(x_prompt, None, None, w)

    caches = [_to_features_major(c[0]) for c in (cache_kv_w128, cache_kv_w512, cache_kv_w2048)]
    for c, (window, _) in zip(caches, GROUPS):
        assert c.shape[2] == window, "the sample group's caches must hold a full window"
    h0_s = jnp.stack([state_ssm_re[0].reshape(nb_s, N_STATE),
                      state_ssm_im[0].reshape(nb_s, N_STATE)], axis=1).astype(F32)
    y_s, h_s, kv_s = _layer(x_sample, h0_s, caches, w)

    return ((y_p, y_s) + tuple(_to_window_buffer(c) for c in kv_p) + split_state(h_p)
            + tuple(_to_window_buffer(c) for c in kv_s) + split_state(h_s))
```

```python
import functools

import jax
import jax.numpy as jnp
from jax import lax
from jax.experimental import pallas as pl
from jax.experimental.pallas import tpu as pltpu

F32 = jnp.float32
BF16 = jnp.bfloat16

D_MODEL = 1024
D_FF = 2816
HEAD_DIM = 64
HEADS = 4
GROUPS = ((128, 1), (512, 4), (2048, 16))
KEYS_PER_QUERY = 128
GROUP_W = HEADS * HEAD_DIM
ATTN_W = len(GROUPS) * GROUP_W
SSM_W = 512
SSM_GROUPS = 32
SSM_CH = 16
SSM_P = 64
N_STATE = SSM_GROUPS * SSM_P
IN_A = SSM_W + 3 * ATTN_W
RMS_EPS = 1e-6
NEG = -1e30
LANES = 128

TILE = 512
FF_CHUNKS = ((0, 1024), (1024, 1024), (2048, 768))
BLK = 8
NQ = SSM_W // LANES
QS = N_STATE // NQ
S5_T = 1024
VMEM_LIMIT = 56 * 1024 * 1024


def _const_spec(shape):
    nd = len(shape)
    return pl.BlockSpec(shape, lambda *_: (0,) * nd, pipeline_mode=pl.Buffered(1))


def _params(n_grid):
    return pltpu.CompilerParams(dimension_semantics=("arbitrary",) * n_grid,
                                vmem_limit_bytes=VMEM_LIMIT)


def _rms(x, g):
    ms = jnp.mean(x * x, axis=-1, keepdims=True)
    return x * lax.rsqrt(ms + RMS_EPS) * g


def _dot(a, b):
    return jnp.dot(a, b, preferred_element_type=F32)


def _dot_nt(a, b):
    return lax.dot_general(a, b, (((1,), (1,)), ((), ())), preferred_element_type=F32)


def _split_dot(x, m):
    hi = x.astype(BF16)
    lo = (x - hi.astype(F32)).astype(BF16)
    return _dot(hi, m) + _dot(lo, m)


def _ffn(x, g_ref, wg_ref, wu_ref, wd_ref):
    xn = _rms(x, g_ref[...]).astype(BF16)
    acc = None
    for f0, fw in FF_CHUNKS:
        hg = _dot(xn, wg_ref[:, f0:f0 + fw])
        hu = _dot(xn, wu_ref[:, f0:f0 + fw])
        a = (hg * jax.nn.sigmoid(hg) * hu).astype(BF16)
        d = _dot(a, wd_ref[f0:f0 + fw, :])
        acc = d if acc is None else acc + d
    return x + 0.5 * acc


def _split3_dot_nt(a, b):
    a_hi = a.astype(BF16)
    a_lo = (a - a_hi.astype(F32)).astype(BF16)
    b_hi = b.astype(BF16)
    b_lo = (b - b_hi.astype(F32)).astype(BF16)
    return _dot_nt(a_hi, b_hi) + _dot_nt(a_hi, b_lo) + _dot_nt(a_lo, b_hi)


def _prep_kernel(are_ref, aim_ref, ldt_ref, bre_ref, bim_ref, cre_ref, cim_ref,
                 pw_ref, xb_ref, k8_ref, gc_ref):
    a_re = are_ref[...]
    a_im = aim_ref[...]
    dt = jnp.exp(ldt_ref[...])
    mag = jnp.exp(a_re * dt)
    ab_re = mag * jnp.cos(a_im * dt)
    ab_im = mag * jnp.sin(a_im * dt)
    inv = 1.0 / (a_re * a_re + a_im * a_im)
    f_re = ((ab_re - 1.0) * a_re + ab_im * a_im) * inv
    f_im = (ab_im * a_re - (ab_re - 1.0) * a_im) * inv
    b_re = bre_ref[...]
    b_im = bim_ref[...]
    bb_re = f_re * b_re - f_im * b_im
    bb_im = f_re * b_im + f_im * b_re
    c_re = cre_ref[...]
    c_im = cim_ref[...]
    gc = SSM_GROUPS * SSM_CH
    same_group = (lax.broadcasted_iota(jnp.int32, (gc, gc), 0) // SSM_CH
                  == lax.broadcasted_iota(jnp.int32, (gc, gc), 1) // SSM_CH)
    p_re = jnp.ones_like(ab_re)
    p_im = jnp.zeros_like(ab_re)
    for i in range(BLK + 1):
        if i == 1:
            pw_ref[0, 0] = p_re
            pw_ref[0, 1] = p_im
        if i == BLK:
            pw_ref[1, 0] = p_re
            pw_ref[1, 1] = p_im
        if i >= 1:
            gc_ref[i - 1, 0] = c_re * p_re - c_im * p_im
            gc_ref[i - 1, 1] = c_re * p_im + c_im * p_re
        if i < BLK:
            x_re = p_re * bb_re - p_im * bb_im
            x_im = p_re * bb_im + p_im * bb_re
            xb_ref[i, 0] = x_re
            xb_ref[i, 1] = x_im
            kk = _split3_dot_nt(x_re, c_re) - _split3_dot_nt(x_im, c_im)
            k8_ref[i] = jnp.where(same_group, kk, 0.0)
        p_re, p_im = p_re * ab_re - p_im * ab_im, p_re * ab_im + p_im * ab_re


def _ssm_prep(a_re, a_im, log_dt, b_re, b_im, c_re, c_im):
    gc = SSM_GROUPS * SSM_CH
    rep = lambda a: jnp.repeat(a.astype(F32), SSM_CH, axis=0)
    b_rows = lambda b: jnp.transpose(b.astype(F32), (0, 2, 1)).reshape(gc, SSM_P)
    c_rows = lambda c: c.astype(F32).reshape(gc, SSM_P)
    return pl.pallas_call(
        _prep_kernel,
        out_shape=(jax.ShapeDtypeStruct((2, 2, gc, SSM_P), F32),
                   jax.ShapeDtypeStruct((BLK, 2, gc, SSM_P), F32),
                   jax.ShapeDtypeStruct((BLK, gc, gc), F32),
                   jax.ShapeDtypeStruct((BLK, 2, gc, SSM_P), F32)),
        name="ssm_prep",
    )(rep(a_re), rep(a_im), rep(log_dt.reshape(SSM_GROUPS, 1)), b_rows(b_re), b_rows(b_im),
      c_rows(c_re), c_rows(c_im))


def _head_norm(t, g, seg):
    sq = t * t
    ms = jnp.concatenate(
        [_split_dot(sq[:, s * GROUP_W:(s + 1) * GROUP_W], seg) for s in range(len(GROUPS))], axis=1)
    return t * lax.rsqrt(ms + RMS_EPS) * g


def _store_attn_rows(val, out_ref, perm_ref, slab0):
    tm = val.shape[0]
    out_ref[:, :GROUP_W] = val[:, :GROUP_W].astype(BF16)
    for g in range(1, len(GROUPS)):
        dil = GROUPS[g][1]
        n = tm // dil
        for half in range(GROUP_W // LANES):
            c0 = g * GROUP_W + half * LANES
            col = val[:, c0:c0 + LANES]
            if perm_ref is None:
                out_ref[:, c0:c0 + LANES] = col.astype(BF16)
                continue
            slab = slab0 + (g - 1) * (GROUP_W // LANES) + half
            perm_ref[slab] = col
            for r in range(dil):
                out_ref[r * n:(r + 1) * n, c0:c0 + LANES] = (
                    perm_ref[slab, pl.ds(r, n, stride=dil), :].astype(BF16))


def _ffn_in_kernel(x_ref, g1_ref, wg_ref, wu_ref, wd_ref, gm_ref, win_ref, gq_ref, gk_ref, seg_ref,
                   x1_ref, u_ref, q_ref, k_ref, v_ref, kvt_ref, *scratch, tiles_per_seq, tail_tiles):
    perm_ref = scratch[0] if scratch else None
    x1 = _ffn(x_ref[...], g1_ref, wg_ref, wu_ref, wd_ref)
    x1_ref[...] = x1
    h = _rms(x1, gm_ref[...]).astype(BF16)
    proj = _dot(h, win_ref[...])
    u_ref[...] = proj[:, :SSM_W]
    seg = seg_ref[...]
    q = _head_norm(proj[:, SSM_W:SSM_W + ATTN_W], gq_ref[...], seg)
    k = _head_norm(proj[:, SSM_W + ATTN_W:SSM_W + 2 * ATTN_W], gk_ref[...], seg)
    v = proj[:, SSM_W + 2 * ATTN_W:]
    slabs = (len(GROUPS) - 1) * (GROUP_W // LANES)
    _store_attn_rows(q * HEAD_DIM ** -0.5, q_ref, perm_ref, 0)
    _store_attn_rows(k, k_ref, perm_ref, slabs)
    _store_attn_rows(v, v_ref, perm_ref, 2 * slabs)
    if perm_ref is None:
        kvt_ref[:, :ATTN_W] = k
        kvt_ref[:, ATTN_W:] = v
    else:
        @pl.when(pl.program_id(0) % tiles_per_seq >= tiles_per_seq - tail_tiles)
        def _():
            kvt_ref[0, :ATTN_W, :] = k.T
            kvt_ref[0, ATTN_W:, :] = v.T


def _ffn_in(x, w, tm, tiles_per_seq, tail_tiles, prompt_form):
    rows = x.shape[0]
    n_tiles = rows // tm
    n_seq = n_tiles // tiles_per_seq
    row_spec = lambda width: pl.BlockSpec((tm, width), lambda i: (i, 0))

    def tail_block(i):
        return jnp.maximum(i % tiles_per_seq - (tiles_per_seq - tail_tiles), 0)

    if prompt_form:
        tail_shape = jax.ShapeDtypeStruct((n_seq, 2 * ATTN_W, tail_tiles * tm), F32)
        tail_spec = pl.BlockSpec((1, 2 * ATTN_W, tm), lambda i: (i // tiles_per_seq, 0, tail_block(i)))
        scratch = [pltpu.VMEM((3 * (len(GROUPS) - 1) * (GROUP_W // LANES), tm, LANES), F32)]
    else:
        assert tiles_per_seq == tail_tiles == 1
        tail_shape = jax.ShapeDtypeStruct((rows, 2 * ATTN_W), F32)
        tail_spec = row_spec(2 * ATTN_W)
        scratch = []
    out_shape = (jax.ShapeDtypeStruct((rows, D_MODEL), F32),
                 jax.ShapeDtypeStruct((rows, SSM_W), F32),
                 jax.ShapeDtypeStruct((rows, ATTN_W), BF16),
                 jax.ShapeDtypeStruct((rows, ATTN_W), BF16),
                 jax.ShapeDtypeStruct((rows, ATTN_W), BF16),
                 tail_shape)
    return pl.pallas_call(
        functools.partial(_ffn_in_kernel, tiles_per_seq=tiles_per_seq, tail_tiles=tail_tiles),
        grid=(n_tiles,),
        in_specs=[row_spec(D_MODEL), _const_spec((1, D_MODEL)),
                  _const_spec((D_MODEL, D_FF)), _const_spec((D_MODEL, D_FF)), _const_spec((D_FF, D_MODEL)),
                  _const_spec((1, D_MODEL)), _const_spec((D_MODEL, IN_A)),
                  _const_spec((1, ATTN_W)), _const_spec((1, ATTN_W)), _const_spec((GROUP_W, GROUP_W))],
        out_specs=(row_spec(D_MODEL), row_spec(SSM_W), row_spec(ATTN_W), row_spec(ATTN_W), row_spec(ATTN_W),
                   tail_spec),
        out_shape=out_shape,
        scratch_shapes=scratch,
        compiler_params=_params(1),
        name="ffn_in",
    )(x, w["g_ffn1"], w["w1_gate"], w["w1_up"], w["w1_down"], w["g_mix"], w["w_in_a"],
      w["g_q"], w["g_k"], w["seg_mean"])


def _glu_out(y, u, dsk_ref, wglu_ref, bglu_ref):
    y = jax.nn.gelu(y + dsk_ref[...] * u)
    z = _dot(y.astype(BF16), wglu_ref[...]) + bglu_ref[...]
    return (y * jax.nn.sigmoid(z)).astype(BF16)


def _s5_kernel(u0_ref, u1_ref, u2_ref, u3_ref, a8re_ref, a8im_ref, k8_ref, w7_ref, gq_ref,
               dsk_ref, wglu_ref, bglu_ref, y_ref, hre_ref, him_ref,
               ls_ref, yq_ref, y2s_ref, sre_ref, sim_ref):
    u_refs = (u0_ref, u1_ref, u2_ref, u3_ref)
    nb, t_blk, _ = u0_ref.shape
    nblk = t_blk // BLK
    qw = 2 * QS

    @pl.when(pl.program_id(0) == 0)
    def _():
        sre_ref[...] = jnp.zeros_like(sre_ref)
        sim_ref[...] = jnp.zeros_like(sim_ref)

    sub = lax.broadcasted_iota(jnp.int32, (nblk, BLK, LANES), 1)
    for q in range(NQ):
        ublks = []
        for b in range(nb):
            uq = u_refs[q][b]
            z3 = _dot(uq.astype(BF16), k8_ref[q]).reshape(nblk, BLK, BLK * LANES)
            acc = z3[:, :, :LANES]
            for i in range(1, BLK):
                zi = z3[:, :, LANES * i:LANES * (i + 1)]
                acc = acc + jnp.where(sub >= i, pltpu.roll(zi, i, axis=1), 0.0)
            yq_ref[b * NQ + q] = acc.reshape(t_blk, LANES)
            ublks.append(jnp.concatenate(
                [u_refs[q][b, pl.ds(r, nblk, stride=BLK), :] for r in range(BLK)], axis=1))
        ls = _dot(jnp.concatenate(ublks, axis=0).astype(BF16), w7_ref[q])
        for b in range(nb):
            ls_ref[b, :, q * qw:(q + 1) * qw] = ls[b * nblk:(b + 1) * nblk]

    for q in range(NQ):
        ch = slice(q * QS, (q + 1) * QS)
        c_re = q * qw
        c_im = c_re + QS
        a_re = a8re_ref[:, ch]
        a_im = a8im_ref[:, ch]

        def body(k, carry, c_re=c_re, c_im=c_im, a_re=a_re, a_im=a_im):
            row = pl.ds(k, 1)
            out = []
            for b in range(nb):
                s_re, s_im = carry[2 * b], carry[2 * b + 1]
                l_re = ls_ref[b, row, c_re:c_re + QS]
                l_im = ls_ref[b, row, c_im:c_im + QS]
                ls_ref[b, row, c_re:c_re + QS] = s_re
                ls_ref[b, row, c_im:c_im + QS] = s_im
                out.append(a_re * s_re - a_im * s_im + l_re)
                out.append(a_re * s_im + a_im * s_re + l_im)
            return tuple(out)

        init = []
        for b in range(nb):
            init += [sre_ref[b:b + 1, ch], sim_ref[b:b + 1, ch]]
        fin = lax.fori_loop(0, nblk, body, tuple(init), unroll=8)
        for b in range(nb):
            sre_ref[b:b + 1, ch] = fin[2 * b]
            sim_ref[b:b + 1, ch] = fin[2 * b + 1]

    for q in range(NQ):
        sc = jnp.concatenate([ls_ref[b, :, q * qw:(q + 1) * qw] for b in range(nb)], axis=0)
        y2 = _dot(sc.astype(BF16), gq_ref[q])
        for b in range(nb):
            for r in range(BLK):
                y2s_ref[pl.ds(r, nblk, stride=BLK), :] = y2[b * nblk:(b + 1) * nblk, LANES * r:LANES * (r + 1)]
            yq_ref[b * NQ + q] = yq_ref[b * NQ + q] + y2s_ref[...]

    for b in range(nb):
        y = jnp.concatenate([yq_ref[b * NQ + q] for q in range(NQ)], axis=1)
        u = jnp.concatenate([u_refs[q][b] for q in range(NQ)], axis=1)
        y_ref[b] = _glu_out(y, u, dsk_ref, wglu_ref, bglu_ref)
    hre_ref[...] = sre_ref[...]
    him_ref[...] = sim_ref[...]


def _s5(u, w, t_blk):
    nb, seq, _ = u.shape
    nblk = t_blk // BLK
    u_specs = [pl.BlockSpec((nb, t_blk, LANES), lambda c, q=q: (0, c, q)) for q in range(NQ)]
    state = pl.BlockSpec((nb, N_STATE), lambda c: (0, 0))
    return pl.pallas_call(
        _s5_kernel,
        grid=(seq // t_blk,),
        in_specs=u_specs + [_const_spec((1, N_STATE)), _const_spec((1, N_STATE)),
                            _const_spec((NQ, LANES, BLK * LANES)),
                            _const_spec((NQ, BLK * LANES, 2 * QS)), _const_spec((NQ, 2 * QS, BLK * LANES)),
                            _const_spec((1, SSM_W)), _const_spec((SSM_W, SSM_W)), _const_spec((1, SSM_W))],
        out_specs=(pl.BlockSpec((nb, t_blk, SSM_W), lambda c: (0, c, 0)), state, state),
        out_shape=(jax.ShapeDtypeStruct((nb, seq, SSM_W), BF16),
                   jax.ShapeDtypeStruct((nb, N_STATE), F32), jax.ShapeDtypeStruct((nb, N_STATE), F32)),
        scratch_shapes=[pltpu.VMEM((nb, nblk, NQ * 2 * QS), F32),
                        pltpu.VMEM((nb * NQ, t_blk, LANES), F32),
                        pltpu.VMEM((t_blk, LANES), F32),
                        pltpu.VMEM((nb, N_STATE), F32), pltpu.VMEM((nb, N_STATE), F32)],
        compiler_params=_params(1),
        name="s5",
    )(u, u, u, u, w["a8_re"], w["a8_im"], w["k8"], w["w7"], w["gq"], w["ssm_d"], w["w_glu"], w["b_glu"])


def _s5_step_kernel(u_ref, hre_ref, him_ref, are_ref, aim_ref, bbd_ref, cbd_ref,
                    dsk_ref, wglu_ref, bglu_ref, y_ref, nre_ref, nim_ref):
    u = u_ref[...]
    ub = u.astype(BF16)
    half = N_STATE // 2
    ys = []
    for n in range(2):
        ch = slice(n * half, (n + 1) * half)
        x = _dot(ub[:, n * 256:(n + 1) * 256], bbd_ref[n])
        a_re, a_im = are_ref[:, ch], aim_ref[:, ch]
        s_re, s_im = hre_ref[:, ch], him_ref[:, ch]
        n_re = a_re * s_re - a_im * s_im + x[:, :half]
        n_im = a_re * s_im + a_im * s_re + x[:, half:]
        nre_ref[:, ch] = n_re
        nim_ref[:, ch] = n_im
        ys.append(_dot(jnp.concatenate([n_re, n_im], axis=1).astype(BF16), cbd_ref[n]))
    y_ref[...] = _glu_out(jnp.concatenate(ys, axis=1), u, dsk_ref, wglu_ref, bglu_ref)


def _s5_step(u, h_re, h_im, w):
    rows = u.shape[0]
    return pl.pallas_call(
        _s5_step_kernel,
        out_shape=(jax.ShapeDtypeStruct((rows, SSM_W), BF16),
                   jax.ShapeDtypeStruct((rows, N_STATE), F32), jax.ShapeDtypeStruct((rows, N_STATE), F32)),
        compiler_params=pltpu.CompilerParams(vmem_limit_bytes=VMEM_LIMIT),
        name="s5_step",
    )(u, h_re, h_im, w["ab_re"], w["ab_im"], w["b_bd"], w["c_bd"], w["ssm_d"], w["w_glu"], w["b_glu"])


def _swa_kernel(q_ref, k_ref, v_ref, kp_ref, vp_ref, o_ref, l_ref, *, qb, chunk):
    first = pl.program_id(2) == 0

    def rows(ref):
        return jnp.concatenate([ref[t] for t in range(ref.shape[0])], axis=0)

    q = rows(q_ref)
    kk = jnp.concatenate([rows(kp_ref), rows(k_ref)], axis=0)
    vv = jnp.concatenate([rows(vp_ref), rows(v_ref)], axis=0)
    nk = 2 * KEYS_PER_QUERY
    row = lax.broadcasted_iota(jnp.int32, (128, nk), 0)
    col = lax.broadcasted_iota(jnp.int32, (128, nk), 1)
    band = jnp.where(col >= row, jnp.where(col <= row + KEYS_PER_QUERY, 0.0, NEG), NEG)
    band_first = band + jnp.where(col < KEYS_PER_QUERY, jnp.where(first, NEG, 0.0), 0.0)
    lane = lax.broadcasted_iota(jnp.int32, (1, GROUP_W), 1) // HEAD_DIM
    lane_l = lax.broadcasted_iota(jnp.int32, (1, LANES), 1)
    for j in range(qb // 128):
        qj = q[128 * j:128 * (j + 1)]
        kj = kk[128 * j:128 * j + nk]
        vj = vv[128 * j:128 * j + nk]
        bias = band_first if j == 0 else band
        qs = jnp.concatenate([jnp.where(lane == h, qj, jnp.zeros_like(qj)) for h in range(HEADS)], axis=0)
        s = (_dot_nt(qs, kj).reshape(HEADS, 128, nk) + bias[None]).reshape(HEADS * 128, nk)
        m = jnp.max(s, axis=-1, keepdims=True)
        p = jnp.exp(s - m)
        den = jnp.sum(p, axis=-1, keepdims=True)
        on = _dot(p.astype(BF16), vj) * (1.0 / den)
        lse = m + jnp.log(den)
        o_acc = jnp.zeros((128, GROUP_W), F32)
        l_acc = jnp.zeros((128, LANES), F32)
        for h in range(HEADS):
            rows_h = slice(128 * h, 128 * (h + 1))
            o_acc = jnp.where(lane == h, on[rows_h], o_acc)
            l_acc = jnp.where(lane_l == h, lse[rows_h], l_acc)
        o_out = o_acc.astype(BF16)
        if chunk >= 128:
            t, r0 = divmod(128 * j, chunk)
            o_ref[t, r0:r0 + 128, :] = o_out
            l_ref[t, r0:r0 + 128, :] = l_acc
        else:
            for s_ in range(128 // chunk):
                t = (128 * j) // chunk + s_
                o_ref[t] = o_out[s_ * chunk:(s_ + 1) * chunk]
                l_ref[t] = l_acc[s_ * chunk:(s_ + 1) * chunk]


def _swa(q, k, v, g, tile):
    _, dil = GROUPS[g]
    nb, seq, _ = q.shape
    n_tiles = seq // tile
    chunk = tile // dil
    n = seq // dil
    qb = min(512, n)
    tpb = qb // chunk
    view = lambda a: a.reshape(nb, n_tiles, dil, chunk, a.shape[-1])
    cur = pl.BlockSpec((None, tpb, None, chunk, GROUP_W), lambda b, r, i: (b, i, r, 0, g))
    if chunk >= 128:
        prev = pl.BlockSpec((None, 1, None, 128, GROUP_W),
                            lambda b, r, i: (b, jnp.maximum(i * tpb - 1, 0), r, chunk // 128 - 1, g))
    else:
        ptiles = 128 // chunk
        prev = pl.BlockSpec((None, ptiles, None, chunk, GROUP_W),
                            lambda b, r, i: (b, jnp.maximum(i * (tpb // ptiles) - 1, 0), r, 0, g))
    out_block = lambda width: pl.BlockSpec((None, tpb, None, chunk, width), lambda b, r, i: (b, i, r, 0, 0))
    o, lse = pl.pallas_call(
        functools.partial(_swa_kernel, qb=qb, chunk=chunk),
        grid=(nb, dil, n // qb),
        in_specs=[cur, cur, cur, prev, prev],
        out_specs=(out_block(GROUP_W), out_block(LANES)),
        out_shape=(jax.ShapeDtypeStruct((nb, n_tiles, dil, chunk, GROUP_W), BF16),
                   jax.ShapeDtypeStruct((nb, n_tiles, dil, chunk, LANES), F32)),
        compiler_params=_params(3),
        name="swa_g%d" % g,
    )(view(q), view(k), view(v), view(k), view(v))
    return o.reshape(nb * seq, GROUP_W), lse.reshape(nb * seq, LANES)


def _swa_step_kernel(q_ref, kv_ref, c0_ref, c1_ref, c2_ref, n0_ref, n1_ref, n2_ref, o_ref, l_ref):
    q = q_ref[0].astype(F32)
    kv = kv_ref[0]
    row8 = lax.broadcasted_iota(jnp.int32, (8, GROUP_W), 0)
    head8 = lax.broadcasted_iota(jnp.int32, (8, GROUP_W), 1) // HEAD_DIM == row8
    diag8 = (lax.broadcasted_iota(jnp.int32, (8, LANES), 0)
             == lax.broadcasted_iota(jnp.int32, (8, LANES), 1))
    n_feat = 2 * GROUP_W
    eye = (lax.broadcasted_iota(jnp.int32, (n_feat, n_feat), 0)
           == lax.broadcasted_iota(jnp.int32, (n_feat, n_feat), 1))
    for g, (window, dil) in enumerate(GROUPS):
        c_ref = (c0_ref, c1_ref, c2_ref)[g]
        n_ref = (n0_ref, n1_ref, n2_ref)[g]
        lanes = slice(g * GROUP_W, (g + 1) * GROUP_W)
        kn = kv[:, lanes]
        vn = kv[:, ATTN_W + g * GROUP_W:ATTN_W + (g + 1) * GROUP_W]
        q8 = jnp.where(head8, jnp.broadcast_to(q[:, lanes], (8, GROUP_W)), 0.0)
        cache = c_ref[0]
        pos = lax.broadcasted_iota(jnp.int32, (8, window), 1)
        s = _dot(q8.astype(BF16), cache[:GROUP_W].astype(BF16))
        s = jnp.where(jnp.bitwise_and(pos, dil - 1) == 0, s, NEG)
        s_new = jnp.sum(q8 * kn, axis=-1, keepdims=True)
        m = jnp.maximum(jnp.max(s, axis=-1, keepdims=True), s_new)
        p = jnp.exp(s - m)
        p_new = jnp.exp(s_new - m)
        den = jnp.sum(p, axis=-1, keepdims=True) + p_new
        o8 = (_dot_nt(p.astype(BF16), cache[GROUP_W:].astype(BF16)) + p_new * vn) / den
        o_ref[0, :, lanes] = jnp.sum(jnp.where(head8, o8, 0.0), axis=0, keepdims=True).astype(BF16)
        lse = jnp.where(diag8, m + jnp.log(den), 0.0)
        l_ref[0, :, g * LANES:(g + 1) * LANES] = jnp.sum(lse, axis=0, keepdims=True)
        new_row = jnp.concatenate([kn, vn], axis=1)
        new_col = jnp.sum(jnp.where(eye, new_row, 0.0), axis=-1, keepdims=True)
        rolled = pltpu.roll(cache, window - 1, axis=1)
        last = lax.broadcasted_iota(jnp.int32, (n_feat, window), 1) == window - 1
        n_ref[0] = jnp.where(last, new_col, rolled)


def _swa_step(q, kvt, caches):
    nb = q.shape[0]
    row3 = lambda width: pl.BlockSpec((1, 1, width), lambda b: (b, 0, 0))
    cache_specs = [pl.BlockSpec((1, 2 * GROUP_W, win), lambda b: (b, 0, 0)) for win, _ in GROUPS]
    outs = pl.pallas_call(
        _swa_step_kernel,
        grid=(nb,),
        in_specs=[row3(ATTN_W), row3(2 * ATTN_W)] + cache_specs,
        out_specs=tuple(cache_specs) + (row3(ATTN_W), row3(3 * LANES)),
        out_shape=tuple(jax.ShapeDtypeStruct(c.shape, F32) for c in caches)
                  + (jax.ShapeDtypeStruct((nb, 1, ATTN_W), BF16),
                     jax.ShapeDtypeStruct((nb, 1, 3 * LANES), F32)),
        compiler_params=_params(1),
        name="swa_step",
    )(q.reshape(nb, 1, ATTN_W), kvt.reshape(nb, 1, 2 * ATTN_W), *caches)
    return list(outs[:3]), outs[3].reshape(nb, ATTN_W), outs[4].reshape(nb, 3 * LANES)


def _position_order(o_ref, l_ref, dil, nat_ref, slab0):
    tm = o_ref.shape[0]
    n = tm // dil
    for r in range(dil):
        rows = slice(r * n, (r + 1) * n)
        dst = pl.ds(r, n, stride=dil)
        nat_ref[slab0, dst, :] = o_ref[rows, :LANES].astype(F32)
        nat_ref[slab0 + 1, dst, :] = o_ref[rows, LANES:].astype(F32)
        nat_ref[slab0 + 2, dst, :] = l_ref[rows, :]
    o = jnp.concatenate([nat_ref[slab0], nat_ref[slab0 + 1]], axis=1)
    return o, nat_ref[slab0 + 2]


def _mix_ffn_kernel(x1_ref, ys_ref, o0_ref, o1_ref, o2_ref, l0_ref, l1_ref, l2_ref,
                    gm_ref, wgt_ref, wsp_ref, wap_ref, wo_ref, seget_ref,
                    g2_ref, wg_ref, wu_ref, wd_ref, y_ref, *scratch):
    x1 = x1_ref[...]
    h = _rms(x1, gm_ref[...]).astype(BF16)
    gates = jax.nn.sigmoid(_dot(h, wgt_ref[...]))
    os_ = [o0_ref[...].astype(F32)]
    ls = [l0_ref[...]]
    for g, (o_ref, l_ref) in enumerate(((o1_ref, l1_ref), (o2_ref, l2_ref)), start=1):
        if scratch:
            o, l = _position_order(o_ref, l_ref, GROUPS[g][1], scratch[0], 3 * (g - 1))
        else:
            o, l = o_ref[...].astype(F32), l_ref[...]
        os_.append(o)
        ls.append(l)
    l_top = jnp.maximum(jnp.maximum(ls[0], ls[1]), ls[2])
    es = [jnp.exp(l - l_top) for l in ls]
    inv = 1.0 / (es[0] + es[1] + es[2])
    seget = seget_ref[...]
    y_attn = None
    for e, o in zip(es, os_):
        t = _split_dot(e * inv, seget) * o
        y_attn = t if y_attn is None else y_attn + t
    mixed = (gates[:, :D_MODEL] * _dot(ys_ref[...], wsp_ref[...])
             + gates[:, D_MODEL:] * _dot(y_attn.astype(BF16), wap_ref[...]))
    x2 = x1 + _dot(mixed.astype(BF16), wo_ref[...])
    y_ref[...] = _ffn(x2, g2_ref, wg_ref, wu_ref, wd_ref)


def _mix_ffn(x1, y_ssm, os_, ls_, w, tm, residue_major):
    rows = x1.shape[0]
    row_spec = lambda width: pl.BlockSpec((tm, width), lambda i: (i, 0))
    n_slabs = 3 * (len(GROUPS) - 1)
    scratch = [pltpu.VMEM((n_slabs, tm, LANES), F32)] if residue_major else []
    return pl.pallas_call(
        _mix_ffn_kernel,
        grid=(rows // tm,),
        in_specs=[row_spec(D_MODEL), row_spec(SSM_W)] + [row_spec(GROUP_W)] * 3 + [row_spec(LANES)] * 3
                 + [_const_spec((1, D_MODEL)), _const_spec((D_MODEL, 2 * D_MODEL)),
                    _const_spec((SSM_W, D_MODEL)), _const_spec((GROUP_W, D_MODEL)),
                    _const_spec((D_MODEL, D_MODEL)), _const_spec((LANES, GROUP_W)),
                    _const_spec((1, D_MODEL)), _const_spec((D_MODEL, D_FF)), _const_spec((D_MODEL, D_FF)),
                    _const_spec((D_FF, D_MODEL))],
        out_specs=row_spec(D_MODEL),
        out_shape=jax.ShapeDtypeStruct((rows, D_MODEL), F32),
        scratch_shapes=scratch,
        compiler_params=_params(1),
        name="mix_ffn",
    )(x1, y_ssm, *os_, *ls_, w["g_mix"], w["w_gates"], w["w_ssm_proj"], w["w_attn_proj"], w["w_o"],
      w["seg_et"], w["g_ffn2"], w["w2_gate"], w["w2_up"], w["w2_down"])


def _prepare_weights(g_ffn1, w1_gate, w1_up, w1_down, g_mix, w_in, g_q, g_k,
                     ssm_a_re, ssm_a_im, ssm_log_dt, ssm_b_re, ssm_b_im, ssm_c_re, ssm_c_im,
                     ssm_d, w_glu, b_glu, w_ssm_proj, w_attn_proj, w_o, g_ffn2, w2_gate, w2_up, w2_down):
    w = {}
    row = lambda a: a.reshape(1, -1).astype(F32)
    w["g_ffn1"], w["g_mix"], w["g_ffn2"] = row(g_ffn1), row(g_mix), row(g_ffn2)
    for name, a in (("w1_gate", w1_gate), ("w1_up", w1_up), ("w1_down", w1_down),
                    ("w2_gate", w2_gate), ("w2_up", w2_up), ("w2_down", w2_down),
                    ("w_glu", w_glu), ("w_ssm_proj", w_ssm_proj), ("w_attn_proj", w_attn_proj),
                    ("w_o", w_o)):
        w[name] = a.astype(BF16)
    w["w_in_a"] = w_in[:, :IN_A].astype(BF16)
    w["w_gates"] = w_in[:, IN_A:].astype(BF16)
    per_head = lambda g: jnp.broadcast_to(g[:, None, :], (len(GROUPS), HEADS, HEAD_DIM)).reshape(1, ATTN_W)
    w["g_q"], w["g_k"] = per_head(g_q.astype(F32)), per_head(g_k.astype(F32))
    head_of_lane = jnp.arange(GROUP_W) // HEAD_DIM
    w["seg_mean"] = ((head_of_lane[:, None] == head_of_lane[None, :]) / HEAD_DIM).astype(BF16)
    w["seg_et"] = (jnp.arange(LANES)[:, None] == head_of_lane[None, :]).astype(BF16)
    w["ssm_d"], w["b_glu"] = row(ssm_d), row(b_glu)

    pw, xb, k8, gcp = _ssm_prep(ssm_a_re, ssm_a_im, ssm_log_dt, ssm_b_re, ssm_b_im, ssm_c_re, ssm_c_im)
    per_group = lambda a: a.reshape(SSM_GROUPS, SSM_CH, SSM_P)[:, 0].reshape(1, N_STATE)
    w["ab_re"], w["ab_im"] = per_group(pw[0, 0]), per_group(pw[0, 1])
    w["a8_re"], w["a8_im"] = per_group(pw[1, 0]), per_group(pw[1, 1])
    gpq = SSM_GROUPS // NQ
    eye = jnp.eye(gpq, dtype=F32)
    k8q = k8.reshape(BLK, NQ, LANES, NQ, LANES)
    w["k8"] = jnp.stack([jnp.transpose(k8q[:, q, :, q, :], (1, 0, 2)).reshape(LANES, BLK * LANES)
                         for q in range(NQ)]).astype(BF16)
    xb6 = xb.reshape(BLK, 2, NQ, gpq, SSM_CH, SSM_P)[::-1]
    w["w7"] = jnp.stack([jnp.einsum("rkgcp,gh->rgckhp", xb6[:, :, q], eye).reshape(BLK * LANES, 2 * QS)
                         for q in range(NQ)]).astype(BF16)
    sign = jnp.array([1.0, -1.0], F32).reshape(1, 2, 1, 1, 1, 1)
    gc6 = gcp.reshape(BLK, 2, NQ, gpq, SSM_CH, SSM_P) * sign
    w["gq"] = jnp.stack([jnp.einsum("rkgcp,gh->kgprhc", gc6[:, :, q], eye).reshape(2 * QS, BLK * LANES)
                         for q in range(NQ)]).astype(BF16)
    bb = xb[0].reshape(2, SSM_GROUPS, SSM_CH, SSM_P)
    eye16 = jnp.eye(16, dtype=F32)
    b_half = lambda part, n: jnp.einsum("gcp,gh->gchp", bb[part, 16 * n:16 * (n + 1)], eye16).reshape(256, 1024)
    c_half = lambda cc, n: jnp.einsum("gcp,gh->gphc", cc[16 * n:16 * (n + 1)].astype(F32), eye16).reshape(1024, 256)
    w["b_bd"] = jnp.stack([jnp.concatenate([b_half(0, n), b_half(1, n)], axis=1)
                           for n in range(2)]).astype(BF16)
    w["c_bd"] = jnp.stack([jnp.concatenate([c_half(ssm_c_re, n), -c_half(ssm_c_im, n)], axis=0)
                           for n in range(2)]).astype(BF16)
    return w


def _layer(x, state, caches, w):
    nb, seq, _ = x.shape
    rows = nb * seq
    xf = x.reshape(rows, D_MODEL)
    if caches is None:
        tm = TILE
        assert seq % tm == 0 and state is None
        tail = min(GROUPS[-1][0], seq)
        x1, u, q, k, v, kvt = _ffn_in(xf, w, tm, seq // tm, tail // tm, True)
        y_ssm, h_re, h_im = _s5(u.reshape(nb, seq, SSM_W), w, min(S5_T, seq))
        y_ssm = y_ssm.reshape(rows, SSM_W)
        q3, k3, v3 = (a.reshape(nb, seq, ATTN_W) for a in (q, k, v))
        os_, ls_ = zip(*[_swa(q3, k3, v3, g, tm) for g in range(len(GROUPS))])
        new_caches = []
        for g, (window, _) in enumerate(GROUPS):
            keep = min(window, seq)
            kg = kvt[:, g * GROUP_W:(g + 1) * GROUP_W, tail - keep:]
            vg = kvt[:, ATTN_W + g * GROUP_W:ATTN_W + (g + 1) * GROUP_W, tail - keep:]
            new_caches.append(jnp.concatenate([kg, vg], axis=1))
        y = _mix_ffn(x1, y_ssm, os_, ls_, w, tm, True)
    else:
        assert seq == 1
        x1, u, q, k, v, kvt = _ffn_in(xf, w, rows, 1, 1, False)
        y_ssm, h_re, h_im = _s5_step(u, state[0], state[1], w)
        new_caches, o_all, l_all = _swa_step(q, kvt, caches)
        os_ = [o_all[:, g * GROUP_W:(g + 1) * GROUP_W] for g in range(len(GROUPS))]
        ls_ = [l_all[:, g * LANES:(g + 1) * LANES] for g in range(len(GROUPS))]
        y = _mix_ffn(x1, y_ssm, os_, ls_, w, rows, False)
    return y.reshape(nb, seq, D_MODEL), (h_re, h_im), new_caches


def _to_features_major(c):
    nb, win = c.shape[:2]
    return jnp.transpose(c, (0, 2, 3, 4, 1)).reshape(nb, 2 * GROUP_W, win)


def _to_window_buffer(c):
    nb, _, win = c.shape
    return jnp.transpose(c.reshape(nb, 2, HEADS, HEAD_DIM, win), (0, 4, 1, 2, 3))[None]


def kernel(x_prompt, x_sample, cache_kv_w128, cache_kv_w512, cache_kv_w2048, state_ssm_re, state_ssm_im, g_ffn1, w1_gate, w1_up, w1_down, g_mix, w_in, g_q, g_k, ssm_a_re, ssm_a_im, ssm_log_dt, ssm_b_re, ssm_b_im, ssm_c_re, ssm_c_im, ssm_d, w_glu, b_glu, w_ssm_proj, w_attn_proj, w_o, g_ffn2, w2_gate, w2_up, w2_down):
    layer_weights = (g_ffn1, w1_gate, w1_up, w1_down, g_mix, w_in, g_q, g_k,
                     ssm_a_re, ssm_a_im, ssm_log_dt, ssm_b_re, ssm_b_im, ssm_c_re, ssm_c_im,
                     ssm_d, w_glu, b_glu, w_ssm_proj, w_attn_proj, w_o, g_ffn2, w2_gate, w2_up, w2_down)
    depth = g_ffn1.shape[0]
    assert depth == 1, "window caches of deeper layers would need the previous layer's outputs"
    w = _prepare_weights(*(a[0] for a in layer_weights))
    nb_s = x_sample.shape[0]
    sdt = state_ssm_re.dtype
    as_state = lambda h: h.reshape(1, -1, SSM_GROUPS, SSM_P).astype(sdt)

    y_p, h_p, kv_p = _layer(x_prompt, None, None, w)

    caches = [_to_features_major(c[0]) for c in (cache_kv_w128, cache_kv_w512, cache_kv_w2048)]
    for c, (window, _) in zip(caches, GROUPS):
        assert c.shape[2] == window, "the sample group's caches must hold a full window"
    state_s = (state_ssm_re[0].reshape(nb_s, N_STATE).astype(F32),
               state_ssm_im[0].reshape(nb_s, N_STATE).astype(F32))
    y_s, h_s, kv_s = _layer(x_sample, state_s, caches, w)

    return ((y_p, y_s) + tuple(_to_window_buffer(c) for c in kv_p) + (as_state(h_p[0]), as_state(h_p[1]))
            + tuple(_to_window_buffer(c) for c in kv_s) + (as_state(h_s[0]), as_state(h_s[1])))
```

```python
import functools

import jax
import jax.numpy as jnp
from jax import lax
from jax.experimental import pallas as pl
from jax.experimental.pallas import tpu as pltpu

F32 = jnp.float32
BF16 = jnp.bfloat16

D_MODEL = 1024
D_FF = 2816
HEAD_DIM = 64
HEADS = 4
GROUPS = ((128, 1), (512, 4), (2048, 16))
KEYS_PER_QUERY = 128
GROUP_W = HEADS * HEAD_DIM
ATTN_W = len(GROUPS) * GROUP_W
SSM_W = 512
SSM_GROUPS = 32
SSM_CH = 16
SSM_P = 64
N_STATE = SSM_GROUPS * SSM_P
IN_A = SSM_W + 3 * ATTN_W
RMS_EPS = 1e-6
NEG = -1e30
LANES = 128

TILE = 512
FF_CHUNKS = ((0, 1024), (1024, 1024), (2048, 768))
BLK = 8
NQ = SSM_W // LANES
QS = N_STATE // NQ
S5_T = 1024
VMEM_LIMIT = 56 * 1024 * 1024


def _const_spec(shape):
    nd = len(shape)
    return pl.BlockSpec(shape, lambda *_: (0,) * nd, pipeline_mode=pl.Buffered(1))


def _params(n_grid):
    return pltpu.CompilerParams(dimension_semantics=("arbitrary",) * n_grid,
                                vmem_limit_bytes=VMEM_LIMIT)


def _rms(x, g):
    ms = jnp.mean(x * x, axis=-1, keepdims=True)
    return x * lax.rsqrt(ms + RMS_EPS) * g


def _dot(a, b):
    return jnp.dot(a, b, preferred_element_type=F32)


def _dot_nt(a, b):
    return lax.dot_general(a, b, (((1,), (1,)), ((), ())), preferred_element_type=F32)


def _split_dot(x, m):
    hi = x.astype(BF16)
    lo = (x - hi.astype(F32)).astype(BF16)
    return _dot(hi, m) + _dot(lo, m)


def _ffn(x, g_ref, wg_ref, wu_ref, wd_ref):
    xn = _rms(x, g_ref[...]).astype(BF16)
    acc = None
    for f0, fw in FF_CHUNKS:
        hg = _dot(xn, wg_ref[:, f0:f0 + fw])
        hu = _dot(xn, wu_ref[:, f0:f0 + fw])
        a = (hg * jax.nn.sigmoid(hg) * hu).astype(BF16)
        d = _dot(a, wd_ref[f0:f0 + fw, :])
        acc = d if acc is None else acc + d
    return x + 0.5 * acc


def _split3_dot_nt(a, b):
    a_hi = a.astype(BF16)
    a_lo = (a - a_hi.astype(F32)).astype(BF16)
    b_hi = b.astype(BF16)
    b_lo = (b - b_hi.astype(F32)).astype(BF16)
    return _dot_nt(a_hi, b_hi) + _dot_nt(a_hi, b_lo) + _dot_nt(a_lo, b_hi)


def _prep_kernel(are_ref, aim_ref, ldt_ref, bre_ref, bim_ref, cre_ref, cim_ref,
                 pw_ref, k8_ref, w7_ref, gq_ref):
    a_re = are_ref[...]
    a_im = aim_ref[...]
    dt = jnp.exp(ldt_ref[...])
    mag = jnp.exp(a_re * dt)
    ab_re = mag * jnp.cos(a_im * dt)
    ab_im = mag * jnp.sin(a_im * dt)
    inv = 1.0 / (a_re * a_re + a_im * a_im)
    f_re = ((ab_re - 1.0) * a_re + ab_im * a_im) * inv
    f_im = (ab_im * a_re - (ab_re - 1.0) * a_im) * inv
    b_re = bre_ref[...]
    b_im = bim_ref[...]
    bb_re = f_re * b_re - f_im * b_im
    bb_im = f_re * b_im + f_im * b_re
    c_re = cre_ref[...]
    c_im = cim_ref[...]
    gc = SSM_GROUPS * SSM_CH
    same_group = (lax.broadcasted_iota(jnp.int32, (gc, gc), 0) // SSM_CH
                  == lax.broadcasted_iota(jnp.int32, (gc, gc), 1) // SSM_CH)
    spread = (lax.broadcasted_iota(jnp.int32, (SSM_P, QS), 1) % SSM_P
              == lax.broadcasted_iota(jnp.int32, (SSM_P, QS), 0)).astype(BF16)
    spread_t = (lax.broadcasted_iota(jnp.int32, (QS, SSM_P), 0) % SSM_P
                == lax.broadcasted_iota(jnp.int32, (QS, SSM_P), 1)).astype(BF16)
    mask_w = (lax.broadcasted_iota(jnp.int32, (LANES, QS), 0) // SSM_CH
              == lax.broadcasted_iota(jnp.int32, (LANES, QS), 1) // SSM_P)
    mask_g = (lax.broadcasted_iota(jnp.int32, (QS, LANES), 0) // SSM_P
              == lax.broadcasted_iota(jnp.int32, (QS, LANES), 1) // SSM_CH)
    p_re = jnp.ones_like(ab_re)
    p_im = jnp.zeros_like(ab_re)
    for i in range(BLK + 1):
        if i == 1:
            pw_ref[0, 0] = p_re
            pw_ref[0, 1] = p_im
        if i == BLK:
            pw_ref[1, 0] = p_re
            pw_ref[1, 1] = p_im
        if i >= 1:
            r = i - 1
            gcs = (c_re * p_re - c_im * p_im, -(c_re * p_im + c_im * p_re))
            for q in range(NQ):
                for part in range(2):
                    slab = gcs[part][q * LANES:(q + 1) * LANES].astype(BF16)
                    blk = jnp.where(mask_g, _dot_nt(spread_t, slab), 0.0)
                    gq_ref[q, part * QS:(part + 1) * QS, r * LANES:(r + 1) * LANES] = blk.astype(BF16)
        if i < BLK:
            r = BLK - 1 - i
            xs = (p_re * bb_re - p_im * bb_im, p_re * bb_im + p_im * bb_re)
            kk = jnp.where(same_group, _split3_dot_nt(xs[0], c_re) - _split3_dot_nt(xs[1], c_im), 0.0)
            for q in range(NQ):
                rows = slice(q * LANES, (q + 1) * LANES)
                k8_ref[q, :, i * LANES:(i + 1) * LANES] = kk[rows, rows].astype(BF16)
                for part in range(2):
                    blk = jnp.where(mask_w, _dot(xs[part][rows].astype(BF16), spread), 0.0)
                    w7_ref[q, r * LANES:(r + 1) * LANES, part * QS:(part + 1) * QS] = blk.astype(BF16)
        p_re, p_im = p_re * ab_re - p_im * ab_im, p_re * ab_im + p_im * ab_re


def _ssm_prep(a_re, a_im, log_dt, b_re, b_im, c_re, c_im):
    gc = SSM_GROUPS * SSM_CH
    rep = lambda a: jnp.repeat(a.astype(F32), SSM_CH, axis=0)
    b_rows = lambda b: jnp.transpose(b.astype(F32), (0, 2, 1)).reshape(gc, SSM_P)
    c_rows = lambda c: c.astype(F32).reshape(gc, SSM_P)
    return pl.pallas_call(
        _prep_kernel,
        out_shape=(jax.ShapeDtypeStruct((2, 2, gc, SSM_P), F32),
                   jax.ShapeDtypeStruct((NQ, LANES, BLK * LANES), BF16),
                   jax.ShapeDtypeStruct((NQ, BLK * LANES, 2 * QS), BF16),
                   jax.ShapeDtypeStruct((NQ, 2 * QS, BLK * LANES), BF16)),
        compiler_params=pltpu.CompilerParams(vmem_limit_bytes=VMEM_LIMIT),
        name="ssm_prep",
    )(rep(a_re), rep(a_im), rep(log_dt.reshape(SSM_GROUPS, 1)), b_rows(b_re), b_rows(b_im),
      c_rows(c_re), c_rows(c_im))


def _head_norm(t, g, seg):
    sq = t * t
    ms = jnp.concatenate(
        [_split_dot(sq[:, s * GROUP_W:(s + 1) * GROUP_W], seg) for s in range(len(GROUPS))], axis=1)
    return t * lax.rsqrt(ms + RMS_EPS) * g


def _store_attn_rows(val, out_ref, perm_ref, slab0):
    tm = val.shape[0]
    out_ref[:, :GROUP_W] = val[:, :GROUP_W].astype(BF16)
    for g in range(1, len(GROUPS)):
        dil = GROUPS[g][1]
        n = tm // dil
        for half in range(GROUP_W // LANES):
            c0 = g * GROUP_W + half * LANES
            col = val[:, c0:c0 + LANES]
            if perm_ref is None:
                out_ref[:, c0:c0 + LANES] = col.astype(BF16)
                continue
            slab = slab0 + (g - 1) * (GROUP_W // LANES) + half
            perm_ref[slab] = col
            for r in range(dil):
                out_ref[r * n:(r + 1) * n, c0:c0 + LANES] = (
                    perm_ref[slab, pl.ds(r, n, stride=dil), :].astype(BF16))


def _ffn_in_kernel(x_ref, g1_ref, wg_ref, wu_ref, wd_ref, gm_ref, win_ref, gq_ref, gk_ref, seg_ref,
                   x1_ref, u_ref, q_ref, k_ref, v_ref, *rest, tiles_per_seq, keeps):
    perm_ref = rest[-1] if len(rest) > 1 else None
    x1 = _ffn(x_ref[...], g1_ref, wg_ref, wu_ref, wd_ref)
    x1_ref[...] = x1
    h = _rms(x1, gm_ref[...]).astype(BF16)
    proj = _dot(h, win_ref[...])
    u_ref[...] = proj[:, :SSM_W]
    seg = seg_ref[...]
    q = _head_norm(proj[:, SSM_W:SSM_W + ATTN_W], gq_ref[...], seg)
    k = _head_norm(proj[:, SSM_W + ATTN_W:SSM_W + 2 * ATTN_W], gk_ref[...], seg)
    v = proj[:, SSM_W + 2 * ATTN_W:]
    slabs = (len(GROUPS) - 1) * (GROUP_W // LANES)
    _store_attn_rows(q * HEAD_DIM ** -0.5, q_ref, perm_ref, 0)
    _store_attn_rows(k, k_ref, perm_ref, slabs)
    _store_attn_rows(v, v_ref, perm_ref, 2 * slabs)
    if perm_ref is None:
        rest[0][:, :ATTN_W] = k
        rest[0][:, ATTN_W:] = v
        return
    tm = x1.shape[0]
    j = pl.program_id(0) % tiles_per_seq
    for g, keep in enumerate(keeps):
        lanes = slice(g * GROUP_W, (g + 1) * GROUP_W)
        cols = min(keep, tm)

        @pl.when(j >= tiles_per_seq - pl.cdiv(keep, tm))
        def _(g=g, lanes=lanes, cols=cols):
            rest[g][0, :GROUP_W, :] = k[:, lanes].T[:, tm - cols:]
            rest[g][0, GROUP_W:, :] = v[:, lanes].T[:, tm - cols:]


def _ffn_in(x, w, tm, tiles_per_seq, prompt_form):
    rows = x.shape[0]
    n_tiles = rows // tm
    n_seq = n_tiles // tiles_per_seq
    row_spec = lambda width: pl.BlockSpec((tm, width), lambda i: (i, 0))
    if prompt_form:
        keeps = tuple(min(window, tiles_per_seq * tm) for window, _ in GROUPS)
        assert all(keep % tm == 0 or keep < tm for keep in keeps)

        def tail_spec(keep):
            first = tiles_per_seq - pl.cdiv(keep, tm)
            return pl.BlockSpec((1, 2 * GROUP_W, min(keep, tm)),
                                lambda i: (i // tiles_per_seq, 0, jnp.maximum(i % tiles_per_seq - first, 0)))

        tail_shapes = tuple(jax.ShapeDtypeStruct((n_seq, 2 * GROUP_W, keep), F32) for keep in keeps)
        tail_specs = tuple(tail_spec(keep) for keep in keeps)
        scratch = [pltpu.VMEM((3 * (len(GROUPS) - 1) * (GROUP_W // LANES), tm, LANES), F32)]
    else:
        assert tiles_per_seq == 1
        keeps = ()
        tail_shapes = (jax.ShapeDtypeStruct((rows, 2 * ATTN_W), F32),)
        tail_specs = (row_spec(2 * ATTN_W),)
        scratch = []
    out_shape = (jax.ShapeDtypeStruct((rows, D_MODEL), F32),
                 jax.ShapeDtypeStruct((rows, SSM_W), F32),
                 jax.ShapeDtypeStruct((rows, ATTN_W), BF16),
                 jax.ShapeDtypeStruct((rows, ATTN_W), BF16),
                 jax.ShapeDtypeStruct((rows, ATTN_W), BF16)) + tail_shapes
    outs = pl.pallas_call(
        functools.partial(_ffn_in_kernel, tiles_per_seq=tiles_per_seq, keeps=keeps),
        grid=(n_tiles,),
        in_specs=[row_spec(D_MODEL), _const_spec((1, D_MODEL)),
                  _const_spec((D_MODEL, D_FF)), _const_spec((D_MODEL, D_FF)), _const_spec((D_FF, D_MODEL)),
                  _const_spec((1, D_MODEL)), _const_spec((D_MODEL, IN_A)),
                  _const_spec((1, ATTN_W)), _const_spec((1, ATTN_W)), _const_spec((GROUP_W, GROUP_W))],
        out_specs=(row_spec(D_MODEL), row_spec(SSM_W), row_spec(ATTN_W), row_spec(ATTN_W), row_spec(ATTN_W))
                  + tail_specs,
        out_shape=out_shape,
        scratch_shapes=scratch,
        compiler_params=_params(1),
        name="ffn_in",
    )(x, w["g_ffn1"], w["w1_gate"], w["w1_up"], w["w1_down"], w["g_mix"], w["w_in_a"],
      w["g_q"], w["g_k"], w["seg_mean"])
    return outs[:5], (list(outs[5:]) if prompt_form else outs[5])


def _glu_out(y, u, dsk_ref, wglu_ref, bglu_ref):
    y = jax.nn.gelu(y + dsk_ref[...] * u)
    z = _dot(y.astype(BF16), wglu_ref[...]) + bglu_ref[...]
    return (y * jax.nn.sigmoid(z)).astype(BF16)


def _s5_kernel(u0_ref, u1_ref, u2_ref, u3_ref, a8re_ref, a8im_ref, k8_ref, w7_ref, gq_ref,
               dsk_ref, wglu_ref, bglu_ref, y_ref, hre_ref, him_ref,
               ls_ref, yq_ref, y2s_ref, sre_ref, sim_ref):
    u_refs = (u0_ref, u1_ref, u2_ref, u3_ref)
    nb, t_blk, _ = u0_ref.shape
    nblk = t_blk // BLK
    qw = 2 * QS

    @pl.when(pl.program_id(0) == 0)
    def _():
        sre_ref[...] = jnp.zeros_like(sre_ref)
        sim_ref[...] = jnp.zeros_like(sim_ref)

    sub = lax.broadcasted_iota(jnp.int32, (nblk, BLK, LANES), 1)
    for q in range(NQ):
        ublks = []
        for b in range(nb):
            uq = u_refs[q][b]
            z3 = _dot(uq.astype(BF16), k8_ref[q]).reshape(nblk, BLK, BLK * LANES)
            acc = z3[:, :, :LANES]
            for i in range(1, BLK):
                zi = z3[:, :, LANES * i:LANES * (i + 1)]
                acc = acc + jnp.where(sub >= i, pltpu.roll(zi, i, axis=1), 0.0)
            yq_ref[b * NQ + q] = acc.reshape(t_blk, LANES)
            ublks.append(jnp.concatenate(
                [u_refs[q][b, pl.ds(r, nblk, stride=BLK), :] for r in range(BLK)], axis=1))
        ls = _dot(jnp.concatenate(ublks, axis=0).astype(BF16), w7_ref[q])
        for b in range(nb):
            ls_ref[b, :, q * qw:(q + 1) * qw] = ls[b * nblk:(b + 1) * nblk]

    for q in range(NQ):
        ch = slice(q * QS, (q + 1) * QS)
        c_re = q * qw
        c_im = c_re + QS
        a_re = a8re_ref[:, ch]
        a_im = a8im_ref[:, ch]

        def body(k, carry, c_re=c_re, c_im=c_im, a_re=a_re, a_im=a_im):
            row = pl.ds(k, 1)
            out = []
            for b in range(nb):
                s_re, s_im = carry[2 * b], carry[2 * b + 1]
                l_re = ls_ref[b, row, c_re:c_re + QS]
                l_im = ls_ref[b, row, c_im:c_im + QS]
                ls_ref[b, row, c_re:c_re + QS] = s_re
                ls_ref[b, row, c_im:c_im + QS] = s_im
                out.append(a_re * s_re - a_im * s_im + l_re)
                out.append(a_re * s_im + a_im * s_re + l_im)
            return tuple(out)

        init = []
        for b in range(nb):
            init += [sre_ref[b:b + 1, ch], sim_ref[b:b + 1, ch]]
        fin = lax.fori_loop(0, nblk, body, tuple(init), unroll=8)
        for b in range(nb):
            sre_ref[b:b + 1, ch] = fin[2 * b]
            sim_ref[b:b + 1, ch] = fin[2 * b + 1]

    for q in range(NQ):
        sc = jnp.concatenate([ls_ref[b, :, q * qw:(q + 1) * qw] for b in range(nb)], axis=0)
        y2 = _dot(sc.astype(BF16), gq_ref[q])
        for b in range(nb):
            for r in range(BLK):
                y2s_ref[pl.ds(r, nblk, stride=BLK), :] = y2[b * nblk:(b + 1) * nblk, LANES * r:LANES * (r + 1)]
            yq_ref[b * NQ + q] = yq_ref[b * NQ + q] + y2s_ref[...]

    for b in range(nb):
        y = jnp.concatenate([yq_ref[b * NQ + q] for q in range(NQ)], axis=1)
        u = jnp.concatenate([u_refs[q][b] for q in range(NQ)], axis=1)
        y_ref[b] = _glu_out(y, u, dsk_ref, wglu_ref, bglu_ref)
    hre_ref[...] = sre_ref[...]
    him_ref[...] = sim_ref[...]


def _s5(u, w, t_blk):
    nb, seq, _ = u.shape
    nblk = t_blk // BLK
    u_specs = [pl.BlockSpec((nb, t_blk, LANES), lambda c, q=q: (0, c, q)) for q in range(NQ)]
    state = pl.BlockSpec((nb, N_STATE), lambda c: (0, 0))
    return pl.pallas_call(
        _s5_kernel,
        grid=(seq // t_blk,),
        in_specs=u_specs + [_const_spec((1, N_STATE)), _const_spec((1, N_STATE)),
                            _const_spec((NQ, LANES, BLK * LANES)),
                            _const_spec((NQ, BLK * LANES, 2 * QS)), _const_spec((NQ, 2 * QS, BLK * LANES)),
                            _const_spec((1, SSM_W)), _const_spec((SSM_W, SSM_W)), _const_spec((1, SSM_W))],
        out_specs=(pl.BlockSpec((nb, t_blk, SSM_W), lambda c: (0, c, 0)), state, state),
        out_shape=(jax.ShapeDtypeStruct((nb, seq, SSM_W), BF16),
                   jax.ShapeDtypeStruct((nb, N_STATE), F32), jax.ShapeDtypeStruct((nb, N_STATE), F32)),
        scratch_shapes=[pltpu.VMEM((nb, nblk, NQ * 2 * QS), F32),
                        pltpu.VMEM((nb * NQ, t_blk, LANES), F32),
                        pltpu.VMEM((t_blk, LANES), F32),
                        pltpu.VMEM((nb, N_STATE), F32), pltpu.VMEM((nb, N_STATE), F32)],
        compiler_params=_params(1),
        name="s5",
    )(u, u, u, u, w["a8_re"], w["a8_im"], w["k8"], w["w7"], w["gq"], w["ssm_d"], w["w_glu"], w["b_glu"])


def _s5_step_kernel(u_ref, hre_ref, him_ref, are_ref, aim_ref, k8_ref, w7_ref, gq_ref,
                    dsk_ref, wglu_ref, bglu_ref, y_ref, nre_ref, nim_ref):
    u = u_ref[...]
    ys = []
    for q in range(NQ):
        ch = slice(q * QS, (q + 1) * QS)
        uq = u[:, q * LANES:(q + 1) * LANES].astype(BF16)
        x = _dot(uq, w7_ref[q, (BLK - 1) * LANES:, :])
        a_re, a_im = are_ref[:, ch], aim_ref[:, ch]
        s_re, s_im = hre_ref[:, ch], him_ref[:, ch]
        nre_ref[:, ch] = a_re * s_re - a_im * s_im + x[:, :QS]
        nim_ref[:, ch] = a_re * s_im + a_im * s_re + x[:, QS:]
        s = jnp.concatenate([s_re, s_im], axis=1).astype(BF16)
        ys.append(_dot(s, gq_ref[q, :, :LANES]) + _dot(uq, k8_ref[q, :, :LANES]))
    y_ref[...] = _glu_out(jnp.concatenate(ys, axis=1), u, dsk_ref, wglu_ref, bglu_ref)


def _s5_step(u, h_re, h_im, w):
    rows = u.shape[0]
    return pl.pallas_call(
        _s5_step_kernel,
        out_shape=(jax.ShapeDtypeStruct((rows, SSM_W), BF16),
                   jax.ShapeDtypeStruct((rows, N_STATE), F32), jax.ShapeDtypeStruct((rows, N_STATE), F32)),
        compiler_params=pltpu.CompilerParams(vmem_limit_bytes=VMEM_LIMIT),
        name="s5_step",
    )(u, h_re, h_im, w["ab_re"], w["ab_im"], w["k8"], w["w7"], w["gq"], w["ssm_d"], w["w_glu"], w["b_glu"])


def _swa_kernel(q_ref, k_ref, v_ref, kp_ref, vp_ref, o_ref, l_ref, *, qb, chunk):
    first = pl.program_id(2) == 0

    def rows(ref):
        return jnp.concatenate([ref[t] for t in range(ref.shape[0])], axis=0)

    q = rows(q_ref)
    kk = jnp.concatenate([rows(kp_ref), rows(k_ref)], axis=0)
    vv = jnp.concatenate([rows(vp_ref), rows(v_ref)], axis=0)
    nk = 2 * KEYS_PER_QUERY
    row = lax.broadcasted_iota(jnp.int32, (128, nk), 0)
    col = lax.broadcasted_iota(jnp.int32, (128, nk), 1)
    band = jnp.where(col >= row, jnp.where(col <= row + KEYS_PER_QUERY, 0.0, NEG), NEG)
    band_first = band + jnp.where(col < KEYS_PER_QUERY, jnp.where(first, NEG, 0.0), 0.0)
    lane = lax.broadcasted_iota(jnp.int32, (1, GROUP_W), 1) // HEAD_DIM
    lane_l = lax.broadcasted_iota(jnp.int32, (1, LANES), 1)
    for j in range(qb // 128):
        qj = q[128 * j:128 * (j + 1)]
        kj = kk[128 * j:128 * j + nk]
        vj = vv[128 * j:128 * j + nk]
        bias = band_first if j == 0 else band
        qs = jnp.concatenate([jnp.where(lane == h, qj, jnp.zeros_like(qj)) for h in range(HEADS)], axis=0)
        s = (_dot_nt(qs, kj).reshape(HEADS, 128, nk) + bias[None]).reshape(HEADS * 128, nk)
        m = jnp.max(s, axis=-1, keepdims=True)
        p = jnp.exp(s - m)
        den = jnp.sum(p, axis=-1, keepdims=True)
        on = _dot(p.astype(BF16), vj) * (1.0 / den)
        lse = m + jnp.log(den)
        o_acc = jnp.zeros((128, GROUP_W), F32)
        l_acc = jnp.zeros((128, LANES), F32)
        for h in range(HEADS):
            rows_h = slice(128 * h, 128 * (h + 1))
            o_acc = jnp.where(lane == h, on[rows_h], o_acc)
            l_acc = jnp.where(lane_l == h, lse[rows_h], l_acc)
        o_out = o_acc.astype(BF16)
        if chunk >= 128:
            t, r0 = divmod(128 * j, chunk)
            o_ref[t, r0:r0 + 128, :] = o_out
            l_ref[t, r0:r0 + 128, :] = l_acc
        else:
            for s_ in range(128 // chunk):
                t = (128 * j) // chunk + s_
                o_ref[t] = o_out[s_ * chunk:(s_ + 1) * chunk]
                l_ref[t] = l_acc[s_ * chunk:(s_ + 1) * chunk]


def _swa(q, k, v, g, tile):
    _, dil = GROUPS[g]
    nb, seq, _ = q.shape
    n_tiles = seq // tile
    chunk = tile // dil
    n = seq // dil
    qb = min(512, n)
    tpb = qb // chunk
    view = lambda a: a.reshape(nb, n_tiles, dil, chunk, a.shape[-1])
    cur = pl.BlockSpec((None, tpb, None, chunk, GROUP_W), lambda b, r, i: (b, i, r, 0, g))
    if chunk >= 128:
        prev = pl.BlockSpec((None, 1, None, 128, GROUP_W),
                            lambda b, r, i: (b, jnp.maximum(i * tpb - 1, 0), r, chunk // 128 - 1, g))
    else:
        ptiles = 128 // chunk
        prev = pl.BlockSpec((None, ptiles, None, chunk, GROUP_W),
                            lambda b, r, i: (b, jnp.maximum(i * (tpb // ptiles) - 1, 0), r, 0, g))
    out_block = lambda width: pl.BlockSpec((None, tpb, None, chunk, width), lambda b, r, i: (b, i, r, 0, 0))
    o, lse = pl.pallas_call(
        functools.partial(_swa_kernel, qb=qb, chunk=chunk),
        grid=(nb, dil, n // qb),
        in_specs=[cur, cur, cur, prev, prev],
        out_specs=(out_block(GROUP_W), out_block(LANES)),
        out_shape=(jax.ShapeDtypeStruct((nb, n_tiles, dil, chunk, GROUP_W), BF16),
                   jax.ShapeDtypeStruct((nb, n_tiles, dil, chunk, LANES), F32)),
        compiler_params=_params(3),
        name="swa_g%d" % g,
    )(view(q), view(k), view(v), view(k), view(v))
    return o.reshape(nb * seq, GROUP_W), lse.reshape(nb * seq, LANES)


def _swa_step_kernel(q_ref, kv_ref, c0_ref, c1_ref, c2_ref, n0_ref, n1_ref, n2_ref, o_ref, l_ref):
    q = q_ref[0].astype(F32)
    kv = kv_ref[0]
    row8 = lax.broadcasted_iota(jnp.int32, (8, GROUP_W), 0)
    head8 = lax.broadcasted_iota(jnp.int32, (8, GROUP_W), 1) // HEAD_DIM == row8
    diag8 = (lax.broadcasted_iota(jnp.int32, (8, LANES), 0)
             == lax.broadcasted_iota(jnp.int32, (8, LANES), 1))
    n_feat = 2 * GROUP_W
    eye = (lax.broadcasted_iota(jnp.int32, (n_feat, n_feat), 0)
           == lax.broadcasted_iota(jnp.int32, (n_feat, n_feat), 1))
    for g, (window, dil) in enumerate(GROUPS):
        c_ref = (c0_ref, c1_ref, c2_ref)[g]
        n_ref = (n0_ref, n1_ref, n2_ref)[g]
        lanes = slice(g * GROUP_W, (g + 1) * GROUP_W)
        kn = kv[:, lanes]
        vn = kv[:, ATTN_W + g * GROUP_W:ATTN_W + (g + 1) * GROUP_W]
        q8 = jnp.where(head8, jnp.broadcast_to(q[:, lanes], (8, GROUP_W)), 0.0)
        cache = c_ref[0]
        pos = lax.broadcasted_iota(jnp.int32, (8, window), 1)
        s = _dot(q8.astype(BF16), cache[:GROUP_W].astype(BF16))
        s = jnp.where(jnp.bitwise_and(pos, dil - 1) == 0, s, NEG)
        s_new = jnp.sum(q8 * kn, axis=-1, keepdims=True)
        m = jnp.maximum(jnp.max(s, axis=-1, keepdims=True), s_new)
        p = jnp.exp(s - m)
        p_new = jnp.exp(s_new - m)
        den = jnp.sum(p, axis=-1, keepdims=True) + p_new
        o8 = (_dot_nt(p.astype(BF16), cache[GROUP_W:].astype(BF16)) + p_new * vn) / den
        o_ref[0, :, lanes] = jnp.sum(jnp.where(head8, o8, 0.0), axis=0, keepdims=True).astype(BF16)
        lse = jnp.where(diag8, m + jnp.log(den), 0.0)
        l_ref[0, :, g * LANES:(g + 1) * LANES] = jnp.sum(lse, axis=0, keepdims=True)
        new_row = jnp.concatenate([kn, vn], axis=1)
        new_col = jnp.sum(jnp.where(eye, new_row, 0.0), axis=-1, keepdims=True)
        rolled = pltpu.roll(cache, window - 1, axis=1)
        last = lax.broadcasted_iota(jnp.int32, (n_feat, window), 1) == window - 1
        n_ref[0] = jnp.where(last, new_col, rolled)


def _swa_step(q, kvt, caches):
    nb = q.shape[0]
    row3 = lambda width: pl.BlockSpec((1, 1, width), lambda b: (b, 0, 0))
    cache_specs = [pl.BlockSpec((1, 2 * GROUP_W, win), lambda b: (b, 0, 0)) for win, _ in GROUPS]
    outs = pl.pallas_call(
        _swa_step_kernel,
        grid=(nb,),
        in_specs=[row3(ATTN_W), row3(2 * ATTN_W)] + cache_specs,
        out_specs=tuple(cache_specs) + (row3(ATTN_W), row3(3 * LANES)),
        out_shape=tuple(jax.ShapeDtypeStruct(c.shape, F32) for c in caches)
                  + (jax.ShapeDtypeStruct((nb, 1, ATTN_W), BF16),
                     jax.ShapeDtypeStruct((nb, 1, 3 * LANES), F32)),
        compiler_params=_params(1),
        name="swa_step",
    )(q.reshape(nb, 1, ATTN_W), kvt.reshape(nb, 1, 2 * ATTN_W), *caches)
    return list(outs[:3]), outs[3].reshape(nb, ATTN_W), outs[4].reshape(nb, 3 * LANES)


def _position_order(o_ref, l_ref, dil, nat_ref, slab0):
    tm = o_ref.shape[0]
    n = tm // dil
    for r in range(dil):
        rows = slice(r * n, (r + 1) * n)
        dst = pl.ds(r, n, stride=dil)
        nat_ref[slab0, dst, :] = o_ref[rows, :LANES].astype(F32)
        nat_ref[slab0 + 1, dst, :] = o_ref[rows, LANES:].astype(F32)
        nat_ref[slab0 + 2, dst, :] = l_ref[rows, :]
    o = jnp.concatenate([nat_ref[slab0], nat_ref[slab0 + 1]], axis=1)
    return o, nat_ref[slab0 + 2]


def _mix_ffn_kernel(x1_ref, ys_ref, o0_ref, o1_ref, o2_ref, l0_ref, l1_ref, l2_ref,
                    gm_ref, wgt_ref, wsp_ref, wap_ref, wo_ref, seget_ref,
                    g2_ref, wg_ref, wu_ref, wd_ref, y_ref, *scratch):
    x1 = x1_ref[...]
    h = _rms(x1, gm_ref[...]).astype(BF16)
    gates = jax.nn.sigmoid(_dot(h, wgt_ref[...]))
    os_ = [o0_ref[...].astype(F32)]
    ls = [l0_ref[...]]
    for g, (o_ref, l_ref) in enumerate(((o1_ref, l1_ref), (o2_ref, l2_ref)), start=1):
        if scratch:
            o, l = _position_order(o_ref, l_ref, GROUPS[g][1], scratch[0], 3 * (g - 1))
        else:
            o, l = o_ref[...].astype(F32), l_ref[...]
        os_.append(o)
        ls.append(l)
    l_top = jnp.maximum(jnp.maximum(ls[0], ls[1]), ls[2])
    es = [jnp.exp(l - l_top) for l in ls]
    inv = 1.0 / (es[0] + es[1] + es[2])
    seget = seget_ref[...]
    y_attn = None
    for e, o in zip(es, os_):
        t = _split_dot(e * inv, seget) * o
        y_attn = t if y_attn is None else y_attn + t
    mixed = (gates[:, :D_MODEL] * _dot(ys_ref[...], wsp_ref[...])
             + gates[:, D_MODEL:] * _dot(y_attn.astype(BF16), wap_ref[...]))
    x2 = x1 + _dot(mixed.astype(BF16), wo_ref[...])
    y_ref[...] = _ffn(x2, g2_ref, wg_ref, wu_ref, wd_ref)


def _mix_ffn(x1, y_ssm, os_, ls_, w, tm, residue_major):
    rows = x1.shape[0]
    row_spec = lambda width: pl.BlockSpec((tm, width), lambda i: (i, 0))
    n_slabs = 3 * (len(GROUPS) - 1)
    scratch = [pltpu.VMEM((n_slabs, tm, LANES), F32)] if residue_major else []
    return pl.pallas_call(
        _mix_ffn_kernel,
        grid=(rows // tm,),
        in_specs=[row_spec(D_MODEL), row_spec(SSM_W)] + [row_spec(GROUP_W)] * 3 + [row_spec(LANES)] * 3
                 + [_const_spec((1, D_MODEL)), _const_spec((D_MODEL, 2 * D_MODEL)),
                    _const_spec((SSM_W, D_MODEL)), _const_spec((GROUP_W, D_MODEL)),
                    _const_spec((D_MODEL, D_MODEL)), _const_spec((LANES, GROUP_W)),
                    _const_spec((1, D_MODEL)), _const_spec((D_MODEL, D_FF)), _const_spec((D_MODEL, D_FF)),
                    _const_spec((D_FF, D_MODEL))],
        out_specs=row_spec(D_MODEL),
        out_shape=jax.ShapeDtypeStruct((rows, D_MODEL), F32),
        scratch_shapes=scratch,
        compiler_params=_params(1),
        name="mix_ffn",
    )(x1, y_ssm, *os_, *ls_, w["g_mix"], w["w_gates"], w["w_ssm_proj"], w["w_attn_proj"], w["w_o"],
      w["seg_et"], w["g_ffn2"], w["w2_gate"], w["w2_up"], w["w2_down"])


def _prepare_weights(g_ffn1, w1_gate, w1_up, w1_down, g_mix, w_in, g_q, g_k,
                     ssm_a_re, ssm_a_im, ssm_log_dt, ssm_b_re, ssm_b_im, ssm_c_re, ssm_c_im,
                     ssm_d, w_glu, b_glu, w_ssm_proj, w_attn_proj, w_o, g_ffn2, w2_gate, w2_up, w2_down):
    w = {}
    row = lambda a: a.reshape(1, -1).astype(F32)
    w["g_ffn1"], w["g_mix"], w["g_ffn2"] = row(g_ffn1), row(g_mix), row(g_ffn2)
    for name, a in (("w1_gate", w1_gate), ("w1_up", w1_up), ("w1_down", w1_down),
                    ("w2_gate", w2_gate), ("w2_up", w2_up), ("w2_down", w2_down),
                    ("w_glu", w_glu), ("w_ssm_proj", w_ssm_proj), ("w_attn_proj", w_attn_proj),
                    ("w_o", w_o)):
        w[name] = a.astype(BF16)
    w["w_in_a"] = w_in[:, :IN_A].astype(BF16)
    w["w_gates"] = w_in[:, IN_A:].astype(BF16)
    per_head = lambda g: jnp.broadcast_to(g[:, None, :], (len(GROUPS), HEADS, HEAD_DIM)).reshape(1, ATTN_W)
    w["g_q"], w["g_k"] = per_head(g_q.astype(F32)), per_head(g_k.astype(F32))
    head_of_lane = jnp.arange(GROUP_W) // HEAD_DIM
    w["seg_mean"] = ((head_of_lane[:, None] == head_of_lane[None, :]) / HEAD_DIM).astype(BF16)
    w["seg_et"] = (jnp.arange(LANES)[:, None] == head_of_lane[None, :]).astype(BF16)
    w["ssm_d"], w["b_glu"] = row(ssm_d), row(b_glu)

    pw, w["k8"], w["w7"], w["gq"] = _ssm_prep(ssm_a_re, ssm_a_im, ssm_log_dt, ssm_b_re, ssm_b_im,
                                               ssm_c_re, ssm_c_im)
    per_group = lambda a: a.reshape(SSM_GROUPS, SSM_CH, SSM_P)[:, 0].reshape(1, N_STATE)
    w["ab_re"], w["ab_im"] = per_group(pw[0, 0]), per_group(pw[0, 1])
    w["a8_re"], w["a8_im"] = per_group(pw[1, 0]), per_group(pw[1, 1])
    return w


def _layer(x, state, caches, w):
    nb, seq, _ = x.shape
    rows = nb * seq
    xf = x.reshape(rows, D_MODEL)
    if caches is None:
        tm = TILE
        assert seq % tm == 0 and state is None
        (x1, u, q, k, v), new_caches = _ffn_in(xf, w, tm, seq // tm, True)
        y_ssm, h_re, h_im = _s5(u.reshape(nb, seq, SSM_W), w, min(S5_T, seq))
        y_ssm = y_ssm.reshape(rows, SSM_W)
        q3, k3, v3 = (a.reshape(nb, seq, ATTN_W) for a in (q, k, v))
        os_, ls_ = zip(*[_swa(q3, k3, v3, g, tm) for g in range(len(GROUPS))])
        y = _mix_ffn(x1, y_ssm, os_, ls_, w, tm, True)
    else:
        assert seq == 1
        (x1, u, q, k, v), kvt = _ffn_in(xf, w, rows, 1, False)
        y_ssm, h_re, h_im = _s5_step(u, state[0], state[1], w)
        new_caches, o_all, l_all = _swa_step(q, kvt, caches)
        os_ = [o_all[:, g * GROUP_W:(g + 1) * GROUP_W] for g in range(len(GROUPS))]
        ls_ = [l_all[:, g * LANES:(g + 1) * LANES] for g in range(len(GROUPS))]
        y = _mix_ffn(x1, y_ssm, os_, ls_, w, rows, False)
    return y.reshape(nb, seq, D_MODEL), (h_re, h_im), new_caches


def _to_features_major(c):
    nb, win = c.shape[:2]
    return jnp.transpose(c, (0, 2, 3, 4, 1)).reshape(nb, 2 * GROUP_W, win)


def _to_window_buffer(c):
    nb, _, win = c.shape
    return jnp.transpose(c.reshape(nb, 2, HEADS, HEAD_DIM, win), (0, 4, 1, 2, 3))[None]


def kernel(x_prompt, x_sample, cache_kv_w128, cache_kv_w512, cache_kv_w2048, state_ssm_re, state_ssm_im, g_ffn1, w1_gate, w1_up, w1_down, g_mix, w_in, g_q, g_k, ssm_a_re, ssm_a_im, ssm_log_dt, ssm_b_re, ssm_b_im, ssm_c_re, ssm_c_im, ssm_d, w_glu, b_glu, w_ssm_proj, w_attn_proj, w_o, g_ffn2, w2_gate, w2_up, w2_down):
    layer_weights = (g_ffn1, w1_gate, w1_up, w1_down, g_mix, w_in, g_q, g_k,
                     ssm_a_re, ssm_a_im, ssm_log_dt, ssm_b_re, ssm_b_im, ssm_c_re, ssm_c_im,
                     ssm_d, w_glu, b_glu, w_ssm_proj, w_attn_proj, w_o, g_ffn2, w2_gate, w2_up, w2_down)
    depth = g_ffn1.shape[0]
    assert depth == 1, "window caches of deeper layers would need the previous layer's outputs"
    w = _prepare_weights(*(a[0] for a in layer_weights))
    nb_s = x_sample.shape[0]
    sdt = state_ssm_re.dtype
    as_state = lambda h: h.reshape(1, -1, SSM_GROUPS, SSM_P).astype(sdt)

    y_p, h_p, kv_p = _layer(x_prompt, None, None, w)

    caches = [_to_features_major(c[0]) for c in (cache_kv_w128, cache_kv_w512, cache_kv_w2048)]
    for c, (window, _) in zip(caches, GROUPS):
        assert c.shape[2] == window, "the sample group's caches must hold a full window"
    state_s = (state_ssm_re[0].reshape(nb_s, N_STATE).astype(F32),
               state_ssm_im[0].reshape(nb_s, N_STATE).astype(F32))
    y_s, h_s, kv_s = _layer(x_sample, state_s, caches, w)

    return ((y_p, y_s) + tuple(_to_window_buffer(c) for c in kv_p) + (as_state(h_p[0]), as_state(h_p[1]))
            + tuple(_to_window_buffer(c) for c in kv_s) + (as_state(h_s[0]), as_state(h_s[1])))
```

```python
import functools

import jax
import jax.numpy as jnp
from jax import lax
from jax.experimental import pallas as pl
from jax.experimental.pallas import tpu as pltpu

F32 = jnp.float32
BF16 = jnp.bfloat16

D_MODEL = 1024
D_FF = 2816
HEAD_DIM = 64
HEADS = 4
GROUPS = ((128, 1), (512, 4), (2048, 16))
KEYS_PER_QUERY = 128
GROUP_W = HEADS * HEAD_DIM
ATTN_W = len(GROUPS) * GROUP_W
SSM_W = 512
SSM_GROUPS = 32
SSM_CH = 16
SSM_P = 64
N_STATE = SSM_GROUPS * SSM_P
IN_A = SSM_W + 3 * ATTN_W
RMS_EPS = 1e-6
NEG = -1e30
LANES = 128

TILE = 512
FF_CHUNKS = ((0, 1024), (1024, 1024), (2048, 768))
BLK = 8
NQ = SSM_W // LANES
QS = N_STATE // NQ
S5_T = 1024
VMEM_LIMIT = 56 * 1024 * 1024


def _const_spec(shape):
    nd = len(shape)
    return pl.BlockSpec(shape, lambda *_: (0,) * nd, pipeline_mode=pl.Buffered(1))


def _params(n_grid):
    return pltpu.CompilerParams(dimension_semantics=("arbitrary",) * n_grid,
                                vmem_limit_bytes=VMEM_LIMIT)


def _rms(x, g):
    ms = jnp.mean(x * x, axis=-1, keepdims=True)
    return x * lax.rsqrt(ms + RMS_EPS) * g


def _dot(a, b):
    return jnp.dot(a, b, preferred_element_type=F32)


def _dot_nt(a, b):
    return lax.dot_general(a, b, (((1,), (1,)), ((), ())), preferred_element_type=F32)


def _split_dot(x, m):
    hi = x.astype(BF16)
    lo = (x - hi.astype(F32)).astype(BF16)
    return _dot(hi, m) + _dot(lo, m)


def _ffn(x, g_ref, wg_ref, wu_ref, wd_ref):
    xn = _rms(x, g_ref[...]).astype(BF16)
    acc = None
    for f0, fw in FF_CHUNKS:
        hg = _dot(xn, wg_ref[:, f0:f0 + fw])
        hu = _dot(xn, wu_ref[:, f0:f0 + fw])
        a = (hg * jax.nn.sigmoid(hg) * hu).astype(BF16)
        d = _dot(a, wd_ref[f0:f0 + fw, :])
        acc = d if acc is None else acc + d
    return x + 0.5 * acc


def _split3_dot_nt(a, b):
    a_hi = a.astype(BF16)
    a_lo = (a - a_hi.astype(F32)).astype(BF16)
    b_hi = b.astype(BF16)
    b_lo = (b - b_hi.astype(F32)).astype(BF16)
    return _dot_nt(a_hi, b_hi) + _dot_nt(a_hi, b_lo) + _dot_nt(a_lo, b_hi)


def _prep_kernel(are_ref, aim_ref, ldt_ref, bre_ref, bim_ref, cre_ref, cim_ref,
                 pw_ref, k8_ref, w7_ref, gq_ref):
    a_re = are_ref[...]
    a_im = aim_ref[...]
    dt = jnp.exp(ldt_ref[...])
    mag = jnp.exp(a_re * dt)
    ab_re = mag * jnp.cos(a_im * dt)
    ab_im = mag * jnp.sin(a_im * dt)
    inv = 1.0 / (a_re * a_re + a_im * a_im)
    f_re = ((ab_re - 1.0) * a_re + ab_im * a_im) * inv
    f_im = (ab_im * a_re - (ab_re - 1.0) * a_im) * inv
    b_re = bre_ref[...]
    b_im = bim_ref[...]
    bb_re = f_re * b_re - f_im * b_im
    bb_im = f_re * b_im + f_im * b_re
    c_re = cre_ref[...]
    c_im = cim_ref[...]
    gc = SSM_GROUPS * SSM_CH
    same_group = (lax.broadcasted_iota(jnp.int32, (gc, gc), 0) // SSM_CH
                  == lax.broadcasted_iota(jnp.int32, (gc, gc), 1) // SSM_CH)
    spread = (lax.broadcasted_iota(jnp.int32, (SSM_P, QS), 1) % SSM_P
              == lax.broadcasted_iota(jnp.int32, (SSM_P, QS), 0)).astype(BF16)
    spread_t = (lax.broadcasted_iota(jnp.int32, (QS, SSM_P), 0) % SSM_P
                == lax.broadcasted_iota(jnp.int32, (QS, SSM_P), 1)).astype(BF16)
    mask_w = (lax.broadcasted_iota(jnp.int32, (LANES, QS), 0) // SSM_CH
              == lax.broadcasted_iota(jnp.int32, (LANES, QS), 1) // SSM_P)
    mask_g = (lax.broadcasted_iota(jnp.int32, (QS, LANES), 0) // SSM_P
              == lax.broadcasted_iota(jnp.int32, (QS, LANES), 1) // SSM_CH)
    p_re = jnp.ones_like(ab_re)
    p_im = jnp.zeros_like(ab_re)
    for i in range(BLK + 1):
        if i == 1:
            pw_ref[0, 0] = p_re
            pw_ref[0, 1] = p_im
        if i == BLK:
            pw_ref[1, 0] = p_re
            pw_ref[1, 1] = p_im
        if i >= 1:
            r = i - 1
            gcs = (c_re * p_re - c_im * p_im, -(c_re * p_im + c_im * p_re))
            for q in range(NQ):
                for part in range(2):
                    slab = gcs[part][q * LANES:(q + 1) * LANES].astype(BF16)
                    blk = jnp.where(mask_g, _dot_nt(spread_t, slab), 0.0)
                    gq_ref[q, part * QS:(part + 1) * QS, r * LANES:(r + 1) * LANES] = blk.astype(BF16)
        if i < BLK:
            r = BLK - 1 - i
            xs = (p_re * bb_re - p_im * bb_im, p_re * bb_im + p_im * bb_re)
            kk = jnp.where(same_group, _split3_dot_nt(xs[0], c_re) - _split3_dot_nt(xs[1], c_im), 0.0)
            for q in range(NQ):
                rows = slice(q * LANES, (q + 1) * LANES)
                k8_ref[q, :, i * LANES:(i + 1) * LANES] = kk[rows, rows].astype(BF16)
                for part in range(2):
                    blk = jnp.where(mask_w, _dot(xs[part][rows].astype(BF16), spread), 0.0)
                    w7_ref[q, r * LANES:(r + 1) * LANES, part * QS:(part + 1) * QS] = blk.astype(BF16)
        p_re, p_im = p_re * ab_re - p_im * ab_im, p_re * ab_im + p_im * ab_re


def _ssm_prep(a_re, a_im, log_dt, b_re, b_im, c_re, c_im):
    gc = SSM_GROUPS * SSM_CH
    rep = lambda a: jnp.repeat(a.astype(F32), SSM_CH, axis=0)
    b_rows = lambda b: jnp.transpose(b.astype(F32), (0, 2, 1)).reshape(gc, SSM_P)
    c_rows = lambda c: c.astype(F32).reshape(gc, SSM_P)
    return pl.pallas_call(
        _prep_kernel,
        out_shape=(jax.ShapeDtypeStruct((2, 2, gc, SSM_P), F32),
                   jax.ShapeDtypeStruct((NQ, LANES, BLK * LANES), BF16),
                   jax.ShapeDtypeStruct((NQ, BLK * LANES, 2 * QS), BF16),
                   jax.ShapeDtypeStruct((NQ, 2 * QS, BLK * LANES), BF16)),
        compiler_params=pltpu.CompilerParams(vmem_limit_bytes=VMEM_LIMIT),
        name="ssm_prep",
    )(rep(a_re), rep(a_im), rep(log_dt.reshape(SSM_GROUPS, 1)), b_rows(b_re), b_rows(b_im),
      c_rows(c_re), c_rows(c_im))


def _head_norm(t, g, seg):
    sq = t * t
    ms = jnp.concatenate(
        [_split_dot(sq[:, s * GROUP_W:(s + 1) * GROUP_W], seg) for s in range(len(GROUPS))], axis=1)
    return t * lax.rsqrt(ms + RMS_EPS) * g


def _store_attn_rows(val, out_ref, perm_ref, slab0):
    tm = val.shape[0]
    out_ref[:, :GROUP_W] = val[:, :GROUP_W].astype(BF16)
    for g in range(1, len(GROUPS)):
        dil = GROUPS[g][1]
        n = tm // dil
        for half in range(GROUP_W // LANES):
            c0 = g * GROUP_W + half * LANES
            col = val[:, c0:c0 + LANES]
            if perm_ref is None:
                out_ref[:, c0:c0 + LANES] = col.astype(BF16)
                continue
            slab = slab0 + (g - 1) * (GROUP_W // LANES) + half
            perm_ref[slab] = col
            for r in range(dil):
                out_ref[r * n:(r + 1) * n, c0:c0 + LANES] = (
                    perm_ref[slab, pl.ds(r, n, stride=dil), :].astype(BF16))


def _ffn_in_kernel(x_ref, g1_ref, wg_ref, wu_ref, wd_ref, gm_ref, win_ref, gq_ref, gk_ref, seg_ref,
                   x1_ref, u_ref, q_ref, k_ref, v_ref, *rest, tiles_per_seq, keeps):
    perm_ref = rest[-1] if len(rest) > 1 else None
    x1 = _ffn(x_ref[...], g1_ref, wg_ref, wu_ref, wd_ref)
    x1_ref[...] = x1
    h = _rms(x1, gm_ref[...]).astype(BF16)
    proj = _dot(h, win_ref[...])
    u_ref[...] = proj[:, :SSM_W]
    seg = seg_ref[...]
    q = _head_norm(proj[:, SSM_W:SSM_W + ATTN_W], gq_ref[...], seg)
    k = _head_norm(proj[:, SSM_W + ATTN_W:SSM_W + 2 * ATTN_W], gk_ref[...], seg)
    v = proj[:, SSM_W + 2 * ATTN_W:]
    slabs = (len(GROUPS) - 1) * (GROUP_W // LANES)
    _store_attn_rows(q * HEAD_DIM ** -0.5, q_ref, perm_ref, 0)
    _store_attn_rows(k, k_ref, perm_ref, slabs)
    _store_attn_rows(v, v_ref, perm_ref, 2 * slabs)
    if perm_ref is None:
        rest[0][:, :ATTN_W] = k
        rest[0][:, ATTN_W:] = v
        return
    tm = x1.shape[0]
    j = pl.program_id(0) % tiles_per_seq
    for g, keep in enumerate(keeps):
        lanes = slice(g * GROUP_W, (g + 1) * GROUP_W)
        cols = min(keep, tm)

        @pl.when(j >= tiles_per_seq - pl.cdiv(keep, tm))
        def _(g=g, lanes=lanes, cols=cols):
            rest[g][0, :GROUP_W, :] = k[:, lanes].T[:, tm - cols:]
            rest[g][0, GROUP_W:, :] = v[:, lanes].T[:, tm - cols:]


def _ffn_in(x, w, tm, tiles_per_seq, prompt_form):
    rows = x.shape[0]
    n_tiles = rows // tm
    n_seq = n_tiles // tiles_per_seq
    row_spec = lambda width: pl.BlockSpec((tm, width), lambda i: (i, 0))
    if prompt_form:
        keeps = tuple(min(window, tiles_per_seq * tm) for window, _ in GROUPS)
        assert all(keep % tm == 0 or keep < tm for keep in keeps)

        def tail_spec(keep):
            first = tiles_per_seq - pl.cdiv(keep, tm)
            return pl.BlockSpec((1, 2 * GROUP_W, min(keep, tm)),
                                lambda i: (i // tiles_per_seq, 0, jnp.maximum(i % tiles_per_seq - first, 0)))

        tail_shapes = tuple(jax.ShapeDtypeStruct((n_seq, 2 * GROUP_W, keep), F32) for keep in keeps)
        tail_specs = tuple(tail_spec(keep) for keep in keeps)
        scratch = [pltpu.VMEM((3 * (len(GROUPS) - 1) * (GROUP_W // LANES), tm, LANES), F32)]
    else:
        assert tiles_per_seq == 1
        keeps = ()
        tail_shapes = (jax.ShapeDtypeStruct((rows, 2 * ATTN_W), F32),)
        tail_specs = (row_spec(2 * ATTN_W),)
        scratch = []
    out_shape = (jax.ShapeDtypeStruct((rows, D_MODEL), F32),
                 jax.ShapeDtypeStruct((rows, SSM_W), F32),
                 jax.ShapeDtypeStruct((rows, ATTN_W), BF16),
                 jax.ShapeDtypeStruct((rows, ATTN_W), BF16),
                 jax.ShapeDtypeStruct((rows, ATTN_W), BF16)) + tail_shapes
    outs = pl.pallas_call(
        functools.partial(_ffn_in_kernel, tiles_per_seq=tiles_per_seq, keeps=keeps),
        grid=(n_tiles,),
        in_specs=[row_spec(D_MODEL), _const_spec((1, D_MODEL)),
                  _const_spec((D_MODEL, D_FF)), _const_spec((D_MODEL, D_FF)), _const_spec((D_FF, D_MODEL)),
                  _const_spec((1, D_MODEL)), _const_spec((D_MODEL, IN_A)),
                  _const_spec((1, ATTN_W)), _const_spec((1, ATTN_W)), _const_spec((GROUP_W, GROUP_W))],
        out_specs=(row_spec(D_MODEL), row_spec(SSM_W), row_spec(ATTN_W), row_spec(ATTN_W), row_spec(ATTN_W))
                  + tail_specs,
        out_shape=out_shape,
        scratch_shapes=scratch,
        compiler_params=_params(1),
        name="ffn_in",
    )(x, w["g_ffn1"], w["w1_gate"], w["w1_up"], w["w1_down"], w["g_mix"], w["w_in_a"],
      w["g_q"], w["g_k"], w["seg_mean"])
    return outs[:5], (list(outs[5:]) if prompt_form else outs[5])


def _glu_out(y, u, dsk_ref, wglu_ref, bglu_ref):
    y = jax.nn.gelu(y + dsk_ref[...] * u)
    z = _dot(y.astype(BF16), wglu_ref[...]) + bglu_ref[...]
    return (y * jax.nn.sigmoid(z)).astype(BF16)


def _s5_kernel(u0_ref, u1_ref, u2_ref, u3_ref, a8re_ref, a8im_ref, k8_ref, w7_ref, gq_ref,
               dsk_ref, wglu_ref, bglu_ref, y_ref, hre_ref, him_ref,
               ls_ref, yq_ref, y2s_ref, sre_ref, sim_ref):
    u_refs = (u0_ref, u1_ref, u2_ref, u3_ref)
    nb, t_blk, _ = u0_ref.shape
    nblk = t_blk // BLK
    qw = 2 * QS

    @pl.when(pl.program_id(0) == 0)
    def _():
        sre_ref[...] = jnp.zeros_like(sre_ref)
        sim_ref[...] = jnp.zeros_like(sim_ref)

    sub = lax.broadcasted_iota(jnp.int32, (nblk, BLK, LANES), 1)
    for q in range(NQ):
        ublks = []
        for b in range(nb):
            uq = u_refs[q][b]
            z3 = _dot(uq.astype(BF16), k8_ref[q]).reshape(nblk, BLK, BLK * LANES)
            acc = z3[:, :, :LANES]
            for i in range(1, BLK):
                zi = z3[:, :, LANES * i:LANES * (i + 1)]
                acc = acc + jnp.where(sub >= i, pltpu.roll(zi, i, axis=1), 0.0)
            yq_ref[b * NQ + q] = acc.reshape(t_blk, LANES)
            ublks.append(jnp.concatenate(
                [u_refs[q][b, pl.ds(r, nblk, stride=BLK), :] for r in range(BLK)], axis=1))
        ls = _dot(jnp.concatenate(ublks, axis=0).astype(BF16), w7_ref[q])
        for b in range(nb):
            ls_ref[b, :, q * qw:(q + 1) * qw] = ls[b * nblk:(b + 1) * nblk]

    for q in range(NQ):
        ch = slice(q * QS, (q + 1) * QS)
        c_re = q * qw
        c_im = c_re + QS
        a_re = a8re_ref[:, ch]
        a_im = a8im_ref[:, ch]

        def body(k, carry, c_re=c_re, c_im=c_im, a_re=a_re, a_im=a_im):
            row = pl.ds(k, 1)
            out = []
            for b in range(nb):
                s_re, s_im = carry[2 * b], carry[2 * b + 1]
                l_re = ls_ref[b, row, c_re:c_re + QS]
                l_im = ls_ref[b, row, c_im:c_im + QS]
                ls_ref[b, row, c_re:c_re + QS] = s_re
                ls_ref[b, row, c_im:c_im + QS] = s_im
                out.append(a_re * s_re - a_im * s_im + l_re)
                out.append(a_re * s_im + a_im * s_re + l_im)
            return tuple(out)

        init = []
        for b in range(nb):
            init += [sre_ref[b:b + 1, ch], sim_ref[b:b + 1, ch]]
        fin = lax.fori_loop(0, nblk, body, tuple(init), unroll=8)
        for b in range(nb):
            sre_ref[b:b + 1, ch] = fin[2 * b]
            sim_ref[b:b + 1, ch] = fin[2 * b + 1]

    for q in range(NQ):
        sc = jnp.concatenate([ls_ref[b, :, q * qw:(q + 1) * qw] for b in range(nb)], axis=0)
        y2 = _dot(sc.astype(BF16), gq_ref[q])
        for b in range(nb):
            for r in range(BLK):
                y2s_ref[pl.ds(r, nblk, stride=BLK), :] = y2[b * nblk:(b + 1) * nblk, LANES * r:LANES * (r + 1)]
            yq_ref[b * NQ + q] = yq_ref[b * NQ + q] + y2s_ref[...]

    for b in range(nb):
        y = jnp.concatenate([yq_ref[b * NQ + q] for q in range(NQ)], axis=1)
        u = jnp.concatenate([u_refs[q][b] for q in range(NQ)], axis=1)
        y_ref[b] = _glu_out(y, u, dsk_ref, wglu_ref, bglu_ref)
    hre_ref[...] = sre_ref[...]
    him_ref[...] = sim_ref[...]


def _s5(u, w, t_blk):
    nb, seq, _ = u.shape
    nblk = t_blk // BLK
    u_specs = [pl.BlockSpec((nb, t_blk, LANES), lambda c, q=q: (0, c, q)) for q in range(NQ)]
    state = pl.BlockSpec((nb, N_STATE), lambda c: (0, 0))
    return pl.pallas_call(
        _s5_kernel,
        grid=(seq // t_blk,),
        in_specs=u_specs + [_const_spec((1, N_STATE)), _const_spec((1, N_STATE)),
                            _const_spec((NQ, LANES, BLK * LANES)),
                            _const_spec((NQ, BLK * LANES, 2 * QS)), _const_spec((NQ, 2 * QS, BLK * LANES)),
                            _const_spec((1, SSM_W)), _const_spec((SSM_W, SSM_W)), _const_spec((1, SSM_W))],
        out_specs=(pl.BlockSpec((nb, t_blk, SSM_W), lambda c: (0, c, 0)), state, state),
        out_shape=(jax.ShapeDtypeStruct((nb, seq, SSM_W), BF16),
                   jax.ShapeDtypeStruct((nb, N_STATE), F32), jax.ShapeDtypeStruct((nb, N_STATE), F32)),
        scratch_shapes=[pltpu.VMEM((nb, nblk, NQ * 2 * QS), F32),
                        pltpu.VMEM((nb * NQ, t_blk, LANES), F32),
                        pltpu.VMEM((t_blk, LANES), F32),
                        pltpu.VMEM((nb, N_STATE), F32), pltpu.VMEM((nb, N_STATE), F32)],
        compiler_params=_params(1),
        name="s5",
    )(u, u, u, u, w["a8_re"], w["a8_im"], w["k8"], w["w7"], w["gq"], w["ssm_d"], w["w_glu"], w["b_glu"])


def _s5_step_kernel(u_ref, hre_ref, him_ref, are_ref, aim_ref, k8_ref, w7_ref, gq_ref,
                    dsk_ref, wglu_ref, bglu_ref, y_ref, nre_ref, nim_ref):
    u = u_ref[...]
    ys = []
    for q in range(NQ):
        ch = slice(q * QS, (q + 1) * QS)
        uq = u[:, q * LANES:(q + 1) * LANES].astype(BF16)
        x = _dot(uq, w7_ref[q, (BLK - 1) * LANES:, :])
        a_re, a_im = are_ref[:, ch], aim_ref[:, ch]
        s_re, s_im = hre_ref[:, ch], him_ref[:, ch]
        nre_ref[:, ch] = a_re * s_re - a_im * s_im + x[:, :QS]
        nim_ref[:, ch] = a_re * s_im + a_im * s_re + x[:, QS:]
        s = jnp.concatenate([s_re, s_im], axis=1).astype(BF16)
        ys.append(_dot(s, gq_ref[q, :, :LANES]) + _dot(uq, k8_ref[q, :, :LANES]))
    y_ref[...] = _glu_out(jnp.concatenate(ys, axis=1), u, dsk_ref, wglu_ref, bglu_ref)


def _s5_step(u, h_re, h_im, w):
    rows = u.shape[0]
    return pl.pallas_call(
        _s5_step_kernel,
        out_shape=(jax.ShapeDtypeStruct((rows, SSM_W), BF16),
                   jax.ShapeDtypeStruct((rows, N_STATE), F32), jax.ShapeDtypeStruct((rows, N_STATE), F32)),
        compiler_params=pltpu.CompilerParams(vmem_limit_bytes=VMEM_LIMIT),
        name="s5_step",
    )(u, h_re, h_im, w["ab_re"], w["ab_im"], w["k8"], w["w7"], w["gq"], w["ssm_d"], w["w_glu"], w["b_glu"])


def _step_attention(q, kn, vn, cache, dil):
    n_feat, window = cache.shape
    row8 = lax.broadcasted_iota(jnp.int32, (8, GROUP_W), 0)
    head8 = lax.broadcasted_iota(jnp.int32, (8, GROUP_W), 1) // HEAD_DIM == row8
    diag8 = (lax.broadcasted_iota(jnp.int32, (8, LANES), 0)
             == lax.broadcasted_iota(jnp.int32, (8, LANES), 1))
    q8 = jnp.where(head8, jnp.broadcast_to(q, (8, GROUP_W)), 0.0)
    pos = lax.broadcasted_iota(jnp.int32, (8, window), 1)
    s = _dot(q8.astype(BF16), cache[:GROUP_W].astype(BF16))
    s = jnp.where(jnp.bitwise_and(pos, dil - 1) == 0, s, NEG)
    s_new = jnp.sum(q8 * kn, axis=-1, keepdims=True)
    m = jnp.maximum(jnp.max(s, axis=-1, keepdims=True), s_new)
    p = jnp.exp(s - m)
    p_new = jnp.exp(s_new - m)
    den = jnp.sum(p, axis=-1, keepdims=True) + p_new
    o8 = (_dot_nt(p.astype(BF16), cache[GROUP_W:].astype(BF16)) + p_new * vn) / den
    o = jnp.sum(jnp.where(head8, o8, 0.0), axis=0, keepdims=True)
    lse = jnp.sum(jnp.where(diag8, m + jnp.log(den), 0.0), axis=0, keepdims=True)
    eye = (lax.broadcasted_iota(jnp.int32, (n_feat, n_feat), 0)
           == lax.broadcasted_iota(jnp.int32, (n_feat, n_feat), 1))
    new_col = jnp.sum(jnp.where(eye, jnp.concatenate([kn, vn], axis=1), 0.0), axis=-1, keepdims=True)
    rolled = pltpu.roll(cache, window - 1, axis=1)
    last = lax.broadcasted_iota(jnp.int32, (n_feat, window), 1) == window - 1
    return o, lse, jnp.where(last, new_col, rolled)


def _swa_kernel(q_ref, k_ref, v_ref, kp_ref, vp_ref, qs_ref, kvs_ref, c_ref,
                o_ref, l_ref, os_ref, ls_ref, n_ref, *, qb, chunk, dil, n_sample):
    first = pl.program_id(2) == 0

    def rows(ref):
        return jnp.concatenate([ref[t] for t in range(ref.shape[0])], axis=0)

    q = rows(q_ref)
    kk = jnp.concatenate([rows(kp_ref), rows(k_ref)], axis=0)
    vv = jnp.concatenate([rows(vp_ref), rows(v_ref)], axis=0)
    nk = 2 * KEYS_PER_QUERY
    row = lax.broadcasted_iota(jnp.int32, (128, nk), 0)
    col = lax.broadcasted_iota(jnp.int32, (128, nk), 1)
    band = jnp.where(col >= row, jnp.where(col <= row + KEYS_PER_QUERY, 0.0, NEG), NEG)
    band_first = band + jnp.where(col < KEYS_PER_QUERY, jnp.where(first, NEG, 0.0), 0.0)
    lane = lax.broadcasted_iota(jnp.int32, (1, GROUP_W), 1) // HEAD_DIM
    lane_l = lax.broadcasted_iota(jnp.int32, (1, LANES), 1)
    for j in range(qb // 128):
        qj = q[128 * j:128 * (j + 1)]
        kj = kk[128 * j:128 * j + nk]
        vj = vv[128 * j:128 * j + nk]
        bias = band_first if j == 0 else band
        qs = jnp.concatenate([jnp.where(lane == h, qj, jnp.zeros_like(qj)) for h in range(HEADS)], axis=0)
        s = (_dot_nt(qs, kj).reshape(HEADS, 128, nk) + bias[None]).reshape(HEADS * 128, nk)
        m = jnp.max(s, axis=-1, keepdims=True)
        p = jnp.exp(s - m)
        den = jnp.sum(p, axis=-1, keepdims=True)
        on = _dot(p.astype(BF16), vj) * (1.0 / den)
        lse = m + jnp.log(den)
        o_acc = jnp.zeros((128, GROUP_W), F32)
        l_acc = jnp.zeros((128, LANES), F32)
        for h in range(HEADS):
            rows_h = slice(128 * h, 128 * (h + 1))
            o_acc = jnp.where(lane == h, on[rows_h], o_acc)
            l_acc = jnp.where(lane_l == h, lse[rows_h], l_acc)
        o_out = o_acc.astype(BF16)
        if chunk >= 128:
            t, r0 = divmod(128 * j, chunk)
            o_ref[t, r0:r0 + 128, :] = o_out
            l_ref[t, r0:r0 + 128, :] = l_acc
        else:
            for s_ in range(128 // chunk):
                t = (128 * j) // chunk + s_
                o_ref[t] = o_out[s_ * chunk:(s_ + 1) * chunk]
                l_ref[t] = l_acc[s_ * chunk:(s_ + 1) * chunk]

    step = (pl.program_id(0) * pl.num_programs(1) + pl.program_id(1)) * pl.num_programs(2) + pl.program_id(2)

    @pl.when(step < n_sample)
    def _():
        kv = kvs_ref[0]
        o, lse, new_cache = _step_attention(qs_ref[0].astype(F32), kv[:, :GROUP_W], kv[:, GROUP_W:],
                                            c_ref[0], dil)
        os_ref[0] = o.astype(BF16)
        ls_ref[0] = lse
        n_ref[0] = new_cache


def _swa(q, k, v, g, tile, q_s, kv_s, cache):
    window, dil = GROUPS[g]
    nb, seq, _ = q.shape
    n_sample = q_s.shape[0]
    n_tiles = seq // tile
    chunk = tile // dil
    n = seq // dil
    qb = min(512, n)
    tpb = qb // chunk
    n_i = n // qb
    assert nb * dil * n_i >= n_sample and cache.shape == (n_sample, 2 * GROUP_W, window)
    view = lambda a: a.reshape(nb, n_tiles, dil, chunk, a.shape[-1])
    cur = pl.BlockSpec((None, tpb, None, chunk, GROUP_W), lambda b, r, i: (b, i, r, 0, g))
    if chunk >= 128:
        prev = pl.BlockSpec((None, 1, None, 128, GROUP_W),
                            lambda b, r, i: (b, jnp.maximum(i * tpb - 1, 0), r, chunk // 128 - 1, g))
    else:
        ptiles = 128 // chunk
        prev = pl.BlockSpec((None, ptiles, None, chunk, GROUP_W),
                            lambda b, r, i: (b, jnp.maximum(i * (tpb // ptiles) - 1, 0), r, 0, g))
    out_block = lambda width: pl.BlockSpec((None, tpb, None, chunk, width), lambda b, r, i: (b, i, r, 0, 0))
    sample_block = lambda *dims: pl.BlockSpec(
        (1,) + dims, lambda b, r, i: (jnp.minimum((b * dil + r) * n_i + i, n_sample - 1),) + (0,) * len(dims))
    o, lse, o_s, l_s, new_cache = pl.pallas_call(
        functools.partial(_swa_kernel, qb=qb, chunk=chunk, dil=dil, n_sample=n_sample),
        grid=(nb, dil, n_i),
        in_specs=[cur, cur, cur, prev, prev,
                  sample_block(1, GROUP_W), sample_block(1, 2 * GROUP_W), sample_block(2 * GROUP_W, window)],
        out_specs=(out_block(GROUP_W), out_block(LANES),
                   sample_block(1, GROUP_W), sample_block(1, LANES), sample_block(2 * GROUP_W, window)),
        out_shape=(jax.ShapeDtypeStruct((nb, n_tiles, dil, chunk, GROUP_W), BF16),
                   jax.ShapeDtypeStruct((nb, n_tiles, dil, chunk, LANES), F32),
                   jax.ShapeDtypeStruct((n_sample, 1, GROUP_W), BF16),
                   jax.ShapeDtypeStruct((n_sample, 1, LANES), F32),
                   jax.ShapeDtypeStruct(cache.shape, F32)),
        compiler_params=_params(3),
        name="swa_g%d" % g,
    )(view(q), view(k), view(v), view(k), view(v),
      q_s.reshape(n_sample, 1, GROUP_W), kv_s.reshape(n_sample, 1, 2 * GROUP_W), cache)
    return (o.reshape(nb * seq, GROUP_W), lse.reshape(nb * seq, LANES),
            o_s.reshape(n_sample, GROUP_W), l_s.reshape(n_sample, LANES), new_cache)


def _position_order(o_ref, l_ref, dil, nat_ref, slab0):
    tm = o_ref.shape[0]
    n = tm // dil
    for r in range(dil):
        rows = slice(r * n, (r + 1) * n)
        dst = pl.ds(r, n, stride=dil)
        nat_ref[slab0, dst, :] = o_ref[rows, :LANES].astype(F32)
        nat_ref[slab0 + 1, dst, :] = o_ref[rows, LANES:].astype(F32)
        nat_ref[slab0 + 2, dst, :] = l_ref[rows, :]
    o = jnp.concatenate([nat_ref[slab0], nat_ref[slab0 + 1]], axis=1)
    return o, nat_ref[slab0 + 2]


def _mix_ffn_kernel(x1_ref, ys_ref, o0_ref, o1_ref, o2_ref, l0_ref, l1_ref, l2_ref,
                    gm_ref, wgt_ref, wsp_ref, wap_ref, wo_ref, seget_ref,
                    g2_ref, wg_ref, wu_ref, wd_ref, y_ref, *scratch):
    x1 = x1_ref[...]
    h = _rms(x1, gm_ref[...]).astype(BF16)
    gates = jax.nn.sigmoid(_dot(h, wgt_ref[...]))
    os_ = [o0_ref[...].astype(F32)]
    ls = [l0_ref[...]]
    for g, (o_ref, l_ref) in enumerate(((o1_ref, l1_ref), (o2_ref, l2_ref)), start=1):
        if scratch:
            o, l = _position_order(o_ref, l_ref, GROUPS[g][1], scratch[0], 3 * (g - 1))
        else:
            o, l = o_ref[...].astype(F32), l_ref[...]
        os_.append(o)
        ls.append(l)
    l_top = jnp.maximum(jnp.maximum(ls[0], ls[1]), ls[2])
    es = [jnp.exp(l - l_top) for l in ls]
    inv = 1.0 / (es[0] + es[1] + es[2])
    seget = seget_ref[...]
    y_attn = None
    for e, o in zip(es, os_):
        t = _split_dot(e * inv, seget) * o
        y_attn = t if y_attn is None else y_attn + t
    mixed = (gates[:, :D_MODEL] * _dot(ys_ref[...], wsp_ref[...])
             + gates[:, D_MODEL:] * _dot(y_attn.astype(BF16), wap_ref[...]))
    x2 = x1 + _dot(mixed.astype(BF16), wo_ref[...])
    y_ref[...] = _ffn(x2, g2_ref, wg_ref, wu_ref, wd_ref)


def _mix_ffn(x1, y_ssm, os_, ls_, w, tm, residue_major):
    rows = x1.shape[0]
    row_spec = lambda width: pl.BlockSpec((tm, width), lambda i: (i, 0))
    n_slabs = 3 * (len(GROUPS) - 1)
    scratch = [pltpu.VMEM((n_slabs, tm, LANES), F32)] if residue_major else []
    return pl.pallas_call(
        _mix_ffn_kernel,
        grid=(rows // tm,),
        in_specs=[row_spec(D_MODEL), row_spec(SSM_W)] + [row_spec(GROUP_W)] * 3 + [row_spec(LANES)] * 3
                 + [_const_spec((1, D_MODEL)), _const_spec((D_MODEL, 2 * D_MODEL)),
                    _const_spec((SSM_W, D_MODEL)), _const_spec((GROUP_W, D_MODEL)),
                    _const_spec((D_MODEL, D_MODEL)), _const_spec((LANES, GROUP_W)),
                    _const_spec((1, D_MODEL)), _const_spec((D_MODEL, D_FF)), _const_spec((D_MODEL, D_FF)),
                    _const_spec((D_FF, D_MODEL))],
        out_specs=row_spec(D_MODEL),
        out_shape=jax.ShapeDtypeStruct((rows, D_MODEL), F32),
        scratch_shapes=scratch,
        compiler_params=_params(1),
        name="mix_ffn",
    )(x1, y_ssm, *os_, *ls_, w["g_mix"], w["w_gates"], w["w_ssm_proj"], w["w_attn_proj"], w["w_o"],
      w["seg_et"], w["g_ffn2"], w["w2_gate"], w["w2_up"], w["w2_down"])


def _prepare_weights(g_ffn1, w1_gate, w1_up, w1_down, g_mix, w_in, g_q, g_k,
                     ssm_a_re, ssm_a_im, ssm_log_dt, ssm_b_re, ssm_b_im, ssm_c_re, ssm_c_im,
                     ssm_d, w_glu, b_glu, w_ssm_proj, w_attn_proj, w_o, g_ffn2, w2_gate, w2_up, w2_down):
    w = {}
    row = lambda a: a.reshape(1, -1).astype(F32)
    w["g_ffn1"], w["g_mix"], w["g_ffn2"] = row(g_ffn1), row(g_mix), row(g_ffn2)
    for name, a in (("w1_gate", w1_gate), ("w1_up", w1_up), ("w1_down", w1_down),
                    ("w2_gate", w2_gate), ("w2_up", w2_up), ("w2_down", w2_down),
                    ("w_glu", w_glu), ("w_ssm_proj", w_ssm_proj), ("w_attn_proj", w_attn_proj),
                    ("w_o", w_o)):
        w[name] = a.astype(BF16)
    w["w_in_a"] = w_in[:, :IN_A].astype(BF16)
    w["w_gates"] = w_in[:, IN_A:].astype(BF16)
    per_head = lambda g: jnp.broadcast_to(g[:, None, :], (len(GROUPS), HEADS, HEAD_DIM)).reshape(1, ATTN_W)
    w["g_q"], w["g_k"] = per_head(g_q.astype(F32)), per_head(g_k.astype(F32))
    head_of_lane = jnp.arange(GROUP_W) // HEAD_DIM
    w["seg_mean"] = ((head_of_lane[:, None] == head_of_lane[None, :]) / HEAD_DIM).astype(BF16)
    w["seg_et"] = (jnp.arange(LANES)[:, None] == head_of_lane[None, :]).astype(BF16)
    w["ssm_d"], w["b_glu"] = row(ssm_d), row(b_glu)

    pw, w["k8"], w["w7"], w["gq"] = _ssm_prep(ssm_a_re, ssm_a_im, ssm_log_dt, ssm_b_re, ssm_b_im,
                                               ssm_c_re, ssm_c_im)
    per_group = lambda a: a.reshape(SSM_GROUPS, SSM_CH, SSM_P)[:, 0].reshape(1, N_STATE)
    w["ab_re"], w["ab_im"] = per_group(pw[0, 0]), per_group(pw[0, 1])
    w["a8_re"], w["a8_im"] = per_group(pw[1, 0]), per_group(pw[1, 1])
    return w


def _forward(x_p, x_s, state_s, caches_s, w):
    nb, seq, _ = x_p.shape
    n_s = x_s.shape[0]
    assert seq % TILE == 0 and x_s.shape[1] == 1
    (x1, u, q, k, v), caches_p = _ffn_in(x_p.reshape(nb * seq, D_MODEL), w, TILE, seq // TILE, True)
    (x1_s, u_s, q_s, _, _), kv_s = _ffn_in(x_s.reshape(n_s, D_MODEL), w, n_s, 1, False)
    y_ssm, *h_p = _s5(u.reshape(nb, seq, SSM_W), w, min(S5_T, seq))
    y_ssm_s, *h_s = _s5_step(u_s, state_s[0], state_s[1], w)
    q3, k3, v3 = (a.reshape(nb, seq, ATTN_W) for a in (q, k, v))
    os_p, ls_p, os_s, ls_s, new_caches_s = [], [], [], [], []
    for g in range(len(GROUPS)):
        lanes = slice(g * GROUP_W, (g + 1) * GROUP_W)
        kv_g = jnp.concatenate([kv_s[:, lanes], kv_s[:, ATTN_W + g * GROUP_W:ATTN_W + (g + 1) * GROUP_W]], axis=1)
        o, l, o_s, l_s, c_new = _swa(q3, k3, v3, g, TILE, q_s[:, lanes], kv_g, caches_s[g])
        for lst, a in zip((os_p, ls_p, os_s, ls_s, new_caches_s), (o, l, o_s, l_s, c_new)):
            lst.append(a)
    y_p = _mix_ffn(x1, y_ssm.reshape(nb * seq, SSM_W), os_p, ls_p, w, TILE, True)
    y_s = _mix_ffn(x1_s, y_ssm_s, os_s, ls_s, w, n_s, False)
    return (y_p.reshape(nb, seq, D_MODEL), y_s.reshape(n_s, 1, D_MODEL), h_p, h_s, caches_p, new_caches_s)


def _to_features_major(c):
    nb, win = c.shape[:2]
    return jnp.transpose(c, (0, 2, 3, 4, 1)).reshape(nb, 2 * GROUP_W, win)


def _to_window_buffer(c):
    nb, _, win = c.shape
    return jnp.transpose(c.reshape(nb, 2, HEADS, HEAD_DIM, win), (0, 4, 1, 2, 3))[None]


def kernel(x_prompt, x_sample, cache_kv_w128, cache_kv_w512, cache_kv_w2048, state_ssm_re, state_ssm_im, g_ffn1, w1_gate, w1_up, w1_down, g_mix, w_in, g_q, g_k, ssm_a_re, ssm_a_im, ssm_log_dt, ssm_b_re, ssm_b_im, ssm_c_re, ssm_c_im, ssm_d, w_glu, b_glu, w_ssm_proj, w_attn_proj, w_o, g_ffn2, w2_gate, w2_up, w2_down):
    layer_weights = (g_ffn1, w1_gate, w1_up, w1_down, g_mix, w_in, g_q, g_k,
                     ssm_a_re, ssm_a_im, ssm_log_dt, ssm_b_re, ssm_b_im, ssm_c_re, ssm_c_im,
                     ssm_d, w_glu, b_glu, w_ssm_proj, w_attn_proj, w_o, g_ffn2, w2_gate, w2_up, w2_down)
    depth = g_ffn1.shape[0]
    assert depth == 1, "window caches of deeper layers would need the previous layer's outputs"
    w = _prepare_weights(*(a[0] for a in layer_weights))
    nb_s = x_sample.shape[0]
    sdt = state_ssm_re.dtype
    as_state = lambda h: h.reshape(1, -1, SSM_GROUPS, SSM_P).astype(sdt)

    caches = [_to_features_major(c[0]) for c in (cache_kv_w128, cache_kv_w512, cache_kv_w2048)]
    state_s = (state_ssm_re[0].reshape(nb_s, N_STATE).astype(F32),
               state_ssm_im[0].reshape(nb_s, N_STATE).astype(F32))
    y_p, y_s, h_p, h_s, kv_p, kv_s = _forward(x_prompt, x_sample, state_s, caches, w)
    return ((y_p, y_s) + tuple(_to_window_buffer(c) for c in kv_p) + (as_state(h_p[0]), as_state(h_p[1]))
            + tuple(_to_window_buffer(c) for c in kv_s) + (as_state(h_s[0]), as_state(h_s[1])))
```

```python
import functools

import jax
import jax.numpy as jnp
from jax import lax
from jax.experimental import pallas as pl
from jax.experimental.pallas import tpu as pltpu

F32 = jnp.float32
BF16 = jnp.bfloat16

D_MODEL = 1024
D_FF = 2816
HEAD_DIM = 64
HEADS = 4
GROUPS = ((128, 1), (512, 4), (2048, 16))
KEYS_PER_QUERY = 128
GROUP_W = HEADS * HEAD_DIM
ATTN_W = len(GROUPS) * GROUP_W
SSM_W = 512
SSM_GROUPS = 32
SSM_CH = 16
SSM_P = 64
N_STATE = SSM_GROUPS * SSM_P
IN_A = SSM_W + 3 * ATTN_W
RMS_EPS = 1e-6
NEG = -1e30
LANES = 128

TILE = 512
FF_CHUNKS = ((0, 1024), (1024, 1024), (2048, 768))
BLK = 8
NQ = SSM_W // LANES
QS = N_STATE // NQ
S5_T = 1024
VMEM_LIMIT = 56 * 1024 * 1024


def _const_spec(shape):
    nd = len(shape)
    return pl.BlockSpec(shape, lambda *_: (0,) * nd, pipeline_mode=pl.Buffered(1))


def _params(n_grid):
    return pltpu.CompilerParams(dimension_semantics=("arbitrary",) * n_grid,
                                vmem_limit_bytes=VMEM_LIMIT)


def _rms(x, g):
    ms = jnp.mean(x * x, axis=-1, keepdims=True)
    return x * lax.rsqrt(ms + RMS_EPS) * g


def _dot(a, b):
    return jnp.dot(a, b, preferred_element_type=F32)


def _dot_nt(a, b):
    return lax.dot_general(a, b, (((1,), (1,)), ((), ())), preferred_element_type=F32)


def _split_dot(x, m):
    hi = x.astype(BF16)
    lo = (x - hi.astype(F32)).astype(BF16)
    return _dot(hi, m) + _dot(lo, m)


def _sigmoid(x):
    return 0.5 * jnp.tanh(0.5 * x) + 0.5


def _ffn(x, g_ref, wg_ref, wu_ref, wd_ref):
    xn = _rms(x, g_ref[...]).astype(BF16)
    acc = None
    for f0, fw in FF_CHUNKS:
        hg = _dot(xn, wg_ref[:, f0:f0 + fw])
        hu = _dot(xn, wu_ref[:, f0:f0 + fw])
        a = (hg * _sigmoid(hg) * hu).astype(BF16)
        d = _dot(a, wd_ref[f0:f0 + fw, :])
        acc = d if acc is None else acc + d
    return x + 0.5 * acc


def _split3_dot_nt(a, b):
    a_hi = a.astype(BF16)
    a_lo = (a - a_hi.astype(F32)).astype(BF16)
    b_hi = b.astype(BF16)
    b_lo = (b - b_hi.astype(F32)).astype(BF16)
    return _dot_nt(a_hi, b_hi) + _dot_nt(a_hi, b_lo) + _dot_nt(a_lo, b_hi)


def _prep_kernel(are_ref, aim_ref, ldt_ref, bre_ref, bim_ref, cre_ref, cim_ref,
                 pw_ref, k8_ref, w7_ref, gq_ref):
    a_re = are_ref[...]
    a_im = aim_ref[...]
    dt = jnp.exp(ldt_ref[...])
    mag = jnp.exp(a_re * dt)
    ab_re = mag * jnp.cos(a_im * dt)
    ab_im = mag * jnp.sin(a_im * dt)
    inv = 1.0 / (a_re * a_re + a_im * a_im)
    f_re = ((ab_re - 1.0) * a_re + ab_im * a_im) * inv
    f_im = (ab_im * a_re - (ab_re - 1.0) * a_im) * inv
    b_re = bre_ref[...]
    b_im = bim_ref[...]
    bb_re = f_re * b_re - f_im * b_im
    bb_im = f_re * b_im + f_im * b_re
    c_re = cre_ref[...]
    c_im = cim_ref[...]
    gc = SSM_GROUPS * SSM_CH
    same_group = (lax.broadcasted_iota(jnp.int32, (gc, gc), 0) // SSM_CH
                  == lax.broadcasted_iota(jnp.int32, (gc, gc), 1) // SSM_CH)
    spread = (lax.broadcasted_iota(jnp.int32, (SSM_P, QS), 1) % SSM_P
              == lax.broadcasted_iota(jnp.int32, (SSM_P, QS), 0)).astype(BF16)
    spread_t = (lax.broadcasted_iota(jnp.int32, (QS, SSM_P), 0) % SSM_P
                == lax.broadcasted_iota(jnp.int32, (QS, SSM_P), 1)).astype(BF16)
    mask_w = (lax.broadcasted_iota(jnp.int32, (LANES, QS), 0) // SSM_CH
              == lax.broadcasted_iota(jnp.int32, (LANES, QS), 1) // SSM_P)
    mask_g = (lax.broadcasted_iota(jnp.int32, (QS, LANES), 0) // SSM_P
              == lax.broadcasted_iota(jnp.int32, (QS, LANES), 1) // SSM_CH)
    p_re = jnp.ones_like(ab_re)
    p_im = jnp.zeros_like(ab_re)
    for i in range(BLK + 1):
        if i == 1:
            pw_ref[0, 0] = p_re
            pw_ref[0, 1] = p_im
        if i == BLK:
            pw_ref[1, 0] = p_re
            pw_ref[1, 1] = p_im
        if i >= 1:
            r = i - 1
            gcs = (c_re * p_re - c_im * p_im, -(c_re * p_im + c_im * p_re))
            for q in range(NQ):
                for part in range(2):
                    slab = gcs[part][q * LANES:(q + 1) * LANES].astype(BF16)
                    blk = jnp.where(mask_g, _dot_nt(spread_t, slab), 0.0)
                    gq_ref[q, part * QS:(part + 1) * QS, r * LANES:(r + 1) * LANES] = blk.astype(BF16)
        if i < BLK:
            r = BLK - 1 - i
            xs = (p_re * bb_re - p_im * bb_im, p_re * bb_im + p_im * bb_re)
            kk = jnp.where(same_group, _split3_dot_nt(xs[0], c_re) - _split3_dot_nt(xs[1], c_im), 0.0)
            for q in range(NQ):
                rows = slice(q * LANES, (q + 1) * LANES)
                k8_ref[q, :, i * LANES:(i + 1) * LANES] = kk[rows, rows].astype(BF16)
                for part in range(2):
                    blk = jnp.where(mask_w, _dot(xs[part][rows].astype(BF16), spread), 0.0)
                    w7_ref[q, r * LANES:(r + 1) * LANES, part * QS:(part + 1) * QS] = blk.astype(BF16)
        p_re, p_im = p_re * ab_re - p_im * ab_im, p_re * ab_im + p_im * ab_re


def _ssm_prep(a_re, a_im, log_dt, b_re, b_im, c_re, c_im):
    gc = SSM_GROUPS * SSM_CH
    rep = lambda a: jnp.repeat(a.astype(F32), SSM_CH, axis=0)
    b_rows = lambda b: jnp.transpose(b.astype(F32), (0, 2, 1)).reshape(gc, SSM_P)
    c_rows = lambda c: c.astype(F32).reshape(gc, SSM_P)
    return pl.pallas_call(
        _prep_kernel,
        out_shape=(jax.ShapeDtypeStruct((2, 2, gc, SSM_P), F32),
                   jax.ShapeDtypeStruct((NQ, LANES, BLK * LANES), BF16),
                   jax.ShapeDtypeStruct((NQ, BLK * LANES, 2 * QS), BF16),
                   jax.ShapeDtypeStruct((NQ, 2 * QS, BLK * LANES), BF16)),
        compiler_params=pltpu.CompilerParams(vmem_limit_bytes=VMEM_LIMIT),
        name="ssm_prep",
    )(rep(a_re), rep(a_im), rep(log_dt.reshape(SSM_GROUPS, 1)), b_rows(b_re), b_rows(b_im),
      c_rows(c_re), c_rows(c_im))


def _head_norm(t, g, seg):
    sq = (t * t).astype(BF16)
    ms = jnp.concatenate(
        [_dot(sq[:, s * GROUP_W:(s + 1) * GROUP_W], seg) for s in range(len(GROUPS))], axis=1)
    return t * lax.rsqrt(ms + RMS_EPS) * g


def _store_attn_rows(val, out_ref, perm_ref, slab0):
    tm = val.shape[0]
    out_ref[:, :GROUP_W] = val[:, :GROUP_W].astype(BF16)
    for g in range(1, len(GROUPS)):
        dil = GROUPS[g][1]
        n = tm // dil
        for half in range(GROUP_W // LANES):
            c0 = g * GROUP_W + half * LANES
            col = val[:, c0:c0 + LANES]
            if perm_ref is None:
                out_ref[:, c0:c0 + LANES] = col.astype(BF16)
                continue
            slab = slab0 + (g - 1) * (GROUP_W // LANES) + half
            perm_ref[slab] = col
            for r in range(dil):
                out_ref[r * n:(r + 1) * n, c0:c0 + LANES] = (
                    perm_ref[slab, pl.ds(r, n, stride=dil), :].astype(BF16))


def _ffn_in_kernel(x_ref, g1_ref, wg_ref, wu_ref, wd_ref, gm_ref, win_ref, gq_ref, gk_ref, seg_ref,
                   x1_ref, u_ref, q_ref, k_ref, v_ref, *rest, tiles_per_seq, keeps):
    perm_ref = rest[-1] if len(rest) > 1 else None
    x1 = _ffn(x_ref[...], g1_ref, wg_ref, wu_ref, wd_ref)
    x1_ref[...] = x1
    h = _rms(x1, gm_ref[...]).astype(BF16)
    proj = _dot(h, win_ref[...])
    u_ref[...] = proj[:, :SSM_W]
    seg = seg_ref[...]
    q = _head_norm(proj[:, SSM_W:SSM_W + ATTN_W], gq_ref[...], seg)
    k = _head_norm(proj[:, SSM_W + ATTN_W:SSM_W + 2 * ATTN_W], gk_ref[...], seg)
    v = proj[:, SSM_W + 2 * ATTN_W:]
    slabs = (len(GROUPS) - 1) * (GROUP_W // LANES)
    _store_attn_rows(q * HEAD_DIM ** -0.5, q_ref, perm_ref, 0)
    _store_attn_rows(k, k_ref, perm_ref, slabs)
    _store_attn_rows(v, v_ref, perm_ref, 2 * slabs)
    if perm_ref is None:
        rest[0][:, :ATTN_W] = k
        rest[0][:, ATTN_W:] = v
        return
    tm = x1.shape[0]
    j = pl.program_id(0) % tiles_per_seq
    for g, keep in enumerate(keeps):
        lanes = slice(g * GROUP_W, (g + 1) * GROUP_W)
        cols = min(keep, tm)

        @pl.when(j >= tiles_per_seq - pl.cdiv(keep, tm))
        def _(g=g, lanes=lanes, cols=cols):
            rest[g][0, :GROUP_W, :] = k[:, lanes].T[:, tm - cols:]
            rest[g][0, GROUP_W:, :] = v[:, lanes].T[:, tm - cols:]


def _ffn_in(x, w, tm, tiles_per_seq, prompt_form):
    rows = x.shape[0]
    n_tiles = rows // tm
    n_seq = n_tiles // tiles_per_seq
    row_spec = lambda width: pl.BlockSpec((tm, width), lambda i: (i, 0))
    if prompt_form:
        keeps = tuple(min(window, tiles_per_seq * tm) for window, _ in GROUPS)
        assert all(keep % tm == 0 or keep < tm for keep in keeps)

        def tail_spec(keep):
            first = tiles_per_seq - pl.cdiv(keep, tm)
            return pl.BlockSpec((1, 2 * GROUP_W, min(keep, tm)),
                                lambda i: (i // tiles_per_seq, 0, jnp.maximum(i % tiles_per_seq - first, 0)))

        tail_shapes = tuple(jax.ShapeDtypeStruct((n_seq, 2 * GROUP_W, keep), F32) for keep in keeps)
        tail_specs = tuple(tail_spec(keep) for keep in keeps)
        scratch = [pltpu.VMEM((3 * (len(GROUPS) - 1) * (GROUP_W // LANES), tm, LANES), F32)]
    else:
        assert tiles_per_seq == 1
        keeps = ()
        tail_shapes = (jax.ShapeDtypeStruct((rows, 2 * ATTN_W), F32),)
        tail_specs = (row_spec(2 * ATTN_W),)
        scratch = []
    out_shape = (jax.ShapeDtypeStruct((rows, D_MODEL), F32),
                 jax.ShapeDtypeStruct((rows, SSM_W), F32),
                 jax.ShapeDtypeStruct((rows, ATTN_W), BF16),
                 jax.ShapeDtypeStruct((rows, ATTN_W), BF16),
                 jax.ShapeDtypeStruct((rows, ATTN_W), BF16)) + tail_shapes
    outs = pl.pallas_call(
        functools.partial(_ffn_in_kernel, tiles_per_seq=tiles_per_seq, keeps=keeps),
        grid=(n_tiles,),
        in_specs=[row_spec(D_MODEL), _const_spec((1, D_MODEL)),
                  _const_spec((D_MODEL, D_FF)), _const_spec((D_MODEL, D_FF)), _const_spec((D_FF, D_MODEL)),
                  _const_spec((1, D_MODEL)), _const_spec((D_MODEL, IN_A)),
                  _const_spec((1, ATTN_W)), _const_spec((1, ATTN_W)), _const_spec((GROUP_W, GROUP_W))],
        out_specs=(row_spec(D_MODEL), row_spec(SSM_W), row_spec(ATTN_W), row_spec(ATTN_W), row_spec(ATTN_W))
                  + tail_specs,
        out_shape=out_shape,
        scratch_shapes=scratch,
        compiler_params=_params(1),
        name="ffn_in",
    )(x, w["g_ffn1"], w["w1_gate"], w["w1_up"], w["w1_down"], w["g_mix"], w["w_in_a"],
      w["g_q"], w["g_k"], w["seg_mean"])
    return outs[:5], (list(outs[5:]) if prompt_form else outs[5])


def _glu_out(y, u, dsk_ref, wglu_ref, bglu_ref):
    y = jax.nn.gelu(y + dsk_ref[...] * u)
    z = _dot(y.astype(BF16), wglu_ref[...]) + bglu_ref[...]
    return (y * _sigmoid(z)).astype(BF16)


def _s5_kernel(u0_ref, u1_ref, u2_ref, u3_ref, a8re_ref, a8im_ref, k8_ref, w7_ref, gq_ref,
               dsk_ref, wglu_ref, bglu_ref, y_ref, hre_ref, him_ref,
               ls_ref, yq_ref, y2s_ref, sre_ref, sim_ref):
    u_refs = (u0_ref, u1_ref, u2_ref, u3_ref)
    nb, t_blk, _ = u0_ref.shape
    nblk = t_blk // BLK
    qw = 2 * QS

    @pl.when(pl.program_id(0) == 0)
    def _():
        sre_ref[...] = jnp.zeros_like(sre_ref)
        sim_ref[...] = jnp.zeros_like(sim_ref)

    sub = lax.broadcasted_iota(jnp.int32, (nblk, BLK, LANES), 1)
    for q in range(NQ):
        ublks = []
        for b in range(nb):
            uq = u_refs[q][b]
            z3 = _dot(uq.astype(BF16), k8_ref[q]).reshape(nblk, BLK, BLK * LANES)
            acc = z3[:, :, :LANES]
            for i in range(1, BLK):
                zi = z3[:, :, LANES * i:LANES * (i + 1)]
                acc = acc + jnp.where(sub >= i, pltpu.roll(zi, i, axis=1), 0.0)
            yq_ref[b * NQ + q] = acc.reshape(t_blk, LANES)
            ublks.append(jnp.concatenate(
                [u_refs[q][b, pl.ds(r, nblk, stride=BLK), :] for r in range(BLK)], axis=1))
        ls = _dot(jnp.concatenate(ublks, axis=0).astype(BF16), w7_ref[q])
        for b in range(nb):
            ls_ref[b, :, q * qw:(q + 1) * qw] = ls[b * nblk:(b + 1) * nblk]

    for q in range(NQ):
        ch = slice(q * QS, (q + 1) * QS)
        c_re = q * qw
        c_im = c_re + QS
        a_re = a8re_ref[:, ch]
        a_im = a8im_ref[:, ch]

        def body(k, carry, c_re=c_re, c_im=c_im, a_re=a_re, a_im=a_im):
            row = pl.ds(k, 1)
            out = []
            for b in range(nb):
                s_re, s_im = carry[2 * b], carry[2 * b + 1]
                l_re = ls_ref[b, row, c_re:c_re + QS]
                l_im = ls_ref[b, row, c_im:c_im + QS]
                ls_ref[b, row, c_re:c_re + QS] = s_re
                ls_ref[b, row, c_im:c_im + QS] = s_im
                out.append(a_re * s_re - a_im * s_im + l_re)
                out.append(a_re * s_im + a_im * s_re + l_im)
            return tuple(out)

        init = []
        for b in range(nb):
            init += [sre_ref[b:b + 1, ch], sim_ref[b:b + 1, ch]]
        fin = lax.fori_loop(0, nblk, body, tuple(init), unroll=True)
        for b in range(nb):
            sre_ref[b:b + 1, ch] = fin[2 * b]
            sim_ref[b:b + 1, ch] = fin[2 * b + 1]

    for q in range(NQ):
        sc = jnp.concatenate([ls_ref[b, :, q * qw:(q + 1) * qw] for b in range(nb)], axis=0)
        y2 = _dot(sc.astype(BF16), gq_ref[q])
        for b in range(nb):
            for r in range(BLK):
                y2s_ref[pl.ds(r, nblk, stride=BLK), :] = y2[b * nblk:(b + 1) * nblk, LANES * r:LANES * (r + 1)]
            yq_ref[b * NQ + q] = yq_ref[b * NQ + q] + y2s_ref[...]

    for b in range(nb):
        y = jnp.concatenate([yq_ref[b * NQ + q] for q in range(NQ)], axis=1)
        u = jnp.concatenate([u_refs[q][b] for q in range(NQ)], axis=1)
        y_ref[b] = _glu_out(y, u, dsk_ref, wglu_ref, bglu_ref)
    hre_ref[...] = sre_ref[...]
    him_ref[...] = sim_ref[...]


def _s5(u, w, t_blk):
    nb, seq, _ = u.shape
    nblk = t_blk // BLK
    u_specs = [pl.BlockSpec((nb, t_blk, LANES), lambda c, q=q: (0, c, q)) for q in range(NQ)]
    state = pl.BlockSpec((nb, N_STATE), lambda c: (0, 0))
    return pl.pallas_call(
        _s5_kernel,
        grid=(seq // t_blk,),
        in_specs=u_specs + [_const_spec((1, N_STATE)), _const_spec((1, N_STATE)),
                            _const_spec((NQ, LANES, BLK * LANES)),
                            _const_spec((NQ, BLK * LANES, 2 * QS)), _const_spec((NQ, 2 * QS, BLK * LANES)),
                            _const_spec((1, SSM_W)), _const_spec((SSM_W, SSM_W)), _const_spec((1, SSM_W))],
        out_specs=(pl.BlockSpec((nb, t_blk, SSM_W), lambda c: (0, c, 0)), state, state),
        out_shape=(jax.ShapeDtypeStruct((nb, seq, SSM_W), BF16),
                   jax.ShapeDtypeStruct((nb, N_STATE), F32), jax.ShapeDtypeStruct((nb, N_STATE), F32)),
        scratch_shapes=[pltpu.VMEM((nb, nblk, NQ * 2 * QS), F32),
                        pltpu.VMEM((nb * NQ, t_blk, LANES), F32),
                        pltpu.VMEM((t_blk, LANES), F32),
                        pltpu.VMEM((nb, N_STATE), F32), pltpu.VMEM((nb, N_STATE), F32)],
        compiler_params=_params(1),
        name="s5",
    )(u, u, u, u, w["a8_re"], w["a8_im"], w["k8"], w["w7"], w["gq"], w["ssm_d"], w["w_glu"], w["b_glu"])


def _s5_step_kernel(u_ref, hre_ref, him_ref, are_ref, aim_ref, k8_ref, w7_ref, gq_ref,
                    dsk_ref, wglu_ref, bglu_ref, y_ref, nre_ref, nim_ref):
    u = u_ref[...]
    ys = []
    for q in range(NQ):
        ch = slice(q * QS, (q + 1) * QS)
        uq = u[:, q * LANES:(q + 1) * LANES].astype(BF16)
        x = _dot(uq, w7_ref[q, (BLK - 1) * LANES:, :])
        a_re, a_im = are_ref[:, ch], aim_ref[:, ch]
        s_re, s_im = hre_ref[:, ch], him_ref[:, ch]
        nre_ref[:, ch] = a_re * s_re - a_im * s_im + x[:, :QS]
        nim_ref[:, ch] = a_re * s_im + a_im * s_re + x[:, QS:]
        s = jnp.concatenate([s_re, s_im], axis=1).astype(BF16)
        ys.append(_dot(s, gq_ref[q, :, :LANES]) + _dot(uq, k8_ref[q, :, :LANES]))
    y_ref[...] = _glu_out(jnp.concatenate(ys, axis=1), u, dsk_ref, wglu_ref, bglu_ref)


def _s5_step(u, h_re, h_im, w):
    rows = u.shape[0]
    return pl.pallas_call(
        _s5_step_kernel,
        out_shape=(jax.ShapeDtypeStruct((rows, SSM_W), BF16),
                   jax.ShapeDtypeStruct((rows, N_STATE), F32), jax.ShapeDtypeStruct((rows, N_STATE), F32)),
        compiler_params=pltpu.CompilerParams(vmem_limit_bytes=VMEM_LIMIT),
        name="s5_step",
    )(u, h_re, h_im, w["ab_re"], w["ab_im"], w["k8"], w["w7"], w["gq"], w["ssm_d"], w["w_glu"], w["b_glu"])


def _step_attention(q, kn, vn, cache, dil):
    n_feat, window = cache.shape
    row8 = lax.broadcasted_iota(jnp.int32, (8, GROUP_W), 0)
    head8 = lax.broadcasted_iota(jnp.int32, (8, GROUP_W), 1) // HEAD_DIM == row8
    diag8 = (lax.broadcasted_iota(jnp.int32, (8, LANES), 0)
             == lax.broadcasted_iota(jnp.int32, (8, LANES), 1))
    q8 = jnp.where(head8, jnp.broadcast_to(q, (8, GROUP_W)), 0.0)
    pos = lax.broadcasted_iota(jnp.int32, (8, window), 1)
    s = _dot(q8.astype(BF16), cache[:GROUP_W].astype(BF16))
    s = jnp.where(jnp.bitwise_and(pos, dil - 1) == 0, s, NEG)
    s_new = jnp.sum(q8 * kn, axis=-1, keepdims=True)
    m = jnp.maximum(jnp.max(s, axis=-1, keepdims=True), s_new)
    p = jnp.exp(s - m)
    p_new = jnp.exp(s_new - m)
    den = jnp.sum(p, axis=-1, keepdims=True) + p_new
    o8 = (_dot_nt(p.astype(BF16), cache[GROUP_W:].astype(BF16)) + p_new * vn) / den
    o = jnp.sum(jnp.where(head8, o8, 0.0), axis=0, keepdims=True)
    lse = jnp.sum(jnp.where(diag8, m + jnp.log(den), 0.0), axis=0, keepdims=True)
    eye = (lax.broadcasted_iota(jnp.int32, (n_feat, n_feat), 0)
           == lax.broadcasted_iota(jnp.int32, (n_feat, n_feat), 1))
    new_col = jnp.sum(jnp.where(eye, jnp.concatenate([kn, vn], axis=1), 0.0), axis=-1, keepdims=True)
    rolled = pltpu.roll(cache, window - 1, axis=1)
    last = lax.broadcasted_iota(jnp.int32, (n_feat, window), 1) == window - 1
    return o, lse, jnp.where(last, new_col, rolled)


def _swa_kernel(q_ref, k_ref, v_ref, kp_ref, vp_ref, qs_ref, kvs_ref, c_ref,
                o_ref, l_ref, os_ref, ls_ref, n_ref, *, qb, chunk, dil, n_sample):
    first = pl.program_id(2) == 0

    def rows(ref):
        return jnp.concatenate([ref[t] for t in range(ref.shape[0])], axis=0)

    q = rows(q_ref)
    kk = jnp.concatenate([rows(kp_ref), rows(k_ref)], axis=0)
    vv = jnp.concatenate([rows(vp_ref), rows(v_ref)], axis=0)
    nk = 2 * KEYS_PER_QUERY
    row = lax.broadcasted_iota(jnp.int32, (128, nk), 0)
    col = lax.broadcasted_iota(jnp.int32, (128, nk), 1)
    band = jnp.where(col >= row, jnp.where(col <= row + KEYS_PER_QUERY, 0.0, NEG), NEG)
    band_first = band + jnp.where(col < KEYS_PER_QUERY, jnp.where(first, NEG, 0.0), 0.0)
    lane = lax.broadcasted_iota(jnp.int32, (1, GROUP_W), 1) // HEAD_DIM
    lane_l = lax.broadcasted_iota(jnp.int32, (1, LANES), 1)
    for j in range(qb // 128):
        qj = q[128 * j:128 * (j + 1)]
        kj = kk[128 * j:128 * j + nk]
        vj = vv[128 * j:128 * j + nk]
        bias = band_first if j == 0 else band
        qs = jnp.concatenate([jnp.where(lane == h, qj, jnp.zeros_like(qj)) for h in range(HEADS)], axis=0)
        s = (_dot_nt(qs, kj).reshape(HEADS, 128, nk) + bias[None]).reshape(HEADS * 128, nk)
        m = jnp.max(s, axis=-1, keepdims=True)
        p = jnp.exp(s - m)
        den = jnp.sum(p, axis=-1, keepdims=True)
        on = _dot(p.astype(BF16), vj) * (1.0 / den)
        lse = m + jnp.log(den)
        o_acc = jnp.zeros((128, GROUP_W), F32)
        l_acc = jnp.zeros((128, LANES), F32)
        for h in range(HEADS):
            rows_h = slice(128 * h, 128 * (h + 1))
            o_acc = jnp.where(lane == h, on[rows_h], o_acc)
            l_acc = jnp.where(lane_l == h, lse[rows_h], l_acc)
        o_out = o_acc.astype(BF16)
        if chunk >= 128:
            t, r0 = divmod(128 * j, chunk)
            o_ref[t, r0:r0 + 128, :] = o_out
            l_ref[t, r0:r0 + 128, :] = l_acc
        else:
            for s_ in range(128 // chunk):
                t = (128 * j) // chunk + s_
                o_ref[t] = o_out[s_ * chunk:(s_ + 1) * chunk]
                l_ref[t] = l_acc[s_ * chunk:(s_ + 1) * chunk]

    step = (pl.program_id(0) * pl.num_programs(1) + pl.program_id(1)) * pl.num_programs(2) + pl.program_id(2)

    @pl.when(step < n_sample)
    def _():
        kv = kvs_ref[0]
        o, lse, new_cache = _step_attention(qs_ref[0].astype(F32), kv[:, :GROUP_W], kv[:, GROUP_W:],
                                            c_ref[0], dil)
        os_ref[0] = o.astype(BF16)
        ls_ref[0] = lse
        n_ref[0] = new_cache


def _swa(q, k, v, g, tile, q_s, kv_s, cache):
    window, dil = GROUPS[g]
    nb, seq, _ = q.shape
    n_sample = q_s.shape[0]
    n_tiles = seq // tile
    chunk = tile // dil
    n = seq // dil
    qb = min(512, n)
    tpb = qb // chunk
    n_i = n // qb
    assert nb * dil * n_i >= n_sample and cache.shape == (n_sample, 2 * GROUP_W, window)
    view = lambda a: a.reshape(nb, n_tiles, dil, chunk, a.shape[-1])
    cur = pl.BlockSpec((None, tpb, None, chunk, GROUP_W), lambda b, r, i: (b, i, r, 0, g))
    if chunk >= 128:
        prev = pl.BlockSpec((None, 1, None, 128, GROUP_W),
                            lambda b, r, i: (b, jnp.maximum(i * tpb - 1, 0), r, chunk // 128 - 1, g))
    else:
        ptiles = 128 // chunk
        prev = pl.BlockSpec((None, ptiles, None, chunk, GROUP_W),
                            lambda b, r, i: (b, jnp.maximum(i * (tpb // ptiles) - 1, 0), r, 0, g))
    out_block = lambda width: pl.BlockSpec((None, tpb, None, chunk, width), lambda b, r, i: (b, i, r, 0, 0))
    sample_block = lambda *dims: pl.BlockSpec(
        (1,) + dims, lambda b, r, i: (jnp.minimum((b * dil + r) * n_i + i, n_sample - 1),) + (0,) * len(dims))
    o, lse, o_s, l_s, new_cache = pl.pallas_call(
        functools.partial(_swa_kernel, qb=qb, chunk=chunk, dil=dil, n_sample=n_sample),
        grid=(nb, dil, n_i),
        in_specs=[cur, cur, cur, prev, prev,
                  sample_block(1, GROUP_W), sample_block(1, 2 * GROUP_W), sample_block(2 * GROUP_W, window)],
        out_specs=(out_block(GROUP_W), out_block(LANES),
                   sample_block(1, GROUP_W), sample_block(1, LANES), sample_block(2 * GROUP_W, window)),
        out_shape=(jax.ShapeDtypeStruct((nb, n_tiles, dil, chunk, GROUP_W), BF16),
                   jax.ShapeDtypeStruct((nb, n_tiles, dil, chunk, LANES), F32),
                   jax.ShapeDtypeStruct((n_sample, 1, GROUP_W), BF16),
                   jax.ShapeDtypeStruct((n_sample, 1, LANES), F32),
                   jax.ShapeDtypeStruct(cache.shape, F32)),
        compiler_params=_params(3),
        name="swa_g%d" % g,
    )(view(q), view(k), view(v), view(k), view(v),
      q_s.reshape(n_sample, 1, GROUP_W), kv_s.reshape(n_sample, 1, 2 * GROUP_W), cache)
    return (o.reshape(nb * seq, GROUP_W), lse.reshape(nb * seq, LANES),
            o_s.reshape(n_sample, GROUP_W), l_s.reshape(n_sample, LANES), new_cache)


def _position_order(o_ref, l_ref, dil, nat_ref, slab0):
    tm = o_ref.shape[0]
    n = tm // dil
    for r in range(dil):
        rows = slice(r * n, (r + 1) * n)
        dst = pl.ds(r, n, stride=dil)
        nat_ref[slab0, dst, :] = o_ref[rows, :LANES].astype(F32)
        nat_ref[slab0 + 1, dst, :] = o_ref[rows, LANES:].astype(F32)
        nat_ref[slab0 + 2, dst, :] = l_ref[rows, :]
    o = jnp.concatenate([nat_ref[slab0], nat_ref[slab0 + 1]], axis=1)
    return o, nat_ref[slab0 + 2]


def _mix_ffn_kernel(x1_ref, ys_ref, o0_ref, o1_ref, o2_ref, l0_ref, l1_ref, l2_ref,
                    gm_ref, wgt_ref, wsp_ref, wap_ref, wo_ref, seget_ref,
                    g2_ref, wg_ref, wu_ref, wd_ref, y_ref, *scratch):
    x1 = x1_ref[...]
    h = _rms(x1, gm_ref[...]).astype(BF16)
    gates = _sigmoid(_dot(h, wgt_ref[...]))
    os_ = [o0_ref[...].astype(F32)]
    ls = [l0_ref[...]]
    for g, (o_ref, l_ref) in enumerate(((o1_ref, l1_ref), (o2_ref, l2_ref)), start=1):
        if scratch:
            o, l = _position_order(o_ref, l_ref, GROUPS[g][1], scratch[0], 3 * (g - 1))
        else:
            o, l = o_ref[...].astype(F32), l_ref[...]
        os_.append(o)
        ls.append(l)
    l_top = jnp.maximum(jnp.maximum(ls[0], ls[1]), ls[2])
    es = [jnp.exp(l - l_top) for l in ls]
    inv = 1.0 / (es[0] + es[1] + es[2])
    seget = seget_ref[...]
    y_attn = None
    for e, o in zip(es, os_):
        t = _split_dot(e * inv, seget) * o
        y_attn = t if y_attn is None else y_attn + t
    mixed = (gates[:, :D_MODEL] * _dot(ys_ref[...], wsp_ref[...])
             + gates[:, D_MODEL:] * _dot(y_attn.astype(BF16), wap_ref[...]))
    x2 = x1 + _dot(mixed.astype(BF16), wo_ref[...])
    y_ref[...] = _ffn(x2, g2_ref, wg_ref, wu_ref, wd_ref)


def _mix_ffn(x1, y_ssm, os_, ls_, w, tm, residue_major):
    rows = x1.shape[0]
    row_spec = lambda width: pl.BlockSpec((tm, width), lambda i: (i, 0))
    n_slabs = 3 * (len(GROUPS) - 1)
    scratch = [pltpu.VMEM((n_slabs, tm, LANES), F32)] if residue_major else []
    return pl.pallas_call(
        _mix_ffn_kernel,
        grid=(rows // tm,),
        in_specs=[row_spec(D_MODEL), row_spec(SSM_W)] + [row_spec(GROUP_W)] * 3 + [row_spec(LANES)] * 3
                 + [_const_spec((1, D_MODEL)), _const_spec((D_MODEL, 2 * D_MODEL)),
                    _const_spec((SSM_W, D_MODEL)), _const_spec((GROUP_W, D_MODEL)),
                    _const_spec((D_MODEL, D_MODEL)), _const_spec((LANES, GROUP_W)),
                    _const_spec((1, D_MODEL)), _const_spec((D_MODEL, D_FF)), _const_spec((D_MODEL, D_FF)),
                    _const_spec((D_FF, D_MODEL))],
        out_specs=row_spec(D_MODEL),
        out_shape=jax.ShapeDtypeStruct((rows, D_MODEL), F32),
        scratch_shapes=scratch,
        compiler_params=_params(1),
        name="mix_ffn",
    )(x1, y_ssm, *os_, *ls_, w["g_mix"], w["w_gates"], w["w_ssm_proj"], w["w_attn_proj"], w["w_o"],
      w["seg_et"], w["g_ffn2"], w["w2_gate"], w["w2_up"], w["w2_down"])


def _prepare_weights(g_ffn1, w1_gate, w1_up, w1_down, g_mix, w_in, g_q, g_k,
                     ssm_a_re, ssm_a_im, ssm_log_dt, ssm_b_re, ssm_b_im, ssm_c_re, ssm_c_im,
                     ssm_d, w_glu, b_glu, w_ssm_proj, w_attn_proj, w_o, g_ffn2, w2_gate, w2_up, w2_down):
    w = {}
    row = lambda a: a.reshape(1, -1).astype(F32)
    w["g_ffn1"], w["g_mix"], w["g_ffn2"] = row(g_ffn1), row(g_mix), row(g_ffn2)
    for name, a in (("w1_gate", w1_gate), ("w1_up", w1_up), ("w1_down", w1_down),
                    ("w2_gate", w2_gate), ("w2_up", w2_up), ("w2_down", w2_down),
                    ("w_glu", w_glu), ("w_ssm_proj", w_ssm_proj), ("w_attn_proj", w_attn_proj),
                    ("w_o", w_o)):
        w[name] = a.astype(BF16)
    w["w_in_a"] = w_in[:, :IN_A].astype(BF16)
    w["w_gates"] = w_in[:, IN_A:].astype(BF16)
    per_head = lambda g: jnp.broadcast_to(g[:, None, :], (len(GROUPS), HEADS, HEAD_DIM)).reshape(1, ATTN_W)
    w["g_q"], w["g_k"] = per_head(g_q.astype(F32)), per_head(g_k.astype(F32))
    head_of_lane = jnp.arange(GROUP_W) // HEAD_DIM
    w["seg_mean"] = ((head_of_lane[:, None] == head_of_lane[None, :]) / HEAD_DIM).astype(BF16)
    w["seg_et"] = (jnp.arange(LANES)[:, None] == head_of_lane[None, :]).astype(BF16)
    w["ssm_d"], w["b_glu"] = row(ssm_d), row(b_glu)

    pw, w["k8"], w["w7"], w["gq"] = _ssm_prep(ssm_a_re, ssm_a_im, ssm_log_dt, ssm_b_re, ssm_b_im,
                                               ssm_c_re, ssm_c_im)
    per_group = lambda a: a.reshape(SSM_GROUPS, SSM_CH, SSM_P)[:, 0].reshape(1, N_STATE)
    w["ab_re"], w["ab_im"] = per_group(pw[0, 0]), per_group(pw[0, 1])
    w["a8_re"], w["a8_im"] = per_group(pw[1, 0]), per_group(pw[1, 1])
    return w


def _forward(x_p, x_s, state_s, caches_s, w):
    nb, seq, _ = x_p.shape
    n_s = x_s.shape[0]
    assert seq % TILE == 0 and x_s.shape[1] == 1
    (x1, u, q, k, v), caches_p = _ffn_in(x_p.reshape(nb * seq, D_MODEL), w, TILE, seq // TILE, True)
    (x1_s, u_s, q_s, _, _), kv_s = _ffn_in(x_s.reshape(n_s, D_MODEL), w, n_s, 1, False)
    y_ssm, *h_p = _s5(u.reshape(nb, seq, SSM_W), w, min(S5_T, seq))
    y_ssm_s, *h_s = _s5_step(u_s, state_s[0], state_s[1], w)
    q3, k3, v3 = (a.reshape(nb, seq, ATTN_W) for a in (q, k, v))
    os_p, ls_p, os_s, ls_s, new_caches_s = [], [], [], [], []
    for g in range(len(GROUPS)):
        lanes = slice(g * GROUP_W, (g + 1) * GROUP_W)
        kv_g = jnp.concatenate([kv_s[:, lanes], kv_s[:, ATTN_W + g * GROUP_W:ATTN_W + (g + 1) * GROUP_W]], axis=1)
        o, l, o_s, l_s, c_new = _swa(q3, k3, v3, g, TILE, q_s[:, lanes], kv_g, caches_s[g])
        for lst, a in zip((os_p, ls_p, os_s, ls_s, new_caches_s), (o, l, o_s, l_s, c_new)):
            lst.append(a)
    y_p = _mix_ffn(x1, y_ssm.reshape(nb * seq, SSM_W), os_p, ls_p, w, TILE, True)
    y_s = _mix_ffn(x1_s, y_ssm_s, os_s, ls_s, w, n_s, False)
    return (y_p.reshape(nb, seq, D_MODEL), y_s.reshape(n_s, 1, D_MODEL), h_p, h_s, caches_p, new_caches_s)


def _to_features_major(c):
    nb, win = c.shape[:2]
    return jnp.transpose(c, (0, 2, 3, 4, 1)).reshape(nb, 2 * GROUP_W, win)


def _to_window_buffer(c):
    nb, _, win = c.shape
    return jnp.transpose(c.reshape(nb, 2, HEADS, HEAD_DIM, win), (0, 4, 1, 2, 3))[None]


def kernel(x_prompt, x_sample, cache_kv_w128, cache_kv_w512, cache_kv_w2048, state_ssm_re, state_ssm_im, g_ffn1, w1_gate, w1_up, w1_down, g_mix, w_in, g_q, g_k, ssm_a_re, ssm_a_im, ssm_log_dt, ssm_b_re, ssm_b_im, ssm_c_re, ssm_c_im, ssm_d, w_glu, b_glu, w_ssm_proj, w_attn_proj, w_o, g_ffn2, w2_gate, w2_up, w2_down):
    layer_weights = (g_ffn1, w1_gate, w1_up, w1_down, g_mix, w_in, g_q, g_k,
                     ssm_a_re, ssm_a_im, ssm_log_dt, ssm_b_re, ssm_b_im, ssm_c_re, ssm_c_im,
                     ssm_d, w_glu, b_glu, w_ssm_proj, w_attn_proj, w_o, g_ffn2, w2_gate, w2_up, w2_down)
    depth = g_ffn1.shape[0]
    assert depth == 1, "window caches of deeper layers would need the previous layer's outputs"
    w = _prepare_weights(*(a[0] for a in layer_weights))
    nb_s = x_sample.shape[0]
    sdt = state_ssm_re.dtype
    as_state = lambda h: h.reshape(1, -1, SSM_GROUPS, SSM_P).astype(sdt)

    caches = [_to_features_major(c[0]) for c in (cache_kv_w128, cache_kv_w512, cache_kv_w2048)]
    state_s = (state_ssm_re[0].reshape(nb_s, N_STATE).astype(F32),
               state_ssm_im[0].reshape(nb_s, N_STATE).astype(F32))
    y_p, y_s, h_p, h_s, kv_p, kv_s = _forward(x_prompt, x_sample, state_s, caches, w)
    return ((y_p, y_s) + tuple(_to_window_buffer(c) for c in kv_p) + (as_state(h_p[0]), as_state(h_p[1]))
            + tuple(_to_window_buffer(c) for c in kv_s) + (as_state(h_s[0]), as_state(h_s[1])))
```

```python
import functools

import jax
import jax.numpy as jnp
from jax import lax
from jax.experimental import pallas as pl
from jax.experimental.pallas import tpu as pltpu

F32 = jnp.float32
BF16 = jnp.bfloat16

D_MODEL = 1024
D_FF = 2816
HEAD_DIM = 64
HEADS = 4
GROUPS = ((128, 1), (512, 4), (2048, 16))
KEYS_PER_QUERY = 128
GROUP_W = HEADS * HEAD_DIM
ATTN_W = len(GROUPS) * GROUP_W
SSM_W = 512
SSM_GROUPS = 32
SSM_CH = 16
SSM_P = 64
N_STATE = SSM_GROUPS * SSM_P
IN_A = SSM_W + 3 * ATTN_W
RMS_EPS = 1e-6
NEG = -1e30
LANES = 128

TILE = 512
FF_CHUNKS = ((0, 1024), (1024, 1024), (2048, 768))
BLK = 8
NQ = SSM_W // LANES
QS = N_STATE // NQ
S5_T = 1024
OWN_REGION_WINDOW = 2048
VMEM_LIMIT = 56 * 1024 * 1024


def _const_spec(shape):
    nd = len(shape)
    return pl.BlockSpec(shape, lambda *_: (0,) * nd, pipeline_mode=pl.Buffered(1))


def _params(n_grid):
    return pltpu.CompilerParams(dimension_semantics=("arbitrary",) * n_grid,
                                vmem_limit_bytes=VMEM_LIMIT)


def _rms(x, g):
    ms = jnp.mean(x * x, axis=-1, keepdims=True)
    return x * lax.rsqrt(ms + RMS_EPS) * g


def _dot(a, b):
    return jnp.dot(a, b, preferred_element_type=F32)


def _dot_nt(a, b):
    return lax.dot_general(a, b, (((1,), (1,)), ((), ())), preferred_element_type=F32)


def _split_dot(x, m):
    hi = x.astype(BF16)
    lo = (x - hi.astype(F32)).astype(BF16)
    return _dot(hi, m) + _dot(lo, m)


def _sigmoid(x):
    return 0.5 * jnp.tanh(0.5 * x) + 0.5


def _ffn(x, g_ref, wg_ref, wu_ref, wd_ref):
    xn = _rms(x, g_ref[...]).astype(BF16)
    acc = None
    for f0, fw in FF_CHUNKS:
        hg = _dot(xn, wg_ref[:, f0:f0 + fw])
        hu = _dot(xn, wu_ref[:, f0:f0 + fw])
        a = (hg * _sigmoid(hg) * hu).astype(BF16)
        d = _dot(a, wd_ref[f0:f0 + fw, :])
        acc = d if acc is None else acc + d
    return x + 0.5 * acc


def _split3_dot_nt(a, b):
    a_hi = a.astype(BF16)
    a_lo = (a - a_hi.astype(F32)).astype(BF16)
    b_hi = b.astype(BF16)
    b_lo = (b - b_hi.astype(F32)).astype(BF16)
    return _dot_nt(a_hi, b_hi) + _dot_nt(a_hi, b_lo) + _dot_nt(a_lo, b_hi)


def _prep_kernel(are_ref, aim_ref, ldt_ref, bre_ref, bim_ref, cre_ref, cim_ref,
                 pw_ref, k8_ref, w7_ref, gq_ref):
    a_re = are_ref[...]
    a_im = aim_ref[...]
    dt = jnp.exp(ldt_ref[...])
    mag = jnp.exp(a_re * dt)
    ab_re = mag * jnp.cos(a_im * dt)
    ab_im = mag * jnp.sin(a_im * dt)
    inv = 1.0 / (a_re * a_re + a_im * a_im)
    f_re = ((ab_re - 1.0) * a_re + ab_im * a_im) * inv
    f_im = (ab_im * a_re - (ab_re - 1.0) * a_im) * inv
    b_re = bre_ref[...]
    b_im = bim_ref[...]
    bb_re = f_re * b_re - f_im * b_im
    bb_im = f_re * b_im + f_im * b_re
    c_re = cre_ref[...]
    c_im = cim_ref[...]
    same_group = (lax.broadcasted_iota(jnp.int32, (LANES, LANES), 0) // SSM_CH
                  == lax.broadcasted_iota(jnp.int32, (LANES, LANES), 1) // SSM_CH)
    spread = (lax.broadcasted_iota(jnp.int32, (SSM_P, QS), 1) % SSM_P
              == lax.broadcasted_iota(jnp.int32, (SSM_P, QS), 0)).astype(BF16)
    spread_t = (lax.broadcasted_iota(jnp.int32, (QS, SSM_P), 0) % SSM_P
                == lax.broadcasted_iota(jnp.int32, (QS, SSM_P), 1)).astype(BF16)
    mask_w = (lax.broadcasted_iota(jnp.int32, (LANES, QS), 0) // SSM_CH
              == lax.broadcasted_iota(jnp.int32, (LANES, QS), 1) // SSM_P)
    mask_g = (lax.broadcasted_iota(jnp.int32, (QS, LANES), 0) // SSM_P
              == lax.broadcasted_iota(jnp.int32, (QS, LANES), 1) // SSM_CH)
    p_re = jnp.ones_like(ab_re)
    p_im = jnp.zeros_like(ab_re)
    for i in range(BLK + 1):
        if i == 1:
            pw_ref[0, 0] = p_re
            pw_ref[0, 1] = p_im
        if i == BLK:
            pw_ref[1, 0] = p_re
            pw_ref[1, 1] = p_im
        if i >= 1:
            r = i - 1
            gcs = (c_re * p_re - c_im * p_im, -(c_re * p_im + c_im * p_re))
            for part in range(2):
                blk = jnp.where(mask_g, _dot_nt(spread_t, gcs[part].astype(BF16)), 0.0)
                gq_ref[0, part * QS:(part + 1) * QS, r * LANES:(r + 1) * LANES] = blk.astype(BF16)
        if i < BLK:
            r = BLK - 1 - i
            xs = (p_re * bb_re - p_im * bb_im, p_re * bb_im + p_im * bb_re)
            kk = jnp.where(same_group, _split3_dot_nt(xs[0], c_re) - _split3_dot_nt(xs[1], c_im), 0.0)
            k8_ref[0, :, i * LANES:(i + 1) * LANES] = kk.astype(BF16)
            for part in range(2):
                blk = jnp.where(mask_w, _dot(xs[part].astype(BF16), spread), 0.0)
                w7_ref[0, r * LANES:(r + 1) * LANES, part * QS:(part + 1) * QS] = blk.astype(BF16)
        p_re, p_im = p_re * ab_re - p_im * ab_im, p_re * ab_im + p_im * ab_re


def _ssm_prep(a_re, a_im, log_dt, b_re, b_im, c_re, c_im):
    gc = SSM_GROUPS * SSM_CH
    rep = lambda a: jnp.repeat(a.astype(F32), SSM_CH, axis=0)
    b_rows = lambda b: jnp.transpose(b.astype(F32), (0, 2, 1)).reshape(gc, SSM_P)
    c_rows = lambda c: c.astype(F32).reshape(gc, SSM_P)
    rows = lambda width: pl.BlockSpec((LANES, width), lambda q: (q, 0))
    per_q = lambda d1, d2: pl.BlockSpec((1, d1, d2), lambda q: (q, 0, 0))
    return pl.pallas_call(
        _prep_kernel,
        grid=(NQ,),
        in_specs=[rows(SSM_P), rows(SSM_P), rows(1)] + [rows(SSM_P)] * 4,
        out_specs=(pl.BlockSpec((2, 2, LANES, SSM_P), lambda q: (0, 0, q, 0)),
                   per_q(LANES, BLK * LANES), per_q(BLK * LANES, 2 * QS), per_q(2 * QS, BLK * LANES)),
        out_shape=(jax.ShapeDtypeStruct((2, 2, gc, SSM_P), F32),
                   jax.ShapeDtypeStruct((NQ, LANES, BLK * LANES), BF16),
                   jax.ShapeDtypeStruct((NQ, BLK * LANES, 2 * QS), BF16),
                   jax.ShapeDtypeStruct((NQ, 2 * QS, BLK * LANES), BF16)),
        compiler_params=_params(1),
        name="ssm_prep",
    )(rep(a_re), rep(a_im), rep(log_dt.reshape(SSM_GROUPS, 1)), b_rows(b_re), b_rows(b_im),
      c_rows(c_re), c_rows(c_im))


def _head_norm(t, g, seg):
    sq = (t * t).astype(BF16)
    ms = jnp.concatenate(
        [_dot(sq[:, s * GROUP_W:(s + 1) * GROUP_W], seg) for s in range(len(GROUPS))], axis=1)
    return t * lax.rsqrt(ms + RMS_EPS) * g


def _store_attn_rows(val, out_ref, perm_ref, slab0):
    tm = val.shape[0]
    out_ref[:, :GROUP_W] = val[:, :GROUP_W].astype(BF16)
    for g in range(1, len(GROUPS)):
        dil = GROUPS[g][1]
        n = tm // dil
        for half in range(GROUP_W // LANES):
            c0 = g * GROUP_W + half * LANES
            col = val[:, c0:c0 + LANES]
            if perm_ref is None:
                out_ref[:, c0:c0 + LANES] = col.astype(BF16)
                continue
            slab = slab0 + (g - 1) * (GROUP_W // LANES) + half
            perm_ref[slab] = col
            for r in range(dil):
                out_ref[r * n:(r + 1) * n, c0:c0 + LANES] = (
                    perm_ref[slab, pl.ds(r, n, stride=dil), :].astype(BF16))


def _ffn_in_kernel(x_ref, g1_ref, wg_ref, wu_ref, wd_ref, gm_ref, win_ref, gq_ref, gk_ref, seg_ref,
                   x1_ref, u_ref, q_ref, k_ref, v_ref, *rest, tiles_per_seq, keeps):
    perm_ref = rest[-1] if len(rest) > 1 else None
    x1 = _ffn(x_ref[...], g1_ref, wg_ref, wu_ref, wd_ref)
    x1_ref[...] = x1
    h = _rms(x1, gm_ref[...]).astype(BF16)
    proj = _dot(h, win_ref[...])
    u_ref[...] = proj[:, :SSM_W]
    seg = seg_ref[...]
    q = _head_norm(proj[:, SSM_W:SSM_W + ATTN_W], gq_ref[...], seg)
    k = _head_norm(proj[:, SSM_W + ATTN_W:SSM_W + 2 * ATTN_W], gk_ref[...], seg)
    v = proj[:, SSM_W + 2 * ATTN_W:]
    slabs = (len(GROUPS) - 1) * (GROUP_W // LANES)
    _store_attn_rows(q * HEAD_DIM ** -0.5, q_ref, perm_ref, 0)
    _store_attn_rows(k, k_ref, perm_ref, slabs)
    _store_attn_rows(v, v_ref, perm_ref, 2 * slabs)
    if perm_ref is None:
        rest[0][:, :ATTN_W] = k
        rest[0][:, ATTN_W:] = v
        return
    tm = x1.shape[0]
    j = pl.program_id(0) % tiles_per_seq
    for g, keep in enumerate(keeps):
        lanes = slice(g * GROUP_W, (g + 1) * GROUP_W)
        cols = min(keep, tm)

        @pl.when(j >= tiles_per_seq - pl.cdiv(keep, tm))
        def _(g=g, lanes=lanes, cols=cols):
            rest[g][0, :GROUP_W, :] = k[:, lanes].T[:, tm - cols:]
            rest[g][0, GROUP_W:, :] = v[:, lanes].T[:, tm - cols:]


def _ffn_in(x, w, tm, tiles_per_seq, prompt_form):
    rows = x.shape[0]
    n_tiles = rows // tm
    n_seq = n_tiles // tiles_per_seq
    row_spec = lambda width: pl.BlockSpec((tm, width), lambda i: (i, 0))
    if prompt_form:
        keeps = tuple(min(window, tiles_per_seq * tm) for window, _ in GROUPS)
        assert all(keep % tm == 0 or keep < tm for keep in keeps)

        def tail_spec(keep):
            first = tiles_per_seq - pl.cdiv(keep, tm)
            return pl.BlockSpec((1, 2 * GROUP_W, min(keep, tm)),
                                lambda i: (i // tiles_per_seq, 0, jnp.maximum(i % tiles_per_seq - first, 0)))

        tail_shapes = tuple(jax.ShapeDtypeStruct((n_seq, 2 * GROUP_W, keep), F32) for keep in keeps)
        tail_specs = tuple(tail_spec(keep) for keep in keeps)
        scratch = [pltpu.VMEM((3 * (len(GROUPS) - 1) * (GROUP_W // LANES), tm, LANES), F32)]
    else:
        assert tiles_per_seq == 1
        keeps = ()
        tail_shapes = (jax.ShapeDtypeStruct((rows, 2 * ATTN_W), F32),)
        tail_specs = (row_spec(2 * ATTN_W),)
        scratch = []
    out_shape = (jax.ShapeDtypeStruct((rows, D_MODEL), F32),
                 jax.ShapeDtypeStruct((rows, SSM_W), F32),
                 jax.ShapeDtypeStruct((rows, ATTN_W), BF16),
                 jax.ShapeDtypeStruct((rows, ATTN_W), BF16),
                 jax.ShapeDtypeStruct((rows, ATTN_W), BF16)) + tail_shapes
    outs = pl.pallas_call(
        functools.partial(_ffn_in_kernel, tiles_per_seq=tiles_per_seq, keeps=keeps),
        grid=(n_tiles,),
        in_specs=[row_spec(D_MODEL), _const_spec((1, D_MODEL)),
                  _const_spec((D_MODEL, D_FF)), _const_spec((D_MODEL, D_FF)), _const_spec((D_FF, D_MODEL)),
                  _const_spec((1, D_MODEL)), _const_spec((D_MODEL, IN_A)),
                  _const_spec((1, ATTN_W)), _const_spec((1, ATTN_W)), _const_spec((GROUP_W, GROUP_W))],
        out_specs=(row_spec(D_MODEL), row_spec(SSM_W), row_spec(ATTN_W), row_spec(ATTN_W), row_spec(ATTN_W))
                  + tail_specs,
        out_shape=out_shape,
        scratch_shapes=scratch,
        compiler_params=_params(1),
        name="ffn_in",
    )(x, w["g_ffn1"], w["w1_gate"], w["w1_up"], w["w1_down"], w["g_mix"], w["w_in_a"],
      w["g_q"], w["g_k"], w["seg_mean"])
    return outs[:5], (list(outs[5:]) if prompt_form else outs[5])


def _glu_out(y, u, dsk_ref, wglu_ref, bglu_ref):
    y = jax.nn.gelu(y + dsk_ref[...] * u)
    z = _dot(y.astype(BF16), wglu_ref[...]) + bglu_ref[...]
    return (y * _sigmoid(z)).astype(BF16)


def _s5_kernel(u0_ref, u1_ref, u2_ref, u3_ref, a8re_ref, a8im_ref, k8_ref, w7_ref, gq_ref,
               dsk_ref, wglu_ref, bglu_ref, y_ref, hre_ref, him_ref,
               ls_ref, yq_ref, y2s_ref, sre_ref, sim_ref):
    u_refs = (u0_ref, u1_ref, u2_ref, u3_ref)
    nb, t_blk, _ = u0_ref.shape
    nblk = t_blk // BLK
    qw = 2 * QS

    @pl.when(pl.program_id(0) == 0)
    def _():
        sre_ref[...] = jnp.zeros_like(sre_ref)
        sim_ref[...] = jnp.zeros_like(sim_ref)

    sub = lax.broadcasted_iota(jnp.int32, (nblk, BLK, LANES), 1)
    for q in range(NQ):
        ublks = []
        for b in range(nb):
            uq = u_refs[q][b]
            z3 = _dot(uq.astype(BF16), k8_ref[q]).reshape(nblk, BLK, BLK * LANES)
            acc = z3[:, :, :LANES]
            for i in range(1, BLK):
                zi = z3[:, :, LANES * i:LANES * (i + 1)]
                acc = acc + jnp.where(sub >= i, pltpu.roll(zi, i, axis=1), 0.0)
            yq_ref[b * NQ + q] = acc.reshape(t_blk, LANES)
            ublks.append(jnp.concatenate(
                [u_refs[q][b, pl.ds(r, nblk, stride=BLK), :] for r in range(BLK)], axis=1))
        ls = _dot(jnp.concatenate(ublks, axis=0).astype(BF16), w7_ref[q])
        for b in range(nb):
            ls_ref[b, :, q * qw:(q + 1) * qw] = ls[b * nblk:(b + 1) * nblk]

    for q in range(NQ):
        ch = slice(q * QS, (q + 1) * QS)
        c_re = q * qw
        c_im = c_re + QS
        a_re = a8re_ref[:, ch]
        a_im = a8im_ref[:, ch]

        def body(k, carry, c_re=c_re, c_im=c_im, a_re=a_re, a_im=a_im):
            row = pl.ds(k, 1)
            out = []
            for b in range(nb):
                s_re, s_im = carry[2 * b], carry[2 * b + 1]
                l_re = ls_ref[b, row, c_re:c_re + QS]
                l_im = ls_ref[b, row, c_im:c_im + QS]
                ls_ref[b, row, c_re:c_re + QS] = s_re
                ls_ref[b, row, c_im:c_im + QS] = s_im
                out.append(a_re * s_re - a_im * s_im + l_re)
                out.append(a_re * s_im + a_im * s_re + l_im)
            return tuple(out)

        init = []
        for b in range(nb):
            init += [sre_ref[b:b + 1, ch], sim_ref[b:b + 1, ch]]
        fin = lax.fori_loop(0, nblk, body, tuple(init), unroll=True)
        for b in range(nb):
            sre_ref[b:b + 1, ch] = fin[2 * b]
            sim_ref[b:b + 1, ch] = fin[2 * b + 1]

    for q in range(NQ):
        sc = jnp.concatenate([ls_ref[b, :, q * qw:(q + 1) * qw] for b in range(nb)], axis=0)
        y2 = _dot(sc.astype(BF16), gq_ref[q])
        for b in range(nb):
            for r in range(BLK):
                y2s_ref[pl.ds(r, nblk, stride=BLK), :] = y2[b * nblk:(b + 1) * nblk, LANES * r:LANES * (r + 1)]
            yq_ref[b * NQ + q] = yq_ref[b * NQ + q] + y2s_ref[...]

    for b in range(nb):
        y = jnp.concatenate([yq_ref[b * NQ + q] for q in range(NQ)], axis=1)
        u = jnp.concatenate([u_refs[q][b] for q in range(NQ)], axis=1)
        y_ref[b] = _glu_out(y, u, dsk_ref, wglu_ref, bglu_ref)
    hre_ref[...] = sre_ref[...]
    him_ref[...] = sim_ref[...]


def _s5(u, w, t_blk):
    nb, seq, _ = u.shape
    nblk = t_blk // BLK
    u_specs = [pl.BlockSpec((nb, t_blk, LANES), lambda c, q=q: (0, c, q)) for q in range(NQ)]
    state = pl.BlockSpec((nb, N_STATE), lambda c: (0, 0))
    return pl.pallas_call(
        _s5_kernel,
        grid=(seq // t_blk,),
        in_specs=u_specs + [_const_spec((1, N_STATE)), _const_spec((1, N_STATE)),
                            _const_spec((NQ, LANES, BLK * LANES)),
                            _const_spec((NQ, BLK * LANES, 2 * QS)), _const_spec((NQ, 2 * QS, BLK * LANES)),
                            _const_spec((1, SSM_W)), _const_spec((SSM_W, SSM_W)), _const_spec((1, SSM_W))],
        out_specs=(pl.BlockSpec((nb, t_blk, SSM_W), lambda c: (0, c, 0)), state, state),
        out_shape=(jax.ShapeDtypeStruct((nb, seq, SSM_W), BF16),
                   jax.ShapeDtypeStruct((nb, N_STATE), F32), jax.ShapeDtypeStruct((nb, N_STATE), F32)),
        scratch_shapes=[pltpu.VMEM((nb, nblk, NQ * 2 * QS), F32),
                        pltpu.VMEM((nb * NQ, t_blk, LANES), F32),
                        pltpu.VMEM((t_blk, LANES), F32),
                        pltpu.VMEM((nb, N_STATE), F32), pltpu.VMEM((nb, N_STATE), F32)],
        compiler_params=_params(1),
        name="s5",
    )(u, u, u, u, w["a8_re"], w["a8_im"], w["k8"], w["w7"], w["gq"], w["ssm_d"], w["w_glu"], w["b_glu"])


def _s5_step_kernel(u_ref, hre_ref, him_ref, are_ref, aim_ref, k8_ref, w7_ref, gq_ref,
                    dsk_ref, wglu_ref, bglu_ref, y_ref, nre_ref, nim_ref):
    u = u_ref[...]
    ys = []
    for q in range(NQ):
        ch = slice(q * QS, (q + 1) * QS)
        uq = u[:, q * LANES:(q + 1) * LANES].astype(BF16)
        x = _dot(uq, w7_ref[q, (BLK - 1) * LANES:, :])
        a_re, a_im = are_ref[:, ch], aim_ref[:, ch]
        s_re, s_im = hre_ref[:, ch], him_ref[:, ch]
        nre_ref[:, ch] = a_re * s_re - a_im * s_im + x[:, :QS]
        nim_ref[:, ch] = a_re * s_im + a_im * s_re + x[:, QS:]
        s = jnp.concatenate([s_re, s_im], axis=1).astype(BF16)
        ys.append(_dot(s, gq_ref[q, :, :LANES]) + _dot(uq, k8_ref[q, :, :LANES]))
    y_ref[...] = _glu_out(jnp.concatenate(ys, axis=1), u, dsk_ref, wglu_ref, bglu_ref)


def _s5_step(u, h_re, h_im, w):
    rows = u.shape[0]
    return pl.pallas_call(
        _s5_step_kernel,
        out_shape=(jax.ShapeDtypeStruct((rows, SSM_W), BF16),
                   jax.ShapeDtypeStruct((rows, N_STATE), F32), jax.ShapeDtypeStruct((rows, N_STATE), F32)),
        compiler_params=pltpu.CompilerParams(vmem_limit_bytes=VMEM_LIMIT),
        name="s5_step",
    )(u, h_re, h_im, w["ab_re"], w["ab_im"], w["k8"], w["w7"], w["gq"], w["ssm_d"], w["w_glu"], w["b_glu"])


def _step_attention(q, kn, vn, cache, dil):
    n_feat, window = cache.shape
    row8 = lax.broadcasted_iota(jnp.int32, (8, GROUP_W), 0)
    head8 = lax.broadcasted_iota(jnp.int32, (8, GROUP_W), 1) // HEAD_DIM == row8
    diag8 = (lax.broadcasted_iota(jnp.int32, (8, LANES), 0)
             == lax.broadcasted_iota(jnp.int32, (8, LANES), 1))
    q8 = jnp.where(head8, jnp.broadcast_to(q, (8, GROUP_W)), 0.0)
    pos = lax.broadcasted_iota(jnp.int32, (8, window), 1)
    s = _dot(q8.astype(BF16), cache[:GROUP_W].astype(BF16))
    s = jnp.where(jnp.bitwise_and(pos, dil - 1) == 0, s, NEG)
    s_new = jnp.sum(q8 * kn, axis=-1, keepdims=True)
    m = jnp.maximum(jnp.max(s, axis=-1, keepdims=True), s_new)
    p = jnp.exp(s - m)
    p_new = jnp.exp(s_new - m)
    den = jnp.sum(p, axis=-1, keepdims=True) + p_new
    o8 = (_dot_nt(p.astype(BF16), cache[GROUP_W:].astype(BF16)) + p_new * vn) / den
    o = jnp.sum(jnp.where(head8, o8, 0.0), axis=0, keepdims=True)
    lse = jnp.sum(jnp.where(diag8, m + jnp.log(den), 0.0), axis=0, keepdims=True)
    eye = (lax.broadcasted_iota(jnp.int32, (n_feat, n_feat), 0)
           == lax.broadcasted_iota(jnp.int32, (n_feat, n_feat), 1))
    new_col = jnp.sum(jnp.where(eye, jnp.concatenate([kn, vn], axis=1), 0.0), axis=-1, keepdims=True)
    rolled = pltpu.roll(cache, window - 1, axis=1)
    last = lax.broadcasted_iota(jnp.int32, (n_feat, window), 1) == window - 1
    return o, lse, jnp.where(last, new_col, rolled)


def _swa_kernel(q_ref, k_ref, v_ref, kp_ref, vp_ref, qs_ref, kvs_ref, c_ref,
                o_ref, l_ref, os_ref, ls_ref, n_ref, *, qb, chunk, dil):
    first = pl.program_id(2) == 0

    def rows(ref):
        return jnp.concatenate([ref[t] for t in range(ref.shape[0])], axis=0)

    q = rows(q_ref)
    kk = jnp.concatenate([rows(kp_ref), rows(k_ref)], axis=0)
    vv = jnp.concatenate([rows(vp_ref), rows(v_ref)], axis=0)
    nk = 2 * KEYS_PER_QUERY
    row = lax.broadcasted_iota(jnp.int32, (128, nk), 0)
    col = lax.broadcasted_iota(jnp.int32, (128, nk), 1)
    band = jnp.where(col >= row, jnp.where(col <= row + KEYS_PER_QUERY, 0.0, NEG), NEG)
    band_first = band + jnp.where(col < KEYS_PER_QUERY, jnp.where(first, NEG, 0.0), 0.0)
    lane = lax.broadcasted_iota(jnp.int32, (1, GROUP_W), 1) // HEAD_DIM
    lane_l = lax.broadcasted_iota(jnp.int32, (1, LANES), 1)
    for j in range(qb // 128):
        qj = q[128 * j:128 * (j + 1)]
        kj = kk[128 * j:128 * j + nk]
        vj = vv[128 * j:128 * j + nk]
        bias = band_first if j == 0 else band
        qs = jnp.concatenate([jnp.where(lane == h, qj, jnp.zeros_like(qj)) for h in range(HEADS)], axis=0)
        s = (_dot_nt(qs, kj).reshape(HEADS, 128, nk) + bias[None]).reshape(HEADS * 128, nk)
        m = jnp.max(s, axis=-1, keepdims=True)
        p = jnp.exp(s - m)
        den = jnp.sum(p, axis=-1, keepdims=True)
        on = _dot(p.astype(BF16), vj) * (1.0 / den)
        lse = m + jnp.log(den)
        o_acc = jnp.zeros((128, GROUP_W), F32)
        l_acc = jnp.zeros((128, LANES), F32)
        for h in range(HEADS):
            rows_h = slice(128 * h, 128 * (h + 1))
            o_acc = jnp.where(lane == h, on[rows_h], o_acc)
            l_acc = jnp.where(lane_l == h, lse[rows_h], l_acc)
        o_out = o_acc.astype(BF16)
        if chunk >= 128:
            t, r0 = divmod(128 * j, chunk)
            o_ref[t, r0:r0 + 128, :] = o_out
            l_ref[t, r0:r0 + 128, :] = l_acc
        else:
            for s_ in range(128 // chunk):
                t = (128 * j) // chunk + s_
                o_ref[t] = o_out[s_ * chunk:(s_ + 1) * chunk]
                l_ref[t] = l_acc[s_ * chunk:(s_ + 1) * chunk]

    def sample_step():
        kv = kvs_ref[0]
        o, lse, new_cache = _step_attention(qs_ref[0].astype(F32), kv[:, :GROUP_W], kv[:, GROUP_W:],
                                            c_ref[0], dil)
        os_ref[0] = o.astype(BF16)
        ls_ref[0] = lse
        n_ref[0] = new_cache

    if c_ref.shape[-1] >= OWN_REGION_WINDOW:
        pl.when(pl.program_id(0) >= 0)(sample_step)
    else:
        sample_step()


def _swa(q, k, v, g, tile, q_s, kv_s, cache):
    window, dil = GROUPS[g]
    nb, seq, _ = q.shape
    n_sample = q_s.shape[0]
    n_tiles = seq // tile
    chunk = tile // dil
    n = seq // dil
    qb = min(512, n)
    tpb = qb // chunk
    n_i = n // qb
    assert nb * dil * n_i >= n_sample and cache.shape == (n_sample, 2 * GROUP_W, window)
    view = lambda a: a.reshape(nb, n_tiles, dil, chunk, a.shape[-1])
    cur = pl.BlockSpec((None, tpb, None, chunk, GROUP_W), lambda b, r, i: (b, i, r, 0, g))
    if chunk >= 128:
        prev = pl.BlockSpec((None, 1, None, 128, GROUP_W),
                            lambda b, r, i: (b, jnp.maximum(i * tpb - 1, 0), r, chunk // 128 - 1, g))
    else:
        ptiles = 128 // chunk
        prev = pl.BlockSpec((None, ptiles, None, chunk, GROUP_W),
                            lambda b, r, i: (b, jnp.maximum(i * (tpb // ptiles) - 1, 0), r, 0, g))
    out_block = lambda width: pl.BlockSpec((None, tpb, None, chunk, width), lambda b, r, i: (b, i, r, 0, 0))
    sample_block = lambda *dims: pl.BlockSpec(
        (1,) + dims, lambda b, r, i: (jnp.minimum((b * dil + r) * n_i + i, n_sample - 1),) + (0,) * len(dims))
    o, lse, o_s, l_s, new_cache = pl.pallas_call(
        functools.partial(_swa_kernel, qb=qb, chunk=chunk, dil=dil),
        grid=(nb, dil, n_i),
        in_specs=[cur, cur, cur, prev, prev,
                  sample_block(1, GROUP_W), sample_block(1, 2 * GROUP_W), sample_block(2 * GROUP_W, window)],
        out_specs=(out_block(GROUP_W), out_block(LANES),
                   sample_block(1, GROUP_W), sample_block(1, LANES), sample_block(2 * GROUP_W, window)),
        out_shape=(jax.ShapeDtypeStruct((nb, n_tiles, dil, chunk, GROUP_W), BF16),
                   jax.ShapeDtypeStruct((nb, n_tiles, dil, chunk, LANES), F32),
                   jax.ShapeDtypeStruct((n_sample, 1, GROUP_W), BF16),
                   jax.ShapeDtypeStruct((n_sample, 1, LANES), F32),
                   jax.ShapeDtypeStruct(cache.shape, F32)),
        compiler_params=_params(3),
        name="swa_g%d" % g,
    )(view(q), view(k), view(v), view(k), view(v),
      q_s.reshape(n_sample, 1, GROUP_W), kv_s.reshape(n_sample, 1, 2 * GROUP_W), cache)
    return (o.reshape(nb * seq, GROUP_W), lse.reshape(nb * seq, LANES),
            o_s.reshape(n_sample, GROUP_W), l_s.reshape(n_sample, LANES), new_cache)


def _position_order(o_ref, l_ref, dil, nat_ref, slab0):
    tm = o_ref.shape[0]
    n = tm // dil
    for r in range(dil):
        rows = slice(r * n, (r + 1) * n)
        dst = pl.ds(r, n, stride=dil)
        nat_ref[slab0, dst, :] = o_ref[rows, :LANES].astype(F32)
        nat_ref[slab0 + 1, dst, :] = o_ref[rows, LANES:].astype(F32)
        nat_ref[slab0 + 2, dst, :] = l_ref[rows, :]
    o = jnp.concatenate([nat_ref[slab0], nat_ref[slab0 + 1]], axis=1)
    return o, nat_ref[slab0 + 2]


def _mix_ffn_kernel(x1_ref, ys_ref, o0_ref, o1_ref, o2_ref, l0_ref, l1_ref, l2_ref,
                    gm_ref, wgt_ref, wsp_ref, wap_ref, wo_ref, seget_ref,
                    g2_ref, wg_ref, wu_ref, wd_ref, y_ref, *scratch):
    x1 = x1_ref[...]
    h = _rms(x1, gm_ref[...]).astype(BF16)
    gates = _sigmoid(_dot(h, wgt_ref[...]))
    os_ = [o0_ref[...].astype(F32)]
    ls = [l0_ref[...]]
    for g, (o_ref, l_ref) in enumerate(((o1_ref, l1_ref), (o2_ref, l2_ref)), start=1):
        if scratch:
            o, l = _position_order(o_ref, l_ref, GROUPS[g][1], scratch[0], 3 * (g - 1))
        else:
            o, l = o_ref[...].astype(F32), l_ref[...]
        os_.append(o)
        ls.append(l)
    l_top = jnp.maximum(jnp.maximum(ls[0], ls[1]), ls[2])
    es = [jnp.exp(l - l_top) for l in ls]
    inv = 1.0 / (es[0] + es[1] + es[2])
    seget = seget_ref[...]
    y_attn = None
    for e, o in zip(es, os_):
        t = _split_dot(e * inv, seget) * o
        y_attn = t if y_attn is None else y_attn + t
    mixed = (gates[:, :D_MODEL] * _dot(ys_ref[...], wsp_ref[...])
             + gates[:, D_MODEL:] * _dot(y_attn.astype(BF16), wap_ref[...]))
    x2 = x1 + _dot(mixed.astype(BF16), wo_ref[...])
    y_ref[...] = _ffn(x2, g2_ref, wg_ref, wu_ref, wd_ref)


def _mix_ffn(x1, y_ssm, os_, ls_, w, tm, residue_major):
    rows = x1.shape[0]
    row_spec = lambda width: pl.BlockSpec((tm, width), lambda i: (i, 0))
    n_slabs = 3 * (len(GROUPS) - 1)
    scratch = [pltpu.VMEM((n_slabs, tm, LANES), F32)] if residue_major else []
    return pl.pallas_call(
        _mix_ffn_kernel,
        grid=(rows // tm,),
        in_specs=[row_spec(D_MODEL), row_spec(SSM_W)] + [row_spec(GROUP_W)] * 3 + [row_spec(LANES)] * 3
                 + [_const_spec((1, D_MODEL)), _const_spec((D_MODEL, 2 * D_MODEL)),
                    _const_spec((SSM_W, D_MODEL)), _const_spec((GROUP_W, D_MODEL)),
                    _const_spec((D_MODEL, D_MODEL)), _const_spec((LANES, GROUP_W)),
                    _const_spec((1, D_MODEL)), _const_spec((D_MODEL, D_FF)), _const_spec((D_MODEL, D_FF)),
                    _const_spec((D_FF, D_MODEL))],
        out_specs=row_spec(D_MODEL),
        out_shape=jax.ShapeDtypeStruct((rows, D_MODEL), F32),
        scratch_shapes=scratch,
        compiler_params=_params(1),
        name="mix_ffn",
    )(x1, y_ssm, *os_, *ls_, w["g_mix"], w["w_gates"], w["w_ssm_proj"], w["w_attn_proj"], w["w_o"],
      w["seg_et"], w["g_ffn2"], w["w2_gate"], w["w2_up"], w["w2_down"])


def _prepare_weights(g_ffn1, w1_gate, w1_up, w1_down, g_mix, w_in, g_q, g_k,
                     ssm_a_re, ssm_a_im, ssm_log_dt, ssm_b_re, ssm_b_im, ssm_c_re, ssm_c_im,
                     ssm_d, w_glu, b_glu, w_ssm_proj, w_attn_proj, w_o, g_ffn2, w2_gate, w2_up, w2_down):
    w = {}
    row = lambda a: a.reshape(1, -1).astype(F32)
    w["g_ffn1"], w["g_mix"], w["g_ffn2"] = row(g_ffn1), row(g_mix), row(g_ffn2)
    for name, a in (("w1_gate", w1_gate), ("w1_up", w1_up), ("w1_down", w1_down),
                    ("w2_gate", w2_gate), ("w2_up", w2_up), ("w2_down", w2_down),
                    ("w_glu", w_glu), ("w_ssm_proj", w_ssm_proj), ("w_attn_proj", w_attn_proj),
                    ("w_o", w_o)):
        w[name] = a.astype(BF16)
    w["w_in_a"] = w_in[:, :IN_A].astype(BF16)
    w["w_gates"] = w_in[:, IN_A:].astype(BF16)
    per_head = lambda g: jnp.broadcast_to(g[:, None, :], (len(GROUPS), HEADS, HEAD_DIM)).reshape(1, ATTN_W)
    w["g_q"], w["g_k"] = per_head(g_q.astype(F32)), per_head(g_k.astype(F32))
    head_of_lane = jnp.arange(GROUP_W) // HEAD_DIM
    w["seg_mean"] = ((head_of_lane[:, None] == head_of_lane[None, :]) / HEAD_DIM).astype(BF16)
    w["seg_et"] = (jnp.arange(LANES)[:, None] == head_of_lane[None, :]).astype(BF16)
    w["ssm_d"], w["b_glu"] = row(ssm_d), row(b_glu)

    pw, w["k8"], w["w7"], w["gq"] = _ssm_prep(ssm_a_re, ssm_a_im, ssm_log_dt, ssm_b_re, ssm_b_im,
                                               ssm_c_re, ssm_c_im)
    per_group = lambda a: a.reshape(SSM_GROUPS, SSM_CH, SSM_P)[:, 0].reshape(1, N_STATE)
    w["ab_re"], w["ab_im"] = per_group(pw[0, 0]), per_group(pw[0, 1])
    w["a8_re"], w["a8_im"] = per_group(pw[1, 0]), per_group(pw[1, 1])
    return w


def _forward(x_p, x_s, state_s, caches_s, w):
    nb, seq, _ = x_p.shape
    n_s = x_s.shape[0]
    assert seq % TILE == 0 and x_s.shape[1] == 1
    (x1, u, q, k, v), caches_p = _ffn_in(x_p.reshape(nb * seq, D_MODEL), w, TILE, seq // TILE, True)
    (x1_s, u_s, q_s, _, _), kv_s = _ffn_in(x_s.reshape(n_s, D_MODEL), w, n_s, 1, False)
    y_ssm, *h_p = _s5(u.reshape(nb, seq, SSM_W), w, min(S5_T, seq))
    y_ssm_s, *h_s = _s5_step(u_s, state_s[0], state_s[1], w)
    q3, k3, v3 = (a.reshape(nb, seq, ATTN_W) for a in (q, k, v))
    os_p, ls_p, os_s, ls_s, new_caches_s = [], [], [], [], []
    for g in range(len(GROUPS)):
        lanes = slice(g * GROUP_W, (g + 1) * GROUP_W)
        kv_g = jnp.concatenate([kv_s[:, lanes], kv_s[:, ATTN_W + g * GROUP_W:ATTN_W + (g + 1) * GROUP_W]], axis=1)
        o, l, o_s, l_s, c_new = _swa(q3, k3, v3, g, TILE, q_s[:, lanes], kv_g, caches_s[g])
        for lst, a in zip((os_p, ls_p, os_s, ls_s, new_caches_s), (o, l, o_s, l_s, c_new)):
            lst.append(a)
    y_p = _mix_ffn(x1, y_ssm.reshape(nb * seq, SSM_W), os_p, ls_p, w, TILE, True)
    y_s = _mix_ffn(x1_s, y_ssm_s, os_s, ls_s, w, n_s, False)
    return (y_p.reshape(nb, seq, D_MODEL), y_s.reshape(n_s, 1, D_MODEL), h_p, h_s, caches_p, new_caches_s)


def _to_features_major(c):
    nb, win = c.shape[:2]
    return jnp.transpose(c, (0, 2, 3, 4, 1)).reshape(nb, 2 * GROUP_W, win)


def _to_window_buffer(c):
    nb, _, win = c.shape
    return jnp.transpose(c.reshape(nb, 2, HEADS, HEAD_DIM, win), (0, 4, 1, 2, 3))[None]


def kernel(x_prompt, x_sample, cache_kv_w128, cache_kv_w512, cache_kv_w2048, state_ssm_re, state_ssm_im, g_ffn1, w1_gate, w1_up, w1_down, g_mix, w_in, g_q, g_k, ssm_a_re, ssm_a_im, ssm_log_dt, ssm_b_re, ssm_b_im, ssm_c_re, ssm_c_im, ssm_d, w_glu, b_glu, w_ssm_proj, w_attn_proj, w_o, g_ffn2, w2_gate, w2_up, w2_down):
    layer_weights = (g_ffn1, w1_gate, w1_up, w1_down, g_mix, w_in, g_q, g_k,
                     ssm_a_re, ssm_a_im, ssm_log_dt, ssm_b_re, ssm_b_im, ssm_c_re, ssm_c_im,
                     ssm_d, w_glu, b_glu, w_ssm_proj, w_attn_proj, w_o, g_ffn2, w2_gate, w2_up, w2_down)
    depth = g_ffn1.shape[0]
    assert depth == 1, "window caches of deeper layers would need the previous layer's outputs"
    w = _prepare_weights(*(a[0] for a in layer_weights))
    nb_s = x_sample.shape[0]
    sdt = state_ssm_re.dtype
    as_state = lambda h: h.reshape(1, -1, SSM_GROUPS, SSM_P).astype(sdt)

    caches = [_to_features_major(c[0]) for c in (cache_kv_w128, cache_kv_w512, cache_kv_w2048)]
    state_s = (state_ssm_re[0].reshape(nb_s, N_STATE).astype(F32),
               state_ssm_im[0].reshape(nb_s, N_STATE).astype(F32))
    y_p, y_s, h_p, h_s, kv_p, kv_s = _forward(x_prompt, x_sample, state_s, caches, w)
    return ((y_p, y_s) + tuple(_to_window_buffer(c) for c in kv_p) + (as_state(h_p[0]), as_state(h_p[1]))
            + tuple(_to_window_buffer(c) for c in kv_s) + (as_state(h_s[0]), as_state(h_s[1])))
```

```python
import functools

import jax
import jax.numpy as jnp
from jax import lax
from jax.experimental import pallas as pl
from jax.experimental.pallas import tpu as pltpu

F32 = jnp.float32
BF16 = jnp.bfloat16

D_MODEL = 1024
D_FF = 2816
HEAD_DIM = 64
HEADS = 4
GROUPS = ((128, 1), (512, 4), (2048, 16))
KEYS_PER_QUERY = 128
GROUP_W = HEADS * HEAD_DIM
ATTN_W = len(GROUPS) * GROUP_W
SSM_W = 512
SSM_GROUPS = 32
SSM_CH = 16
SSM_P = 64
N_STATE = SSM_GROUPS * SSM_P
IN_A = SSM_W + 3 * ATTN_W
RMS_EPS = 1e-6
NEG = -1e30
LANES = 128

TILE = 512
FF_CHUNKS = ((0, 1024), (1024, 1024), (2048, 768))
BLK = 8
NQ = SSM_W // LANES
QS = N_STATE // NQ
S5_T = 1024
OWN_REGION_WINDOW = 2048
VMEM_LIMIT = 56 * 1024 * 1024


def _const_spec(shape):
    nd = len(shape)
    return pl.BlockSpec(shape, lambda *_: (0,) * nd, pipeline_mode=pl.Buffered(1))


def _params(n_grid):
    return pltpu.CompilerParams(dimension_semantics=("arbitrary",) * n_grid,
                                vmem_limit_bytes=VMEM_LIMIT)


def _rms(x, g):
    ms = jnp.mean(x * x, axis=-1, keepdims=True)
    return x * lax.rsqrt(ms + RMS_EPS) * g


def _dot(a, b):
    return jnp.dot(a, b, preferred_element_type=F32)


def _dot_nt(a, b):
    return lax.dot_general(a, b, (((1,), (1,)), ((), ())), preferred_element_type=F32)


def _split_dot(x, m):
    hi = x.astype(BF16)
    lo = (x - hi.astype(F32)).astype(BF16)
    return _dot(hi, m) + _dot(lo, m)


class _CastStream:
    def __init__(self, items, n_steps, step_of):
        self.cols = [(c0, cw) for _, c0, cw in items]
        self.arrays = [a for a, _, _ in items]
        self.in_specs, self.out_specs, self.out_shapes = [], [], []
        for a, c0, cw in items:
            rows, width = a.shape
            chunks = n_steps
            while rows % chunks or (rows // chunks) % 16:
                assert chunks % 2 == 0, (a.shape, n_steps)
                chunks //= 2
            index = lambda *g, hold=n_steps // chunks: (step_of(*g) // hold, 0)
            self.in_specs.append(pl.BlockSpec((rows // chunks, width), index))
            self.out_specs.append(pl.BlockSpec((rows // chunks, cw), index))
            self.out_shapes.append(jax.ShapeDtypeStruct((rows, cw), BF16))

    def __len__(self):
        return len(self.cols)

    def run(self, refs):
        n = len(self.cols)
        for i_ref, o_ref, (c0, cw) in zip(refs[:n], refs[n:], self.cols):
            o_ref[...] = i_ref[:, c0:c0 + cw].astype(BF16)


def _sigmoid(x):
    return 0.5 * jnp.tanh(0.5 * x) + 0.5


def _ffn(x, g_ref, wg_ref, wu_ref, wd_ref):
    xn = _rms(x, g_ref[...]).astype(BF16)
    acc = None
    for f0, fw in FF_CHUNKS:
        hg = _dot(xn, wg_ref[:, f0:f0 + fw])
        hu = _dot(xn, wu_ref[:, f0:f0 + fw])
        a = (hg * _sigmoid(hg) * hu).astype(BF16)
        d = _dot(a, wd_ref[f0:f0 + fw, :])
        acc = d if acc is None else acc + d
    return x + 0.5 * acc


def _split3_dot_nt(a, b):
    a_hi = a.astype(BF16)
    a_lo = (a - a_hi.astype(F32)).astype(BF16)
    b_hi = b.astype(BF16)
    b_lo = (b - b_hi.astype(F32)).astype(BF16)
    return _dot_nt(a_hi, b_hi) + _dot_nt(a_hi, b_lo) + _dot_nt(a_lo, b_hi)


def _prep_kernel(are_ref, aim_ref, ldt_ref, bre_ref, bim_ref, cre_ref, cim_ref, *rest, casts):
    n = len(casts)
    pw_ref, k8_ref, w7_ref, gq_ref = rest[n:n + 4]
    casts.run(rest[:n] + rest[n + 4:])
    _prep_quarter(are_ref, aim_ref, ldt_ref, bre_ref, bim_ref, cre_ref, cim_ref,
                  pw_ref, k8_ref, w7_ref, gq_ref)


def _prep_quarter(are_ref, aim_ref, ldt_ref, bre_ref, bim_ref, cre_ref, cim_ref,
                  pw_ref, k8_ref, w7_ref, gq_ref):
    a_re = are_ref[...]
    a_im = aim_ref[...]
    dt = jnp.exp(ldt_ref[...])
    mag = jnp.exp(a_re * dt)
    ab_re = mag * jnp.cos(a_im * dt)
    ab_im = mag * jnp.sin(a_im * dt)
    inv = 1.0 / (a_re * a_re + a_im * a_im)
    f_re = ((ab_re - 1.0) * a_re + ab_im * a_im) * inv
    f_im = (ab_im * a_re - (ab_re - 1.0) * a_im) * inv
    b_re = bre_ref[...]
    b_im = bim_ref[...]
    bb_re = f_re * b_re - f_im * b_im
    bb_im = f_re * b_im + f_im * b_re
    c_re = cre_ref[...]
    c_im = cim_ref[...]
    same_group = (lax.broadcasted_iota(jnp.int32, (LANES, LANES), 0) // SSM_CH
                  == lax.broadcasted_iota(jnp.int32, (LANES, LANES), 1) // SSM_CH)
    spread = (lax.broadcasted_iota(jnp.int32, (SSM_P, QS), 1) % SSM_P
              == lax.broadcasted_iota(jnp.int32, (SSM_P, QS), 0)).astype(BF16)
    spread_t = (lax.broadcasted_iota(jnp.int32, (QS, SSM_P), 0) % SSM_P
                == lax.broadcasted_iota(jnp.int32, (QS, SSM_P), 1)).astype(BF16)
    mask_w = (lax.broadcasted_iota(jnp.int32, (LANES, QS), 0) // SSM_CH
              == lax.broadcasted_iota(jnp.int32, (LANES, QS), 1) // SSM_P)
    mask_g = (lax.broadcasted_iota(jnp.int32, (QS, LANES), 0) // SSM_P
              == lax.broadcasted_iota(jnp.int32, (QS, LANES), 1) // SSM_CH)
    p_re = jnp.ones_like(ab_re)
    p_im = jnp.zeros_like(ab_re)
    for i in range(BLK + 1):
        if i == 1:
            pw_ref[0, 0] = p_re
            pw_ref[0, 1] = p_im
        if i == BLK:
            pw_ref[1, 0] = p_re
            pw_ref[1, 1] = p_im
        if i >= 1:
            r = i - 1
            gcs = (c_re * p_re - c_im * p_im, -(c_re * p_im + c_im * p_re))
            for part in range(2):
                blk = jnp.where(mask_g, _dot_nt(spread_t, gcs[part].astype(BF16)), 0.0)
                gq_ref[0, part * QS:(part + 1) * QS, r * LANES:(r + 1) * LANES] = blk.astype(BF16)
        if i < BLK:
            r = BLK - 1 - i
            xs = (p_re * bb_re - p_im * bb_im, p_re * bb_im + p_im * bb_re)
            kk = jnp.where(same_group, _split3_dot_nt(xs[0], c_re) - _split3_dot_nt(xs[1], c_im), 0.0)
            k8_ref[0, :, i * LANES:(i + 1) * LANES] = kk.astype(BF16)
            for part in range(2):
                blk = jnp.where(mask_w, _dot(xs[part].astype(BF16), spread), 0.0)
                w7_ref[0, r * LANES:(r + 1) * LANES, part * QS:(part + 1) * QS] = blk.astype(BF16)
        p_re, p_im = p_re * ab_re - p_im * ab_im, p_re * ab_im + p_im * ab_re


def _ssm_prep(a_re, a_im, log_dt, b_re, b_im, c_re, c_im, cast_items):
    gc = SSM_GROUPS * SSM_CH
    casts = _CastStream(cast_items, NQ, lambda q: q)
    rep = lambda a: jnp.repeat(a.astype(F32), SSM_CH, axis=0)
    b_rows = lambda b: jnp.transpose(b.astype(F32), (0, 2, 1)).reshape(gc, SSM_P)
    c_rows = lambda c: c.astype(F32).reshape(gc, SSM_P)
    rows = lambda width: pl.BlockSpec((LANES, width), lambda q: (q, 0))
    per_q = lambda d1, d2: pl.BlockSpec((1, d1, d2), lambda q: (q, 0, 0))
    outs = pl.pallas_call(
        functools.partial(_prep_kernel, casts=casts),
        grid=(NQ,),
        in_specs=[rows(SSM_P), rows(SSM_P), rows(1)] + [rows(SSM_P)] * 4 + casts.in_specs,
        out_specs=(pl.BlockSpec((2, 2, LANES, SSM_P), lambda q: (0, 0, q, 0)),
                   per_q(LANES, BLK * LANES), per_q(BLK * LANES, 2 * QS), per_q(2 * QS, BLK * LANES))
                  + tuple(casts.out_specs),
        out_shape=(jax.ShapeDtypeStruct((2, 2, gc, SSM_P), F32),
                   jax.ShapeDtypeStruct((NQ, LANES, BLK * LANES), BF16),
                   jax.ShapeDtypeStruct((NQ, BLK * LANES, 2 * QS), BF16),
                   jax.ShapeDtypeStruct((NQ, 2 * QS, BLK * LANES), BF16)) + tuple(casts.out_shapes),
        compiler_params=_params(1),
        name="ssm_prep",
    )(rep(a_re), rep(a_im), rep(log_dt.reshape(SSM_GROUPS, 1)), b_rows(b_re), b_rows(b_im),
      c_rows(c_re), c_rows(c_im), *casts.arrays)
    return outs[:4], outs[4:]


def _head_norm(t, g, seg):
    sq = (t * t).astype(BF16)
    ms = jnp.concatenate(
        [_dot(sq[:, s * GROUP_W:(s + 1) * GROUP_W], seg) for s in range(len(GROUPS))], axis=1)
    return t * lax.rsqrt(ms + RMS_EPS) * g


def _store_attn_rows(val, out_ref, perm_ref, slab0):
    tm = val.shape[0]
    out_ref[:, :GROUP_W] = val[:, :GROUP_W].astype(BF16)
    for g in range(1, len(GROUPS)):
        dil = GROUPS[g][1]
        n = tm // dil
        for half in range(GROUP_W // LANES):
            c0 = g * GROUP_W + half * LANES
            col = val[:, c0:c0 + LANES]
            if perm_ref is None:
                out_ref[:, c0:c0 + LANES] = col.astype(BF16)
                continue
            slab = slab0 + (g - 1) * (GROUP_W // LANES) + half
            perm_ref[slab] = col
            for r in range(dil):
                out_ref[r * n:(r + 1) * n, c0:c0 + LANES] = (
                    perm_ref[slab, pl.ds(r, n, stride=dil), :].astype(BF16))


def _ffn_in_kernel(x_ref, g1_ref, wg_ref, wu_ref, wd_ref, gm_ref, win_ref, gq_ref, gk_ref, seg_ref,
                   x1_ref, u_ref, q_ref, k_ref, v_ref, *rest, tiles_per_seq, keeps):
    perm_ref = rest[-1] if len(rest) > 1 else None
    x1 = _ffn(x_ref[...], g1_ref, wg_ref, wu_ref, wd_ref)
    x1_ref[...] = x1
    h = _rms(x1, gm_ref[...]).astype(BF16)
    proj = _dot(h, win_ref[...])
    u_ref[...] = proj[:, :SSM_W]
    seg = seg_ref[...]
    q = _head_norm(proj[:, SSM_W:SSM_W + ATTN_W], gq_ref[...], seg)
    k = _head_norm(proj[:, SSM_W + ATTN_W:SSM_W + 2 * ATTN_W], gk_ref[...], seg)
    v = proj[:, SSM_W + 2 * ATTN_W:]
    slabs = (len(GROUPS) - 1) * (GROUP_W // LANES)
    _store_attn_rows(q * HEAD_DIM ** -0.5, q_ref, perm_ref, 0)
    _store_attn_rows(k, k_ref, perm_ref, slabs)
    _store_attn_rows(v, v_ref, perm_ref, 2 * slabs)
    if perm_ref is None:
        rest[0][:, :ATTN_W] = k
        rest[0][:, ATTN_W:] = v
        return
    tm = x1.shape[0]
    j = pl.program_id(0) % tiles_per_seq
    for g, keep in enumerate(keeps):
        lanes = slice(g * GROUP_W, (g + 1) * GROUP_W)
        cols = min(keep, tm)

        @pl.when(j >= tiles_per_seq - pl.cdiv(keep, tm))
        def _(g=g, lanes=lanes, cols=cols):
            rest[g][0, :GROUP_W, :] = k[:, lanes].T[:, tm - cols:]
            rest[g][0, GROUP_W:, :] = v[:, lanes].T[:, tm - cols:]


def _ffn_in(x, w, tm, tiles_per_seq, prompt_form):
    rows = x.shape[0]
    n_tiles = rows // tm
    n_seq = n_tiles // tiles_per_seq
    row_spec = lambda width: pl.BlockSpec((tm, width), lambda i: (i, 0))
    if prompt_form:
        keeps = tuple(min(window, tiles_per_seq * tm) for window, _ in GROUPS)
        assert all(keep % tm == 0 or keep < tm for keep in keeps)

        def tail_spec(keep):
            first = tiles_per_seq - pl.cdiv(keep, tm)
            return pl.BlockSpec((1, 2 * GROUP_W, min(keep, tm)),
                                lambda i: (i // tiles_per_seq, 0, jnp.maximum(i % tiles_per_seq - first, 0)))

        tail_shapes = tuple(jax.ShapeDtypeStruct((n_seq, 2 * GROUP_W, keep), F32) for keep in keeps)
        tail_specs = tuple(tail_spec(keep) for keep in keeps)
        scratch = [pltpu.VMEM((3 * (len(GROUPS) - 1) * (GROUP_W // LANES), tm, LANES), F32)]
    else:
        assert tiles_per_seq == 1
        keeps = ()
        tail_shapes = (jax.ShapeDtypeStruct((rows, 2 * ATTN_W), F32),)
        tail_specs = (row_spec(2 * ATTN_W),)
        scratch = []
    out_shape = (jax.ShapeDtypeStruct((rows, D_MODEL), F32),
                 jax.ShapeDtypeStruct((rows, SSM_W), F32),
                 jax.ShapeDtypeStruct((rows, ATTN_W), BF16),
                 jax.ShapeDtypeStruct((rows, ATTN_W), BF16),
                 jax.ShapeDtypeStruct((rows, ATTN_W), BF16)) + tail_shapes
    outs = pl.pallas_call(
        functools.partial(_ffn_in_kernel, tiles_per_seq=tiles_per_seq, keeps=keeps),
        grid=(n_tiles,),
        in_specs=[row_spec(D_MODEL), _const_spec((1, D_MODEL)),
                  _const_spec((D_MODEL, D_FF)), _const_spec((D_MODEL, D_FF)), _const_spec((D_FF, D_MODEL)),
                  _const_spec((1, D_MODEL)), _const_spec((D_MODEL, IN_A)),
                  _const_spec((1, ATTN_W)), _const_spec((1, ATTN_W)), _const_spec((GROUP_W, GROUP_W))],
        out_specs=(row_spec(D_MODEL), row_spec(SSM_W), row_spec(ATTN_W), row_spec(ATTN_W), row_spec(ATTN_W))
                  + tail_specs,
        out_shape=out_shape,
        scratch_shapes=scratch,
        compiler_params=_params(1),
        name="ffn_in",
    )(x, w["g_ffn1"], w["w1_gate"], w["w1_up"], w["w1_down"], w["g_mix"], w["w_in_a"],
      w["g_q"], w["g_k"], w["seg_mean"])
    return outs[:5], (list(outs[5:]) if prompt_form else outs[5])


def _glu_out(y, u, dsk_ref, wglu_ref, bglu_ref):
    y = jax.nn.gelu(y + dsk_ref[...] * u)
    z = _dot(y.astype(BF16), wglu_ref[...]) + bglu_ref[...]
    return (y * _sigmoid(z)).astype(BF16)


def _s5_kernel(u0_ref, u1_ref, u2_ref, u3_ref, a8re_ref, a8im_ref, k8_ref, w7_ref, gq_ref,
               dsk_ref, wglu_ref, bglu_ref, y_ref, hre_ref, him_ref,
               ls_ref, yq_ref, y2s_ref, sre_ref, sim_ref):
    u_refs = (u0_ref, u1_ref, u2_ref, u3_ref)
    nb, t_blk, _ = u0_ref.shape
    nblk = t_blk // BLK
    qw = 2 * QS

    @pl.when(pl.program_id(0) == 0)
    def _():
        sre_ref[...] = jnp.zeros_like(sre_ref)
        sim_ref[...] = jnp.zeros_like(sim_ref)

    sub = lax.broadcasted_iota(jnp.int32, (nblk, BLK, LANES), 1)
    for q in range(NQ):
        ublks = []
        for b in range(nb):
            uq = u_refs[q][b]
            z3 = _dot(uq.astype(BF16), k8_ref[q]).reshape(nblk, BLK, BLK * LANES)
            acc = z3[:, :, :LANES]
            for i in range(1, BLK):
                zi = z3[:, :, LANES * i:LANES * (i + 1)]
                acc = acc + jnp.where(sub >= i, pltpu.roll(zi, i, axis=1), 0.0)
            yq_ref[b * NQ + q] = acc.reshape(t_blk, LANES)
            ublks.append(jnp.concatenate(
                [u_refs[q][b, pl.ds(r, nblk, stride=BLK), :] for r in range(BLK)], axis=1))
        ls = _dot(jnp.concatenate(ublks, axis=0).astype(BF16), w7_ref[q])
        for b in range(nb):
            ls_ref[b, :, q * qw:(q + 1) * qw] = ls[b * nblk:(b + 1) * nblk]

    for q in range(NQ):
        ch = slice(q * QS, (q + 1) * QS)
        c_re = q * qw
        c_im = c_re + QS
        a_re = a8re_ref[:, ch]
        a_im = a8im_ref[:, ch]

        def body(k, carry, c_re=c_re, c_im=c_im, a_re=a_re, a_im=a_im):
            row = pl.ds(k, 1)
            out = []
            for b in range(nb):
                s_re, s_im = carry[2 * b], carry[2 * b + 1]
                l_re = ls_ref[b, row, c_re:c_re + QS]
                l_im = ls_ref[b, row, c_im:c_im + QS]
                ls_ref[b, row, c_re:c_re + QS] = s_re
                ls_ref[b, row, c_im:c_im + QS] = s_im
                out.append(a_re * s_re - a_im * s_im + l_re)
                out.append(a_re * s_im + a_im * s_re + l_im)
            return tuple(out)

        init = []
        for b in range(nb):
            init += [sre_ref[b:b + 1, ch], sim_ref[b:b + 1, ch]]
        fin = lax.fori_loop(0, nblk, body, tuple(init), unroll=True)
        for b in range(nb):
            sre_ref[b:b + 1, ch] = fin[2 * b]
            sim_ref[b:b + 1, ch] = fin[2 * b + 1]

    for q in range(NQ):
        sc = jnp.concatenate([ls_ref[b, :, q * qw:(q + 1) * qw] for b in range(nb)], axis=0)
        y2 = _dot(sc.astype(BF16), gq_ref[q])
        for b in range(nb):
            for r in range(BLK):
                y2s_ref[pl.ds(r, nblk, stride=BLK), :] = y2[b * nblk:(b + 1) * nblk, LANES * r:LANES * (r + 1)]
            yq_ref[b * NQ + q] = yq_ref[b * NQ + q] + y2s_ref[...]

    for b in range(nb):
        y = jnp.concatenate([yq_ref[b * NQ + q] for q in range(NQ)], axis=1)
        u = jnp.concatenate([u_refs[q][b] for q in range(NQ)], axis=1)
        y_ref[b] = _glu_out(y, u, dsk_ref, wglu_ref, bglu_ref)
    hre_ref[...] = sre_ref[...]
    him_ref[...] = sim_ref[...]


def _s5(u, w, t_blk):
    nb, seq, _ = u.shape
    nblk = t_blk // BLK
    u_specs = [pl.BlockSpec((nb, t_blk, LANES), lambda c, q=q: (0, c, q)) for q in range(NQ)]
    state = pl.BlockSpec((nb, N_STATE), lambda c: (0, 0))
    return pl.pallas_call(
        _s5_kernel,
        grid=(seq // t_blk,),
        in_specs=u_specs + [_const_spec((1, N_STATE)), _const_spec((1, N_STATE)),
                            _const_spec((NQ, LANES, BLK * LANES)),
                            _const_spec((NQ, BLK * LANES, 2 * QS)), _const_spec((NQ, 2 * QS, BLK * LANES)),
                            _const_spec((1, SSM_W)), _const_spec((SSM_W, SSM_W)), _const_spec((1, SSM_W))],
        out_specs=(pl.BlockSpec((nb, t_blk, SSM_W), lambda c: (0, c, 0)), state, state),
        out_shape=(jax.ShapeDtypeStruct((nb, seq, SSM_W), BF16),
                   jax.ShapeDtypeStruct((nb, N_STATE), F32), jax.ShapeDtypeStruct((nb, N_STATE), F32)),
        scratch_shapes=[pltpu.VMEM((nb, nblk, NQ * 2 * QS), F32),
                        pltpu.VMEM((nb * NQ, t_blk, LANES), F32),
                        pltpu.VMEM((t_blk, LANES), F32),
                        pltpu.VMEM((nb, N_STATE), F32), pltpu.VMEM((nb, N_STATE), F32)],
        compiler_params=_params(1),
        name="s5",
    )(u, u, u, u, w["a8_re"], w["a8_im"], w["k8"], w["w7"], w["gq"], w["ssm_d"], w["w_glu"], w["b_glu"])


def _s5_step_kernel(u_ref, hre_ref, him_ref, are_ref, aim_ref, k8_ref, w7_ref, gq_ref,
                    dsk_ref, wglu_ref, bglu_ref, y_ref, nre_ref, nim_ref):
    u = u_ref[...]
    ys = []
    for q in range(NQ):
        ch = slice(q * QS, (q + 1) * QS)
        uq = u[:, q * LANES:(q + 1) * LANES].astype(BF16)
        x = _dot(uq, w7_ref[q, (BLK - 1) * LANES:, :])
        a_re, a_im = are_ref[:, ch], aim_ref[:, ch]
        s_re, s_im = hre_ref[:, ch], him_ref[:, ch]
        nre_ref[:, ch] = a_re * s_re - a_im * s_im + x[:, :QS]
        nim_ref[:, ch] = a_re * s_im + a_im * s_re + x[:, QS:]
        s = jnp.concatenate([s_re, s_im], axis=1).astype(BF16)
        ys.append(_dot(s, gq_ref[q, :, :LANES]) + _dot(uq, k8_ref[q, :, :LANES]))
    y_ref[...] = _glu_out(jnp.concatenate(ys, axis=1), u, dsk_ref, wglu_ref, bglu_ref)


def _s5_step(u, h_re, h_im, w):
    rows = u.shape[0]
    return pl.pallas_call(
        _s5_step_kernel,
        out_shape=(jax.ShapeDtypeStruct((rows, SSM_W), BF16),
                   jax.ShapeDtypeStruct((rows, N_STATE), F32), jax.ShapeDtypeStruct((rows, N_STATE), F32)),
        compiler_params=pltpu.CompilerParams(vmem_limit_bytes=VMEM_LIMIT),
        name="s5_step",
    )(u, h_re, h_im, w["ab_re"], w["ab_im"], w["k8"], w["w7"], w["gq"], w["ssm_d"], w["w_glu"], w["b_glu"])


def _step_attention(q, kn, vn, cache, dil):
    n_feat, window = cache.shape
    row8 = lax.broadcasted_iota(jnp.int32, (8, GROUP_W), 0)
    head8 = lax.broadcasted_iota(jnp.int32, (8, GROUP_W), 1) // HEAD_DIM == row8
    diag8 = (lax.broadcasted_iota(jnp.int32, (8, LANES), 0)
             == lax.broadcasted_iota(jnp.int32, (8, LANES), 1))
    q8 = jnp.where(head8, jnp.broadcast_to(q, (8, GROUP_W)), 0.0)
    pos = lax.broadcasted_iota(jnp.int32, (8, window), 1)
    s = _dot(q8.astype(BF16), cache[:GROUP_W].astype(BF16))
    s = jnp.where(jnp.bitwise_and(pos, dil - 1) == 0, s, NEG)
    s_new = jnp.sum(q8 * kn, axis=-1, keepdims=True)
    m = jnp.maximum(jnp.max(s, axis=-1, keepdims=True), s_new)
    p = jnp.exp(s - m)
    p_new = jnp.exp(s_new - m)
    den = jnp.sum(p, axis=-1, keepdims=True) + p_new
    o8 = (_dot_nt(p.astype(BF16), cache[GROUP_W:].astype(BF16)) + p_new * vn) / den
    o = jnp.sum(jnp.where(head8, o8, 0.0), axis=0, keepdims=True)
    lse = jnp.sum(jnp.where(diag8, m + jnp.log(den), 0.0), axis=0, keepdims=True)
    eye = (lax.broadcasted_iota(jnp.int32, (n_feat, n_feat), 0)
           == lax.broadcasted_iota(jnp.int32, (n_feat, n_feat), 1))
    new_col = jnp.sum(jnp.where(eye, jnp.concatenate([kn, vn], axis=1), 0.0), axis=-1, keepdims=True)
    rolled = pltpu.roll(cache, window - 1, axis=1)
    last = lax.broadcasted_iota(jnp.int32, (n_feat, window), 1) == window - 1
    return o, lse, jnp.where(last, new_col, rolled)


def _swa_kernel(q_ref, k_ref, v_ref, kp_ref, vp_ref, qs_ref, kvs_ref, c_ref, *rest, qb, chunk, dil, casts):
    n = len(casts)
    o_ref, l_ref, os_ref, ls_ref, n_ref = rest[n:n + 5]
    casts.run(rest[:n] + rest[n + 5:])
    first = pl.program_id(2) == 0

    def rows(ref):
        return jnp.concatenate([ref[t] for t in range(ref.shape[0])], axis=0)

    q = rows(q_ref)
    kk = jnp.concatenate([rows(kp_ref), rows(k_ref)], axis=0)
    vv = jnp.concatenate([rows(vp_ref), rows(v_ref)], axis=0)
    nk = 2 * KEYS_PER_QUERY
    row = lax.broadcasted_iota(jnp.int32, (128, nk), 0)
    col = lax.broadcasted_iota(jnp.int32, (128, nk), 1)
    band = jnp.where(col >= row, jnp.where(col <= row + KEYS_PER_QUERY, 0.0, NEG), NEG)
    band_first = band + jnp.where(col < KEYS_PER_QUERY, jnp.where(first, NEG, 0.0), 0.0)
    lane = lax.broadcasted_iota(jnp.int32, (1, GROUP_W), 1) // HEAD_DIM
    lane_l = lax.broadcasted_iota(jnp.int32, (1, LANES), 1)
    for j in range(qb // 128):
        qj = q[128 * j:128 * (j + 1)]
        kj = kk[128 * j:128 * j + nk]
        vj = vv[128 * j:128 * j + nk]
        bias = band_first if j == 0 else band
        qs = jnp.concatenate([jnp.where(lane == h, qj, jnp.zeros_like(qj)) for h in range(HEADS)], axis=0)
        s = (_dot_nt(qs, kj).reshape(HEADS, 128, nk) + bias[None]).reshape(HEADS * 128, nk)
        m = jnp.max(s, axis=-1, keepdims=True)
        p = jnp.exp(s - m)
        den = jnp.sum(p, axis=-1, keepdims=True)
        on = _dot(p.astype(BF16), vj) * (1.0 / den)
        lse = m + jnp.log(den)
        o_acc = jnp.zeros((128, GROUP_W), F32)
        l_acc = jnp.zeros((128, LANES), F32)
        for h in range(HEADS):
            rows_h = slice(128 * h, 128 * (h + 1))
            o_acc = jnp.where(lane == h, on[rows_h], o_acc)
            l_acc = jnp.where(lane_l == h, lse[rows_h], l_acc)
        o_out = o_acc.astype(BF16)
        if chunk >= 128:
            t, r0 = divmod(128 * j, chunk)
            o_ref[t, r0:r0 + 128, :] = o_out
            l_ref[t, r0:r0 + 128, :] = l_acc
        else:
            for s_ in range(128 // chunk):
                t = (128 * j) // chunk + s_
                o_ref[t] = o_out[s_ * chunk:(s_ + 1) * chunk]
                l_ref[t] = l_acc[s_ * chunk:(s_ + 1) * chunk]

    def sample_step():
        kv = kvs_ref[0]
        o, lse, new_cache = _step_attention(qs_ref[0].astype(F32), kv[:, :GROUP_W], kv[:, GROUP_W:],
                                            c_ref[0], dil)
        os_ref[0] = o.astype(BF16)
        ls_ref[0] = lse
        n_ref[0] = new_cache

    if c_ref.shape[-1] >= OWN_REGION_WINDOW:
        pl.when(pl.program_id(0) >= 0)(sample_step)
    else:
        sample_step()


def _swa(q, k, v, g, tile, q_s, kv_s, cache, cast_items):
    window, dil = GROUPS[g]
    nb, seq, _ = q.shape
    n_sample = q_s.shape[0]
    n_tiles = seq // tile
    chunk = tile // dil
    n = seq // dil
    qb = min(512, n)
    tpb = qb // chunk
    n_i = n // qb
    assert nb * dil * n_i >= n_sample and cache.shape == (n_sample, 2 * GROUP_W, window)
    view = lambda a: a.reshape(nb, n_tiles, dil, chunk, a.shape[-1])
    cur = pl.BlockSpec((None, tpb, None, chunk, GROUP_W), lambda b, r, i: (b, i, r, 0, g))
    if chunk >= 128:
        prev = pl.BlockSpec((None, 1, None, 128, GROUP_W),
                            lambda b, r, i: (b, jnp.maximum(i * tpb - 1, 0), r, chunk // 128 - 1, g))
    else:
        ptiles = 128 // chunk
        prev = pl.BlockSpec((None, ptiles, None, chunk, GROUP_W),
                            lambda b, r, i: (b, jnp.maximum(i * (tpb // ptiles) - 1, 0), r, 0, g))
    out_block = lambda width: pl.BlockSpec((None, tpb, None, chunk, width), lambda b, r, i: (b, i, r, 0, 0))
    step_of = lambda b, r, i: (b * dil + r) * n_i + i
    sample_block = lambda *dims: pl.BlockSpec(
        (1,) + dims, lambda b, r, i: (jnp.minimum(step_of(b, r, i), n_sample - 1),) + (0,) * len(dims))
    casts = _CastStream(cast_items, nb * dil * n_i, step_of)
    o, lse, o_s, l_s, new_cache, *cast_outs = pl.pallas_call(
        functools.partial(_swa_kernel, qb=qb, chunk=chunk, dil=dil, casts=casts),
        grid=(nb, dil, n_i),
        in_specs=[cur, cur, cur, prev, prev,
                  sample_block(1, GROUP_W), sample_block(1, 2 * GROUP_W), sample_block(2 * GROUP_W, window)]
                 + casts.in_specs,
        out_specs=(out_block(GROUP_W), out_block(LANES),
                   sample_block(1, GROUP_W), sample_block(1, LANES), sample_block(2 * GROUP_W, window))
                  + tuple(casts.out_specs),
        out_shape=(jax.ShapeDtypeStruct((nb, n_tiles, dil, chunk, GROUP_W), BF16),
                   jax.ShapeDtypeStruct((nb, n_tiles, dil, chunk, LANES), F32),
                   jax.ShapeDtypeStruct((n_sample, 1, GROUP_W), BF16),
                   jax.ShapeDtypeStruct((n_sample, 1, LANES), F32),
                   jax.ShapeDtypeStruct(cache.shape, F32)) + tuple(casts.out_shapes),
        compiler_params=_params(3),
        name="swa_g%d" % g,
    )(view(q), view(k), view(v), view(k), view(v),
      q_s.reshape(n_sample, 1, GROUP_W), kv_s.reshape(n_sample, 1, 2 * GROUP_W), cache, *casts.arrays)
    return (o.reshape(nb * seq, GROUP_W), lse.reshape(nb * seq, LANES),
            o_s.reshape(n_sample, GROUP_W), l_s.reshape(n_sample, LANES), new_cache, cast_outs)


def _position_order(o_ref, l_ref, dil, nat_ref, slab0):
    tm = o_ref.shape[0]
    n = tm // dil
    for r in range(dil):
        rows = slice(r * n, (r + 1) * n)
        dst = pl.ds(r, n, stride=dil)
        nat_ref[slab0, dst, :] = o_ref[rows, :LANES].astype(F32)
        nat_ref[slab0 + 1, dst, :] = o_ref[rows, LANES:].astype(F32)
        nat_ref[slab0 + 2, dst, :] = l_ref[rows, :]
    o = jnp.concatenate([nat_ref[slab0], nat_ref[slab0 + 1]], axis=1)
    return o, nat_ref[slab0 + 2]


def _mix_ffn_kernel(x1_ref, ys_ref, o0_ref, o1_ref, o2_ref, l0_ref, l1_ref, l2_ref,
                    gm_ref, wgt_ref, wsp_ref, wap_ref, wo_ref, seget_ref,
                    g2_ref, wg_ref, wu_ref, wd_ref, y_ref, *scratch):
    x1 = x1_ref[...]
    h = _rms(x1, gm_ref[...]).astype(BF16)
    gates = _sigmoid(_dot(h, wgt_ref[...]))
    os_ = [o0_ref[...].astype(F32)]
    ls = [l0_ref[...]]
    for g, (o_ref, l_ref) in enumerate(((o1_ref, l1_ref), (o2_ref, l2_ref)), start=1):
        if scratch:
            o, l = _position_order(o_ref, l_ref, GROUPS[g][1], scratch[0], 3 * (g - 1))
        else:
            o, l = o_ref[...].astype(F32), l_ref[...]
        os_.append(o)
        ls.append(l)
    l_top = jnp.maximum(jnp.maximum(ls[0], ls[1]), ls[2])
    es = [jnp.exp(l - l_top) for l in ls]
    inv = 1.0 / (es[0] + es[1] + es[2])
    seget = seget_ref[...]
    y_attn = None
    for e, o in zip(es, os_):
        t = _split_dot(e * inv, seget) * o
        y_attn = t if y_attn is None else y_attn + t
    mixed = (gates[:, :D_MODEL] * _dot(ys_ref[...], wsp_ref[...])
             + gates[:, D_MODEL:] * _dot(y_attn.astype(BF16), wap_ref[...]))
    x2 = x1 + _dot(mixed.astype(BF16), wo_ref[...])
    y_ref[...] = _ffn(x2, g2_ref, wg_ref, wu_ref, wd_ref)


def _mix_ffn(x1, y_ssm, os_, ls_, w, tm, residue_major):
    rows = x1.shape[0]
    row_spec = lambda width: pl.BlockSpec((tm, width), lambda i: (i, 0))
    n_slabs = 3 * (len(GROUPS) - 1)
    scratch = [pltpu.VMEM((n_slabs, tm, LANES), F32)] if residue_major else []
    return pl.pallas_call(
        _mix_ffn_kernel,
        grid=(rows // tm,),
        in_specs=[row_spec(D_MODEL), row_spec(SSM_W)] + [row_spec(GROUP_W)] * 3 + [row_spec(LANES)] * 3
                 + [_const_spec((1, D_MODEL)), _const_spec((D_MODEL, 2 * D_MODEL)),
                    _const_spec((SSM_W, D_MODEL)), _const_spec((GROUP_W, D_MODEL)),
                    _const_spec((D_MODEL, D_MODEL)), _const_spec((LANES, GROUP_W)),
                    _const_spec((1, D_MODEL)), _const_spec((D_MODEL, D_FF)), _const_spec((D_MODEL, D_FF)),
                    _const_spec((D_FF, D_MODEL))],
        out_specs=row_spec(D_MODEL),
        out_shape=jax.ShapeDtypeStruct((rows, D_MODEL), F32),
        scratch_shapes=scratch,
        compiler_params=_params(1),
        name="mix_ffn",
    )(x1, y_ssm, *os_, *ls_, w["g_mix"], w["w_gates"], w["w_ssm_proj"], w["w_attn_proj"], w["w_o"],
      w["seg_et"], w["g_ffn2"], w["w2_gate"], w["w2_up"], w["w2_down"])


def _prepare_weights(g_ffn1, w1_gate, w1_up, w1_down, g_mix, w_in, g_q, g_k,
                     ssm_a_re, ssm_a_im, ssm_log_dt, ssm_b_re, ssm_b_im, ssm_c_re, ssm_c_im,
                     ssm_d, w_glu, b_glu, w_ssm_proj, w_attn_proj, w_o, g_ffn2, w2_gate, w2_up, w2_down):
    w = {}
    row = lambda a: a.reshape(1, -1).astype(F32)
    w["g_ffn1"], w["g_mix"], w["g_ffn2"] = row(g_ffn1), row(g_mix), row(g_ffn2)
    w["w_glu"] = w_glu.astype(BF16)
    f32 = lambda a: a.astype(F32)
    w["late"] = (
        (("w2_gate", f32(w2_gate), 0, D_FF), ("w2_up", f32(w2_up), 0, D_FF)),
        (("w2_down", f32(w2_down), 0, D_MODEL), ("w_o", f32(w_o), 0, D_MODEL),
         ("w_ssm_proj", f32(w_ssm_proj), 0, D_MODEL), ("w_attn_proj", f32(w_attn_proj), 0, D_MODEL),
         ("w_gates", f32(w_in), IN_A, 2 * D_MODEL)),
        ())
    per_head = lambda g: jnp.broadcast_to(g[:, None, :], (len(GROUPS), HEADS, HEAD_DIM)).reshape(1, ATTN_W)
    w["g_q"], w["g_k"] = per_head(g_q.astype(F32)), per_head(g_k.astype(F32))
    head_of_lane = jnp.arange(GROUP_W) // HEAD_DIM
    w["seg_mean"] = ((head_of_lane[:, None] == head_of_lane[None, :]) / HEAD_DIM).astype(BF16)
    w["seg_et"] = (jnp.arange(LANES)[:, None] == head_of_lane[None, :]).astype(BF16)
    w["ssm_d"], w["b_glu"] = row(ssm_d), row(b_glu)

    early = (("w1_gate", f32(w1_gate), 0, D_FF), ("w1_up", f32(w1_up), 0, D_FF),
             ("w1_down", f32(w1_down), 0, D_MODEL), ("w_in_a", f32(w_in), 0, IN_A))
    (pw, w["k8"], w["w7"], w["gq"]), converted = _ssm_prep(
        ssm_a_re, ssm_a_im, ssm_log_dt, ssm_b_re, ssm_b_im, ssm_c_re, ssm_c_im, [e[1:] for e in early])
    w.update({e[0]: c for e, c in zip(early, converted)})
    per_group = lambda a: a.reshape(SSM_GROUPS, SSM_CH, SSM_P)[:, 0].reshape(1, N_STATE)
    w["ab_re"], w["ab_im"] = per_group(pw[0, 0]), per_group(pw[0, 1])
    w["a8_re"], w["a8_im"] = per_group(pw[1, 0]), per_group(pw[1, 1])
    return w


def _forward(x_p, x_s, state_s, caches_s, w):
    nb, seq, _ = x_p.shape
    n_s = x_s.shape[0]
    assert seq % TILE == 0 and x_s.shape[1] == 1
    (x1, u, q, k, v), caches_p = _ffn_in(x_p.reshape(nb * seq, D_MODEL), w, TILE, seq // TILE, True)
    (x1_s, u_s, q_s, _, _), kv_s = _ffn_in(x_s.reshape(n_s, D_MODEL), w, n_s, 1, False)
    y_ssm, *h_p = _s5(u.reshape(nb, seq, SSM_W), w, min(S5_T, seq))
    y_ssm_s, *h_s = _s5_step(u_s, state_s[0], state_s[1], w)
    q3, k3, v3 = (a.reshape(nb, seq, ATTN_W) for a in (q, k, v))
    os_p, ls_p, os_s, ls_s, new_caches_s = [], [], [], [], []
    for g in range(len(GROUPS)):
        lanes = slice(g * GROUP_W, (g + 1) * GROUP_W)
        kv_g = jnp.concatenate([kv_s[:, lanes], kv_s[:, ATTN_W + g * GROUP_W:ATTN_W + (g + 1) * GROUP_W]], axis=1)
        late = w["late"][g]
        o, l, o_s, l_s, c_new, converted = _swa(q3, k3, v3, g, TILE, q_s[:, lanes], kv_g, caches_s[g],
                                                [e[1:] for e in late])
        w = {**w, **{e[0]: c for e, c in zip(late, converted)}}
        for lst, a in zip((os_p, ls_p, os_s, ls_s, new_caches_s), (o, l, o_s, l_s, c_new)):
            lst.append(a)
    y_p = _mix_ffn(x1, y_ssm.reshape(nb * seq, SSM_W), os_p, ls_p, w, TILE, True)
    y_s = _mix_ffn(x1_s, y_ssm_s, os_s, ls_s, w, n_s, False)
    return (y_p.reshape(nb, seq, D_MODEL), y_s.reshape(n_s, 1, D_MODEL), h_p, h_s, caches_p, new_caches_s)


def _to_features_major(c):
    nb, win = c.shape[:2]
    return jnp.transpose(c, (0, 2, 3, 4, 1)).reshape(nb, 2 * GROUP_W, win)


def _to_window_buffer(c):
    nb, _, win = c.shape
    return jnp.transpose(c.reshape(nb, 2, HEADS, HEAD_DIM, win), (0, 4, 1, 2, 3))[None]


def kernel(x_prompt, x_sample, cache_kv_w128, cache_kv_w512, cache_kv_w2048, state_ssm_re, state_ssm_im, g_ffn1, w1_gate, w1_up, w1_down, g_mix, w_in, g_q, g_k, ssm_a_re, ssm_a_im, ssm_log_dt, ssm_b_re, ssm_b_im, ssm_c_re, ssm_c_im, ssm_d, w_glu, b_glu, w_ssm_proj, w_attn_proj, w_o, g_ffn2, w2_gate, w2_up, w2_down):
    layer_weights = (g_ffn1, w1_gate, w1_up, w1_down, g_mix, w_in, g_q, g_k,
                     ssm_a_re, ssm_a_im, ssm_log_dt, ssm_b_re, ssm_b_im, ssm_c_re, ssm_c_im,
                     ssm_d, w_glu, b_glu, w_ssm_proj, w_attn_proj, w_o, g_ffn2, w2_gate, w2_up, w2_down)
    depth = g_ffn1.shape[0]
    assert depth == 1, "window caches of deeper layers would need the previous layer's outputs"
    w = _prepare_weights(*(a[0] for a in layer_weights))
    nb_s = x_sample.shape[0]
    sdt = state_ssm_re.dtype
    as_state = lambda h: h.reshape(1, -1, SSM_GROUPS, SSM_P).astype(sdt)

    caches = [_to_features_major(c[0]) for c in (cache_kv_w128, cache_kv_w512, cache_kv_w2048)]
    state_s = (state_ssm_re[0].reshape(nb_s, N_STATE).astype(F32),
               state_ssm_im[0].reshape(nb_s, N_STATE).astype(F32))
    y_p, y_s, h_p, h_s, kv_p, kv_s = _forward(x_prompt, x_sample, state_s, caches, w)
    return ((y_p, y_s) + tuple(_to_window_buffer(c) for c in kv_p) + (as_state(h_p[0]), as_state(h_p[1]))
            + tuple(_to_window_buffer(c) for c in kv_s) + (as_state(h_s[0]), as_state(h_s[1])))
```

```python
import functools

import jax
import jax.numpy as jnp
from jax import lax
from jax.experimental import pallas as pl
from jax.experimental.pallas import tpu as pltpu

F32 = jnp.float32
BF16 = jnp.bfloat16

D_MODEL = 1024
D_FF = 2816
HEAD_DIM = 64
HEADS = 4
GROUPS = ((128, 1), (512, 4), (2048, 16))
KEYS_PER_QUERY = 128
GROUP_W = HEADS * HEAD_DIM
ATTN_W = len(GROUPS) * GROUP_W
SSM_W = 512
SSM_GROUPS = 32
SSM_CH = 16
SSM_P = 64
N_STATE = SSM_GROUPS * SSM_P
IN_A = SSM_W + 3 * ATTN_W
RMS_EPS = 1e-6
NEG = -1e30
LANES = 128
SUBLANES = 8

TILE = 512
FF_CHUNKS = ((0, 1024), (1024, 1024), (2048, 768))
BLK = 4
NQ = SSM_W // LANES
QS = N_STATE // NQ
S5_T = 1024
OWN_REGION_WINDOW = 2048
VMEM_LIMIT = 56 * 1024 * 1024


def _const_spec(shape):
    nd = len(shape)
    return pl.BlockSpec(shape, lambda *_: (0,) * nd, pipeline_mode=pl.Buffered(1))


def _params(n_grid):
    return pltpu.CompilerParams(dimension_semantics=("arbitrary",) * n_grid,
                                vmem_limit_bytes=VMEM_LIMIT)


def _rms(x, g):
    ms = jnp.mean(x * x, axis=-1, keepdims=True)
    return x * lax.rsqrt(ms + RMS_EPS) * g


def _dot(a, b):
    return jnp.dot(a, b, preferred_element_type=F32)


def _dot_nt(a, b):
    return lax.dot_general(a, b, (((1,), (1,)), ((), ())), preferred_element_type=F32)


def _split_dot(x, m):
    hi = x.astype(BF16)
    lo = (x - hi.astype(F32)).astype(BF16)
    return _dot(hi, m) + _dot(lo, m)


class _CastStream:
    def __init__(self, items, n_steps, step_of):
        self.cols = [(c0, cw) for _, c0, cw in items]
        self.arrays = [a for a, _, _ in items]
        self.in_specs, self.out_specs, self.out_shapes = [], [], []
        for a, c0, cw in items:
            rows, width = a.shape
            chunks = n_steps
            while rows % chunks or (rows // chunks) % 16:
                assert chunks % 2 == 0, (a.shape, n_steps)
                chunks //= 2
            index = lambda *g, hold=n_steps // chunks: (step_of(*g) // hold, 0)
            if c0 == 0:
                width = cw
            self.in_specs.append(pl.BlockSpec((rows // chunks, width), index))
            self.out_specs.append(pl.BlockSpec((rows // chunks, cw), index))
            self.out_shapes.append(jax.ShapeDtypeStruct((rows, cw), BF16))

    def __len__(self):
        return len(self.cols)

    def run(self, refs):
        n = len(self.cols)
        for i_ref, o_ref, (c0, cw) in zip(refs[:n], refs[n:], self.cols):
            o_ref[...] = i_ref[:, c0:c0 + cw].astype(BF16)


def _sigmoid(x):
    return 0.5 * jnp.tanh(0.5 * x) + 0.5


def _ffn(x, g_ref, wg_ref, wu_ref, wd_ref):
    xn = _rms(x, g_ref[...]).astype(BF16)
    acc = None
    for f0, fw in FF_CHUNKS:
        hg = _dot(xn, wg_ref[:, f0:f0 + fw])
        hu = _dot(xn, wu_ref[:, f0:f0 + fw])
        a = (hg * _sigmoid(hg) * hu).astype(BF16)
        d = _dot(a, wd_ref[f0:f0 + fw, :])
        acc = d if acc is None else acc + d
    return x + 0.5 * acc


def _split3_dot_nt(a, b):
    a_hi = a.astype(BF16)
    a_lo = (a - a_hi.astype(F32)).astype(BF16)
    b_hi = b.astype(BF16)
    b_lo = (b - b_hi.astype(F32)).astype(BF16)
    return _dot_nt(a_hi, b_hi) + _dot_nt(a_hi, b_lo) + _dot_nt(a_lo, b_hi)


def _prep_kernel(are_ref, aim_ref, ldt_ref, bre_ref, bim_ref, cre_ref, cim_ref, *rest, casts):
    n = len(casts)
    pw_ref, k8_ref, w7_ref, gq_ref = rest[n:n + 4]
    casts.run(rest[:n] + rest[n + 4:])
    _prep_quarter(are_ref, aim_ref, ldt_ref, bre_ref, bim_ref, cre_ref, cim_ref,
                  pw_ref, k8_ref, w7_ref, gq_ref)


def _prep_quarter(are_ref, aim_ref, ldt_ref, bre_ref, bim_ref, cre_ref, cim_ref,
                  pw_ref, k8_ref, w7_ref, gq_ref):
    a_re = are_ref[...]
    a_im = aim_ref[...]
    dt = jnp.exp(ldt_ref[...])
    mag = jnp.exp(a_re * dt)
    ab_re = mag * jnp.cos(a_im * dt)
    ab_im = mag * jnp.sin(a_im * dt)
    inv = 1.0 / (a_re * a_re + a_im * a_im)
    f_re = ((ab_re - 1.0) * a_re + ab_im * a_im) * inv
    f_im = (ab_im * a_re - (ab_re - 1.0) * a_im) * inv
    b_re = bre_ref[...]
    b_im = bim_ref[...]
    bb_re = f_re * b_re - f_im * b_im
    bb_im = f_re * b_im + f_im * b_re
    c_re = cre_ref[...]
    c_im = cim_ref[...]
    same_group = (lax.broadcasted_iota(jnp.int32, (LANES, LANES), 0) // SSM_CH
                  == lax.broadcasted_iota(jnp.int32, (LANES, LANES), 1) // SSM_CH)
    spread = (lax.broadcasted_iota(jnp.int32, (SSM_P, QS), 1) % SSM_P
              == lax.broadcasted_iota(jnp.int32, (SSM_P, QS), 0)).astype(BF16)
    spread_t = (lax.broadcasted_iota(jnp.int32, (QS, SSM_P), 0) % SSM_P
                == lax.broadcasted_iota(jnp.int32, (QS, SSM_P), 1)).astype(BF16)
    mask_w = (lax.broadcasted_iota(jnp.int32, (LANES, QS), 0) // SSM_CH
              == lax.broadcasted_iota(jnp.int32, (LANES, QS), 1) // SSM_P)
    mask_g = (lax.broadcasted_iota(jnp.int32, (QS, LANES), 0) // SSM_P
              == lax.broadcasted_iota(jnp.int32, (QS, LANES), 1) // SSM_CH)
    p_re = jnp.ones_like(ab_re)
    p_im = jnp.zeros_like(ab_re)
    for i in range(BLK + 1):
        if i == 1:
            pw_ref[0, 0] = p_re
            pw_ref[0, 1] = p_im
        if i == BLK:
            pw_ref[1, 0] = p_re
            pw_ref[1, 1] = p_im
        if i >= 1:
            r = i - 1
            gcs = (c_re * p_re - c_im * p_im, -(c_re * p_im + c_im * p_re))
            for part in range(2):
                blk = jnp.where(mask_g, _dot_nt(spread_t, gcs[part].astype(BF16)), 0.0)
                gq_ref[0, part * QS:(part + 1) * QS, r * LANES:(r + 1) * LANES] = blk.astype(BF16)
        if i < BLK:
            r = BLK - 1 - i
            xs = (p_re * bb_re - p_im * bb_im, p_re * bb_im + p_im * bb_re)
            kk = jnp.where(same_group, _split3_dot_nt(xs[0], c_re) - _split3_dot_nt(xs[1], c_im), 0.0)
            k8_ref[0, :, i * LANES:(i + 1) * LANES] = kk.astype(BF16)
            for part in range(2):
                blk = jnp.where(mask_w, _dot(xs[part].astype(BF16), spread), 0.0)
                w7_ref[0, r * LANES:(r + 1) * LANES, part * QS:(part + 1) * QS] = blk.astype(BF16)
        p_re, p_im = p_re * ab_re - p_im * ab_im, p_re * ab_im + p_im * ab_re


def _ssm_prep(a_re, a_im, log_dt, b_re, b_im, c_re, c_im, cast_items):
    gc = SSM_GROUPS * SSM_CH
    casts = _CastStream(cast_items, NQ, lambda q: q)
    rep = lambda a: jnp.repeat(a.astype(F32), SSM_CH, axis=0)
    b_rows = lambda b: jnp.transpose(b.astype(F32), (0, 2, 1)).reshape(gc, SSM_P)
    c_rows = lambda c: c.astype(F32).reshape(gc, SSM_P)
    rows = lambda width: pl.BlockSpec((LANES, width), lambda q: (q, 0))
    per_q = lambda d1, d2: pl.BlockSpec((1, d1, d2), lambda q: (q, 0, 0))
    outs = pl.pallas_call(
        functools.partial(_prep_kernel, casts=casts),
        grid=(NQ,),
        in_specs=[rows(SSM_P), rows(SSM_P), rows(1)] + [rows(SSM_P)] * 4 + casts.in_specs,
        out_specs=(pl.BlockSpec((2, 2, LANES, SSM_P), lambda q: (0, 0, q, 0)),
                   per_q(LANES, BLK * LANES), per_q(BLK * LANES, 2 * QS), per_q(2 * QS, BLK * LANES))
                  + tuple(casts.out_specs),
        out_shape=(jax.ShapeDtypeStruct((2, 2, gc, SSM_P), F32),
                   jax.ShapeDtypeStruct((NQ, LANES, BLK * LANES), BF16),
                   jax.ShapeDtypeStruct((NQ, BLK * LANES, 2 * QS), BF16),
                   jax.ShapeDtypeStruct((NQ, 2 * QS, BLK * LANES), BF16)) + tuple(casts.out_shapes),
        compiler_params=_params(1),
        name="ssm_prep",
    )(rep(a_re), rep(a_im), rep(log_dt.reshape(SSM_GROUPS, 1)), b_rows(b_re), b_rows(b_im),
      c_rows(c_re), c_rows(c_im), *casts.arrays)
    return outs[:4], outs[4:]


def _head_norm(t, g, seg):
    sq = (t * t).astype(BF16)
    ms = jnp.concatenate(
        [_dot(sq[:, s * GROUP_W:(s + 1) * GROUP_W], seg) for s in range(len(GROUPS))], axis=1)
    return t * lax.rsqrt(ms + RMS_EPS) * g


def _store_attn_rows(val, out_ref, perm_ref, slab0):
    tm = val.shape[0]
    out_ref[:, :GROUP_W] = val[:, :GROUP_W].astype(BF16)
    for g in range(1, len(GROUPS)):
        dil = GROUPS[g][1]
        n = tm // dil
        for half in range(GROUP_W // LANES):
            c0 = g * GROUP_W + half * LANES
            col = val[:, c0:c0 + LANES]
            if perm_ref is None:
                out_ref[:, c0:c0 + LANES] = col.astype(BF16)
                continue
            slab = slab0 + (g - 1) * (GROUP_W // LANES) + half
            perm_ref[slab] = col
            for r in range(dil):
                out_ref[r * n:(r + 1) * n, c0:c0 + LANES] = (
                    perm_ref[slab, pl.ds(r, n, stride=dil), :].astype(BF16))


def _ffn_in_kernel(x_ref, g1_ref, wg_ref, wu_ref, wd_ref, gm_ref, win_ref, gq_ref, gk_ref, seg_ref,
                   x1_ref, u_ref, q_ref, k_ref, v_ref, *rest, tiles_per_seq, keeps):
    perm_ref = rest[-1] if len(rest) > 1 else None
    x1 = _ffn(x_ref[...], g1_ref, wg_ref, wu_ref, wd_ref)
    x1_ref[...] = x1
    h = _rms(x1, gm_ref[...]).astype(BF16)
    proj = _dot(h, win_ref[...])
    u_ref[...] = proj[:, :SSM_W]
    seg = seg_ref[...]
    q = _head_norm(proj[:, SSM_W:SSM_W + ATTN_W], gq_ref[...], seg)
    k = _head_norm(proj[:, SSM_W + ATTN_W:SSM_W + 2 * ATTN_W], gk_ref[...], seg)
    v = proj[:, SSM_W + 2 * ATTN_W:]
    slabs = (len(GROUPS) - 1) * (GROUP_W // LANES)
    _store_attn_rows(q * HEAD_DIM ** -0.5, q_ref, perm_ref, 0)
    _store_attn_rows(k, k_ref, perm_ref, slabs)
    _store_attn_rows(v, v_ref, perm_ref, 2 * slabs)
    if perm_ref is None:
        rest[0][:, :ATTN_W] = k
        rest[0][:, ATTN_W:] = v
        return
    tm = x1.shape[0]
    j = pl.program_id(0) % tiles_per_seq
    for g, keep in enumerate(keeps):
        lanes = slice(g * GROUP_W, (g + 1) * GROUP_W)
        cols = min(keep, tm)

        @pl.when(j >= tiles_per_seq - pl.cdiv(keep, tm))
        def _(g=g, lanes=lanes, cols=cols):
            rest[g][0, :GROUP_W, :] = k[:, lanes].T[:, tm - cols:]
            rest[g][0, GROUP_W:, :] = v[:, lanes].T[:, tm - cols:]


def _ffn_in(x, w, tm, tiles_per_seq, prompt_form):
    rows = x.shape[0]
    n_tiles = rows // tm
    n_seq = n_tiles // tiles_per_seq
    row_spec = lambda width: pl.BlockSpec((tm, width), lambda i: (i, 0))
    if prompt_form:
        keeps = tuple(min(window, tiles_per_seq * tm) for window, _ in GROUPS)
        assert all(keep % tm == 0 or keep < tm for keep in keeps)

        def tail_spec(keep):
            first = tiles_per_seq - pl.cdiv(keep, tm)
            return pl.BlockSpec((1, 2 * GROUP_W, min(keep, tm)),
                                lambda i: (i // tiles_per_seq, 0, jnp.maximum(i % tiles_per_seq - first, 0)))

        tail_shapes = tuple(jax.ShapeDtypeStruct((n_seq, 2 * GROUP_W, keep), F32) for keep in keeps)
        tail_specs = tuple(tail_spec(keep) for keep in keeps)
        scratch = [pltpu.VMEM((3 * (len(GROUPS) - 1) * (GROUP_W // LANES), tm, LANES), F32)]
    else:
        assert tiles_per_seq == 1
        keeps = ()
        tail_shapes = (jax.ShapeDtypeStruct((rows, 2 * ATTN_W), F32),)
        tail_specs = (row_spec(2 * ATTN_W),)
        scratch = []
    out_shape = (jax.ShapeDtypeStruct((rows, D_MODEL), F32),
                 jax.ShapeDtypeStruct((rows, SSM_W), F32),
                 jax.ShapeDtypeStruct((rows, ATTN_W), BF16),
                 jax.ShapeDtypeStruct((rows, ATTN_W), BF16),
                 jax.ShapeDtypeStruct((rows, ATTN_W), BF16)) + tail_shapes
    outs = pl.pallas_call(
        functools.partial(_ffn_in_kernel, tiles_per_seq=tiles_per_seq, keeps=keeps),
        grid=(n_tiles,),
        in_specs=[row_spec(D_MODEL), _const_spec((1, D_MODEL)),
                  _const_spec((D_MODEL, D_FF)), _const_spec((D_MODEL, D_FF)), _const_spec((D_FF, D_MODEL)),
                  _const_spec((1, D_MODEL)), _const_spec((D_MODEL, IN_A)),
                  _const_spec((1, ATTN_W)), _const_spec((1, ATTN_W)), _const_spec((GROUP_W, GROUP_W))],
        out_specs=(row_spec(D_MODEL), row_spec(SSM_W), row_spec(ATTN_W), row_spec(ATTN_W), row_spec(ATTN_W))
                  + tail_specs,
        out_shape=out_shape,
        scratch_shapes=scratch,
        compiler_params=_params(1),
        name="ffn_in",
    )(x, w["g_ffn1"], w["w1_gate"], w["w1_up"], w["w1_down"], w["g_mix"], w["w_in_a"],
      w["g_q"], w["g_k"], w["seg_mean"])
    return outs[:5], (list(outs[5:]) if prompt_form else outs[5])


def _glu_out(y, u, dsk_ref, wglu_ref, bglu_ref):
    y = jax.nn.gelu(y + dsk_ref[...] * u)
    z = _dot(y.astype(BF16), wglu_ref[...]) + bglu_ref[...]
    return (y * _sigmoid(z)).astype(BF16)


def _s5_kernel(u0_ref, u1_ref, u2_ref, u3_ref, are_ref, aim_ref, k8_ref, w7_ref, gq_ref,
               dsk_ref, wglu_ref, bglu_ref, y_ref, hre_ref, him_ref,
               lsp_ref, yq_ref, y2s_ref, sp_ref):
    u_refs = (u0_ref, u1_ref, u2_ref, u3_ref)
    nb, t_blk, _ = u0_ref.shape
    nblk = t_blk // BLK
    tiles = QS // LANES

    @pl.when(pl.program_id(0) == 0)
    def _():
        sp_ref[...] = jnp.zeros_like(sp_ref)

    sub = jnp.bitwise_and(lax.broadcasted_iota(jnp.int32, (t_blk // SUBLANES, SUBLANES, LANES), 1), BLK - 1)
    for q in range(NQ):
        ublks = []
        for b in range(nb):
            uq = u_refs[q][b]
            z3 = _dot(uq.astype(BF16), k8_ref[q]).reshape(t_blk // SUBLANES, SUBLANES, BLK * LANES)
            acc = z3[:, :, :LANES]
            for i in range(1, BLK):
                zi = z3[:, :, LANES * i:LANES * (i + 1)]
                acc = acc + jnp.where(sub >= i, pltpu.roll(zi, i, axis=1), 0.0)
            yq_ref[b * NQ + q] = acc.reshape(t_blk, LANES)
            ublks.append(jnp.concatenate(
                [u_refs[q][b, pl.ds(r, nblk, stride=BLK), :] for r in range(BLK)], axis=1))
        ls = _dot(jnp.concatenate(ublks, axis=0).astype(BF16), w7_ref[q])
        for b in range(nb):
            for t in range(2 * tiles):
                dst = pl.ds((q % 2) * tiles + t % tiles, nblk, stride=SUBLANES)
                lsp_ref[b, t // tiles, q // 2, dst, :] = ls[b * nblk:(b + 1) * nblk, LANES * t:LANES * (t + 1)]

    chains = [(b, half) for b in range(nb) for half in range(2)]
    coef = {half: (are_ref[half], aim_ref[half]) for half in range(2)}
    state = {c: (sp_ref[c[0], 0, c[1]], sp_ref[c[0], 1, c[1]]) for c in chains}
    for k in range(nblk):
        rows = slice(SUBLANES * k, SUBLANES * (k + 1))
        for b, half in chains:
            a_re, a_im = coef[half]
            s_re, s_im = state[b, half]
            l_re = lsp_ref[b, 0, half, rows, :]
            l_im = lsp_ref[b, 1, half, rows, :]
            lsp_ref[b, 0, half, rows, :] = s_re
            lsp_ref[b, 1, half, rows, :] = s_im
            state[b, half] = (a_re * s_re - a_im * s_im + l_re, a_re * s_im + a_im * s_re + l_im)
    for (b, half), (s_re, s_im) in state.items():
        sp_ref[b, 0, half] = s_re
        sp_ref[b, 1, half] = s_im
        for j in range(SUBLANES):
            lanes = slice((half * SUBLANES + j) * LANES, (half * SUBLANES + j + 1) * LANES)
            hre_ref[b:b + 1, lanes] = s_re[j:j + 1]
            him_ref[b:b + 1, lanes] = s_im[j:j + 1]

    for q in range(NQ):
        sc = jnp.concatenate(
            [jnp.concatenate(
                [lsp_ref[b, t // tiles, q // 2, pl.ds((q % 2) * tiles + t % tiles, nblk, stride=SUBLANES), :]
                 for t in range(2 * tiles)], axis=1) for b in range(nb)], axis=0)
        y2 = _dot(sc.astype(BF16), gq_ref[q])
        for b in range(nb):
            for r in range(BLK):
                y2s_ref[pl.ds(r, nblk, stride=BLK), :] = y2[b * nblk:(b + 1) * nblk, LANES * r:LANES * (r + 1)]
            yq_ref[b * NQ + q] = yq_ref[b * NQ + q] + y2s_ref[...]

    for b in range(nb):
        y = jnp.concatenate([yq_ref[b * NQ + q] for q in range(NQ)], axis=1)
        u = jnp.concatenate([u_refs[q][b] for q in range(NQ)], axis=1)
        y_ref[b] = _glu_out(y, u, dsk_ref, wglu_ref, bglu_ref)


def _s5(u, w, t_blk):
    nb, seq, _ = u.shape
    nblk = t_blk // BLK
    u_specs = [pl.BlockSpec((nb, t_blk, LANES), lambda c, q=q: (0, c, q)) for q in range(NQ)]
    state = pl.BlockSpec((nb, N_STATE), lambda c: (0, 0))
    packed = lambda a: a.reshape(2, SUBLANES, LANES)
    return pl.pallas_call(
        _s5_kernel,
        grid=(seq // t_blk,),
        in_specs=u_specs + [_const_spec((2, SUBLANES, LANES)), _const_spec((2, SUBLANES, LANES)),
                            _const_spec((NQ, LANES, BLK * LANES)),
                            _const_spec((NQ, BLK * LANES, 2 * QS)), _const_spec((NQ, 2 * QS, BLK * LANES)),
                            _const_spec((1, SSM_W)), _const_spec((SSM_W, SSM_W)), _const_spec((1, SSM_W))],
        out_specs=(pl.BlockSpec((nb, t_blk, SSM_W), lambda c: (0, c, 0)), state, state),
        out_shape=(jax.ShapeDtypeStruct((nb, seq, SSM_W), BF16),
                   jax.ShapeDtypeStruct((nb, N_STATE), F32), jax.ShapeDtypeStruct((nb, N_STATE), F32)),
        scratch_shapes=[pltpu.VMEM((nb, 2, 2, nblk * SUBLANES, LANES), F32),
                        pltpu.VMEM((nb * NQ, t_blk, LANES), F32),
                        pltpu.VMEM((t_blk, LANES), F32),
                        pltpu.VMEM((nb, 2, 2, SUBLANES, LANES), F32)],
        compiler_params=_params(1),
        name="s5",
    )(u, u, u, u, packed(w["a8_re"]), packed(w["a8_im"]), w["k8"], w["w7"], w["gq"],
      w["ssm_d"], w["w_glu"], w["b_glu"])


def _s5_step_kernel(u_ref, hre_ref, him_ref, are_ref, aim_ref, k8_ref, w7_ref, gq_ref,
                    dsk_ref, wglu_ref, bglu_ref, y_ref, nre_ref, nim_ref):
    u = u_ref[...]
    ys = []
    for q in range(NQ):
        ch = slice(q * QS, (q + 1) * QS)
        uq = u[:, q * LANES:(q + 1) * LANES].astype(BF16)
        x = _dot(uq, w7_ref[q, (BLK - 1) * LANES:, :])
        a_re, a_im = are_ref[:, ch], aim_ref[:, ch]
        s_re, s_im = hre_ref[:, ch], him_ref[:, ch]
        nre_ref[:, ch] = a_re * s_re - a_im * s_im + x[:, :QS]
        nim_ref[:, ch] = a_re * s_im + a_im * s_re + x[:, QS:]
        s = jnp.concatenate([s_re, s_im], axis=1).astype(BF16)
        ys.append(_dot(s, gq_ref[q, :, :LANES]) + _dot(uq, k8_ref[q, :, :LANES]))
    y_ref[...] = _glu_out(jnp.concatenate(ys, axis=1), u, dsk_ref, wglu_ref, bglu_ref)


def _s5_step(u, h_re, h_im, w):
    rows = u.shape[0]
    return pl.pallas_call(
        _s5_step_kernel,
        out_shape=(jax.ShapeDtypeStruct((rows, SSM_W), BF16),
                   jax.ShapeDtypeStruct((rows, N_STATE), F32), jax.ShapeDtypeStruct((rows, N_STATE), F32)),
        compiler_params=pltpu.CompilerParams(vmem_limit_bytes=VMEM_LIMIT),
        name="s5_step",
    )(u, h_re, h_im, w["ab_re"], w["ab_im"], w["k8"], w["w7"], w["gq"], w["ssm_d"], w["w_glu"], w["b_glu"])


def _step_attention(q, kn, vn, cache, dil):
    n_feat, window = cache.shape
    row8 = lax.broadcasted_iota(jnp.int32, (8, GROUP_W), 0)
    head8 = lax.broadcasted_iota(jnp.int32, (8, GROUP_W), 1) // HEAD_DIM == row8
    diag8 = (lax.broadcasted_iota(jnp.int32, (8, LANES), 0)
             == lax.broadcasted_iota(jnp.int32, (8, LANES), 1))
    q8 = jnp.where(head8, jnp.broadcast_to(q, (8, GROUP_W)), 0.0)
    pos = lax.broadcasted_iota(jnp.int32, (8, window), 1)
    s = _dot(q8.astype(BF16), cache[:GROUP_W].astype(BF16))
    s = jnp.where(jnp.bitwise_and(pos, dil - 1) == 0, s, NEG)
    s_new = jnp.sum(q8 * kn, axis=-1, keepdims=True)
    m = jnp.maximum(jnp.max(s, axis=-1, keepdims=True), s_new)
    p = jnp.exp(s - m)
    p_new = jnp.exp(s_new - m)
    den = jnp.sum(p, axis=-1, keepdims=True) + p_new
    o8 = (_dot_nt(p.astype(BF16), cache[GROUP_W:].astype(BF16)) + p_new * vn) / den
    o = jnp.sum(jnp.where(head8, o8, 0.0), axis=0, keepdims=True)
    lse = jnp.sum(jnp.where(diag8, m + jnp.log(den), 0.0), axis=0, keepdims=True)
    eye = (lax.broadcasted_iota(jnp.int32, (n_feat, n_feat), 0)
           == lax.broadcasted_iota(jnp.int32, (n_feat, n_feat), 1))
    new_col = jnp.sum(jnp.where(eye, jnp.concatenate([kn, vn], axis=1), 0.0), axis=-1, keepdims=True)
    rolled = pltpu.roll(cache, window - 1, axis=1)
    last = lax.broadcasted_iota(jnp.int32, (n_feat, window), 1) == window - 1
    return o, lse, jnp.where(last, new_col, rolled)


def _swa_kernel(q_ref, k_ref, v_ref, kp_ref, vp_ref, qs_ref, kvs_ref, c_ref, *rest, qb, chunk, dil, casts):
    n = len(casts)
    o_ref, l_ref, os_ref, ls_ref, n_ref = rest[n:n + 5]
    casts.run(rest[:n] + rest[n + 5:])
    first = pl.program_id(2) == 0

    def rows(ref):
        return jnp.concatenate([ref[t] for t in range(ref.shape[0])], axis=0)

    q = rows(q_ref)
    kk = jnp.concatenate([rows(kp_ref), rows(k_ref)], axis=0)
    vv = jnp.concatenate([rows(vp_ref), rows(v_ref)], axis=0)
    nk = 2 * KEYS_PER_QUERY
    row = lax.broadcasted_iota(jnp.int32, (128, nk), 0)
    col = lax.broadcasted_iota(jnp.int32, (128, nk), 1)
    band = jnp.where(col >= row, jnp.where(col <= row + KEYS_PER_QUERY, 0.0, NEG), NEG)
    band_first = band + jnp.where(col < KEYS_PER_QUERY, jnp.where(first, NEG, 0.0), 0.0)
    lane = lax.broadcasted_iota(jnp.int32, (1, GROUP_W), 1) // HEAD_DIM
    lane_l = lax.broadcasted_iota(jnp.int32, (1, LANES), 1)
    for j in range(qb // 128):
        qj = q[128 * j:128 * (j + 1)]
        kj = kk[128 * j:128 * j + nk]
        vj = vv[128 * j:128 * j + nk]
        bias = band_first if j == 0 else band
        qs = jnp.concatenate([jnp.where(lane == h, qj, jnp.zeros_like(qj)) for h in range(HEADS)], axis=0)
        s = (_dot_nt(qs, kj).reshape(HEADS, 128, nk) + bias[None]).reshape(HEADS * 128, nk)
        m = jnp.max(s, axis=-1, keepdims=True)
        p = jnp.exp(s - m)
        den = jnp.sum(p, axis=-1, keepdims=True)
        on = _dot(p.astype(BF16), vj) * (1.0 / den)
        lse = m + jnp.log(den)
        o_acc = jnp.zeros((128, GROUP_W), F32)
        l_acc = jnp.zeros((128, LANES), F32)
        for h in range(HEADS):
            rows_h = slice(128 * h, 128 * (h + 1))
            o_acc = jnp.where(lane == h, on[rows_h], o_acc)
            l_acc = jnp.where(lane_l == h, lse[rows_h], l_acc)
        o_out = o_acc.astype(BF16)
        if chunk >= 128:
            t, r0 = divmod(128 * j, chunk)
            o_ref[t, r0:r0 + 128, :] = o_out
            l_ref[t, r0:r0 + 128, :] = l_acc
        else:
            for s_ in range(128 // chunk):
                t = (128 * j) // chunk + s_
                o_ref[t] = o_out[s_ * chunk:(s_ + 1) * chunk]
                l_ref[t] = l_acc[s_ * chunk:(s_ + 1) * chunk]

    def sample_step():
        kv = kvs_ref[0]
        o, lse, new_cache = _step_attention(qs_ref[0].astype(F32), kv[:, :GROUP_W], kv[:, GROUP_W:],
                                            c_ref[0], dil)
        os_ref[0] = o.astype(BF16)
        ls_ref[0] = lse
        n_ref[0] = new_cache

    if c_ref.shape[-1] >= OWN_REGION_WINDOW:
        pl.when(pl.program_id(0) >= 0)(sample_step)
    else:
        sample_step()


def _swa(q, k, v, g, tile, q_s, kv_s, cache, cast_items):
    window, dil = GROUPS[g]
    nb, seq, _ = q.shape
    n_sample = q_s.shape[0]
    n_tiles = seq // tile
    chunk = tile // dil
    n = seq // dil
    qb = min(512, n)
    tpb = qb // chunk
    n_i = n // qb
    assert nb * dil * n_i >= n_sample and cache.shape == (n_sample, 2 * GROUP_W, window)
    view = lambda a: a.reshape(nb, n_tiles, dil, chunk, a.shape[-1])
    cur = pl.BlockSpec((None, tpb, None, chunk, GROUP_W), lambda b, r, i: (b, i, r, 0, g))
    if chunk >= 128:
        prev = pl.BlockSpec((None, 1, None, 128, GROUP_W),
                            lambda b, r, i: (b, jnp.maximum(i * tpb - 1, 0), r, chunk // 128 - 1, g))
    else:
        ptiles = 128 // chunk
        prev = pl.BlockSpec((None, ptiles, None, chunk, GROUP_W),
                            lambda b, r, i: (b, jnp.maximum(i * (tpb // ptiles) - 1, 0), r, 0, g))
    out_block = lambda width: pl.BlockSpec((None, tpb, None, chunk, width), lambda b, r, i: (b, i, r, 0, 0))
    step_of = lambda b, r, i: (b * dil + r) * n_i + i
    sample_block = lambda *dims: pl.BlockSpec(
        (1,) + dims, lambda b, r, i: (jnp.minimum(step_of(b, r, i), n_sample - 1),) + (0,) * len(dims))
    casts = _CastStream(cast_items, nb * dil * n_i, step_of)
    o, lse, o_s, l_s, new_cache, *cast_outs = pl.pallas_call(
        functools.partial(_swa_kernel, qb=qb, chunk=chunk, dil=dil, casts=casts),
        grid=(nb, dil, n_i),
        in_specs=[cur, cur, cur, prev, prev,
                  sample_block(1, GROUP_W), sample_block(1, 2 * GROUP_W), sample_block(2 * GROUP_W, window)]
                 + casts.in_specs,
        out_specs=(out_block(GROUP_W), out_block(LANES),
                   sample_block(1, GROUP_W), sample_block(1, LANES), sample_block(2 * GROUP_W, window))
                  + tuple(casts.out_specs),
        out_shape=(jax.ShapeDtypeStruct((nb, n_tiles, dil, chunk, GROUP_W), BF16),
                   jax.ShapeDtypeStruct((nb, n_tiles, dil, chunk, LANES), F32),
                   jax.ShapeDtypeStruct((n_sample, 1, GROUP_W), BF16),
                   jax.ShapeDtypeStruct((n_sample, 1, LANES), F32),
                   jax.ShapeDtypeStruct(cache.shape, F32)) + tuple(casts.out_shapes),
        compiler_params=_params(3),
        name="swa_g%d" % g,
    )(view(q), view(k), view(v), view(k), view(v),
      q_s.reshape(n_sample, 1, GROUP_W), kv_s.reshape(n_sample, 1, 2 * GROUP_W), cache, *casts.arrays)
    return (o.reshape(nb * seq, GROUP_W), lse.reshape(nb * seq, LANES),
            o_s.reshape(n_sample, GROUP_W), l_s.reshape(n_sample, LANES), new_cache, cast_outs)


def _position_order(o_ref, l_ref, dil, nat_ref, slab0):
    tm = o_ref.shape[0]
    n = tm // dil
    for r in range(dil):
        rows = slice(r * n, (r + 1) * n)
        dst = pl.ds(r, n, stride=dil)
        nat_ref[slab0, dst, :] = o_ref[rows, :LANES].astype(F32)
        nat_ref[slab0 + 1, dst, :] = o_ref[rows, LANES:].astype(F32)
        nat_ref[slab0 + 2, dst, :] = l_ref[rows, :]
    o = jnp.concatenate([nat_ref[slab0], nat_ref[slab0 + 1]], axis=1)
    return o, nat_ref[slab0 + 2]


def _mix_ffn_kernel(x1_ref, ys_ref, o0_ref, o1_ref, o2_ref, l0_ref, l1_ref, l2_ref,
                    gm_ref, wgt_ref, wsp_ref, wap_ref, wo_ref, seget_ref,
                    g2_ref, wg_ref, wu_ref, wd_ref, y_ref, *scratch):
    x1 = x1_ref[...]
    h = _rms(x1, gm_ref[...]).astype(BF16)
    gates = _sigmoid(_dot(h, wgt_ref[...]))
    os_ = [o0_ref[...].astype(F32)]
    ls = [l0_ref[...]]
    for g, (o_ref, l_ref) in enumerate(((o1_ref, l1_ref), (o2_ref, l2_ref)), start=1):
        if scratch:
            o, l = _position_order(o_ref, l_ref, GROUPS[g][1], scratch[0], 3 * (g - 1))
        else:
            o, l = o_ref[...].astype(F32), l_ref[...]
        os_.append(o)
        ls.append(l)
    l_top = jnp.maximum(jnp.maximum(ls[0], ls[1]), ls[2])
    es = [jnp.exp(l - l_top) for l in ls]
    inv = 1.0 / (es[0] + es[1] + es[2])
    seget = seget_ref[...]
    y_attn = None
    for e, o in zip(es, os_):
        t = _split_dot(e * inv, seget) * o
        y_attn = t if y_attn is None else y_attn + t
    mixed = (gates[:, :D_MODEL] * _dot(ys_ref[...], wsp_ref[...])
             + gates[:, D_MODEL:] * _dot(y_attn.astype(BF16), wap_ref[...]))
    x2 = x1 + _dot(mixed.astype(BF16), wo_ref[...])
    y_ref[...] = _ffn(x2, g2_ref, wg_ref, wu_ref, wd_ref)


def _mix_ffn(x1, y_ssm, os_, ls_, w, tm, residue_major):
    rows = x1.shape[0]
    row_spec = lambda width: pl.BlockSpec((tm, width), lambda i: (i, 0))
    n_slabs = 3 * (len(GROUPS) - 1)
    scratch = [pltpu.VMEM((n_slabs, tm, LANES), F32)] if residue_major else []
    return pl.pallas_call(
        _mix_ffn_kernel,
        grid=(rows // tm,),
        in_specs=[row_spec(D_MODEL), row_spec(SSM_W)] + [row_spec(GROUP_W)] * 3 + [row_spec(LANES)] * 3
                 + [_const_spec((1, D_MODEL)), _const_spec((D_MODEL, 2 * D_MODEL)),
                    _const_spec((SSM_W, D_MODEL)), _const_spec((GROUP_W, D_MODEL)),
                    _const_spec((D_MODEL, D_MODEL)), _const_spec((LANES, GROUP_W)),
                    _const_spec((1, D_MODEL)), _const_spec((D_MODEL, D_FF)), _const_spec((D_MODEL, D_FF)),
                    _const_spec((D_FF, D_MODEL))],
        out_specs=row_spec(D_MODEL),
        out_shape=jax.ShapeDtypeStruct((rows, D_MODEL), F32),
        scratch_shapes=scratch,
        compiler_params=_params(1),
        name="mix_ffn",
    )(x1, y_ssm, *os_, *ls_, w["g_mix"], w["w_gates"], w["w_ssm_proj"], w["w_attn_proj"], w["w_o"],
      w["seg_et"], w["g_ffn2"], w["w2_gate"], w["w2_up"], w["w2_down"])


def _prepare_weights(g_ffn1, w1_gate, w1_up, w1_down, g_mix, w_in, g_q, g_k,
                     ssm_a_re, ssm_a_im, ssm_log_dt, ssm_b_re, ssm_b_im, ssm_c_re, ssm_c_im,
                     ssm_d, w_glu, b_glu, w_ssm_proj, w_attn_proj, w_o, g_ffn2, w2_gate, w2_up, w2_down):
    w = {}
    row = lambda a: a.reshape(1, -1).astype(F32)
    w["g_ffn1"], w["g_mix"], w["g_ffn2"] = row(g_ffn1), row(g_mix), row(g_ffn2)
    w["w_glu"] = w_glu.astype(BF16)
    f32 = lambda a: a.astype(F32)
    w["late"] = (
        (("w2_gate", f32(w2_gate), 0, D_FF), ("w2_up", f32(w2_up), 0, D_FF)),
        (("w2_down", f32(w2_down), 0, D_MODEL), ("w_o", f32(w_o), 0, D_MODEL),
         ("w_ssm_proj", f32(w_ssm_proj), 0, D_MODEL), ("w_attn_proj", f32(w_attn_proj), 0, D_MODEL),
         ("w_gates", f32(w_in), IN_A, 2 * D_MODEL)),
        ())
    per_head = lambda g: jnp.broadcast_to(g[:, None, :], (len(GROUPS), HEADS, HEAD_DIM)).reshape(1, ATTN_W)
    w["g_q"], w["g_k"] = per_head(g_q.astype(F32)), per_head(g_k.astype(F32))
    head_of_lane = jnp.arange(GROUP_W) // HEAD_DIM
    w["seg_mean"] = ((head_of_lane[:, None] == head_of_lane[None, :]) / HEAD_DIM).astype(BF16)
    w["seg_et"] = (jnp.arange(LANES)[:, None] == head_of_lane[None, :]).astype(BF16)
    w["ssm_d"], w["b_glu"] = row(ssm_d), row(b_glu)

    early = (("w1_gate", f32(w1_gate), 0, D_FF), ("w1_up", f32(w1_up), 0, D_FF),
             ("w1_down", f32(w1_down), 0, D_MODEL), ("w_in_a", f32(w_in), 0, IN_A))
    (pw, w["k8"], w["w7"], w["gq"]), converted = _ssm_prep(
        ssm_a_re, ssm_a_im, ssm_log_dt, ssm_b_re, ssm_b_im, ssm_c_re, ssm_c_im, [e[1:] for e in early])
    w.update({e[0]: c for e, c in zip(early, converted)})
    per_group = lambda a: a.reshape(SSM_GROUPS, SSM_CH, SSM_P)[:, 0].reshape(1, N_STATE)
    w["ab_re"], w["ab_im"] = per_group(pw[0, 0]), per_group(pw[0, 1])
    w["a8_re"], w["a8_im"] = per_group(pw[1, 0]), per_group(pw[1, 1])
    return w


def _forward(x_p, x_s, state_s, caches_s, w):
    nb, seq, _ = x_p.shape
    n_s = x_s.shape[0]
    assert seq % TILE == 0 and x_s.shape[1] == 1
    (x1, u, q, k, v), caches_p = _ffn_in(x_p.reshape(nb * seq, D_MODEL), w, TILE, seq // TILE, True)
    (x1_s, u_s, q_s, _, _), kv_s = _ffn_in(x_s.reshape(n_s, D_MODEL), w, n_s, 1, False)
    y_ssm, *h_p = _s5(u.reshape(nb, seq, SSM_W), w, min(S5_T, seq))
    y_ssm_s, *h_s = _s5_step(u_s, state_s[0], state_s[1], w)
    q3, k3, v3 = (a.reshape(nb, seq, ATTN_W) for a in (q, k, v))
    os_p, ls_p, os_s, ls_s, new_caches_s = [], [], [], [], []
    for g in range(len(GROUPS)):
        lanes = slice(g * GROUP_W, (g + 1) * GROUP_W)
        kv_g = jnp.concatenate([kv_s[:, lanes], kv_s[:, ATTN_W + g * GROUP_W:ATTN_W + (g + 1) * GROUP_W]], axis=1)
        late = w["late"][g]
        o, l, o_s, l_s, c_new, converted = _swa(q3, k3, v3, g, TILE, q_s[:, lanes], kv_g, caches_s[g],
                                                [e[1:] for e in late])
        w = {**w, **{e[0]: c for e, c in zip(late, converted)}}
        for lst, a in zip((os_p, ls_p, os_s, ls_s, new_caches_s), (o, l, o_s, l_s, c_new)):
            lst.append(a)
    y_p = _mix_ffn(x1, y_ssm.reshape(nb * seq, SSM_W), os_p, ls_p, w, TILE, True)
    y_s = _mix_ffn(x1_s, y_ssm_s, os_s, ls_s, w, n_s, False)
    return (y_p.reshape(nb, seq, D_MODEL), y_s.reshape(n_s, 1, D_MODEL), h_p, h_s, caches_p, new_caches_s)


def _to_features_major(c):
    nb, win = c.shape[:2]
    return jnp.transpose(c, (0, 2, 3, 4, 1)).reshape(nb, 2 * GROUP_W, win)


def _to_window_buffer(c):
    nb, _, win = c.shape
    return jnp.transpose(c.reshape(nb, 2, HEADS, HEAD_DIM, win), (0, 4, 1, 2, 3))[None]


def kernel(x_prompt, x_sample, cache_kv_w128, cache_kv_w512, cache_kv_w2048, state_ssm_re, state_ssm_im, g_ffn1, w1_gate, w1_up, w1_down, g_mix, w_in, g_q, g_k, ssm_a_re, ssm_a_im, ssm_log_dt, ssm_b_re, ssm_b_im, ssm_c_re, ssm_c_im, ssm_d, w_glu, b_glu, w_ssm_proj, w_attn_proj, w_o, g_ffn2, w2_gate, w2_up, w2_down):
    layer_weights = (g_ffn1, w1_gate, w1_up, w1_down, g_mix, w_in, g_q, g_k,
                     ssm_a_re, ssm_a_im, ssm_log_dt, ssm_b_re, ssm_b_im, ssm_c_re, ssm_c_im,
                     ssm_d, w_glu, b_glu, w_ssm_proj, w_attn_proj, w_o, g_ffn2, w2_gate, w2_up, w2_down)
    depth = g_ffn1.shape[0]
    assert depth == 1, "window caches of deeper layers would need the previous layer's outputs"
    w = _prepare_weights(*(a[0] for a in layer_weights))
    nb_s = x_sample.shape[0]
    sdt = state_ssm_re.dtype
    as_state = lambda h: h.reshape(1, -1, SSM_GROUPS, SSM_P).astype(sdt)

    caches = [_to_features_major(c[0]) for c in (cache_kv_w128, cache_kv_w512, cache_kv_w2048)]
    state_s = (state_ssm_re[0].reshape(nb_s, N_STATE).astype(F32),
               state_ssm_im[0].reshape(nb_s, N_STATE).astype(F32))
    y_p, y_s, h_p, h_s, kv_p, kv_s = _forward(x_prompt, x_sample, state_s, caches, w)
    return ((y_p, y_s) + tuple(_to_window_buffer(c) for c in kv_p) + (as_state(h_p[0]), as_state(h_p[1]))
            + tuple(_to_window_buffer(c) for c in kv_s) + (as_state(h_s[0]), as_state(h_s[1])))
```

```python
import functools

import jax
import jax.numpy as jnp
from jax import lax
from jax.experimental import pallas as pl
from jax.experimental.pallas import tpu as pltpu

F32 = jnp.float32
BF16 = jnp.bfloat16

D_MODEL = 1024
D_FF = 2816
HEAD_DIM = 64
HEADS = 4
GROUPS = ((128, 1), (512, 4), (2048, 16))
KEYS_PER_QUERY = 128
GROUP_W = HEADS * HEAD_DIM
ATTN_W = len(GROUPS) * GROUP_W
SSM_W = 512
SSM_GROUPS = 32
SSM_CH = 16
SSM_P = 64
N_STATE = SSM_GROUPS * SSM_P
IN_A = SSM_W + 3 * ATTN_W
RMS_EPS = 1e-6
NEG = -1e30
LANES = 128
SUBLANES = 8

TILE = 512
FF_CHUNKS = ((0, 1024), (1024, 1024), (2048, 768))
BLK = 4
NQ = SSM_W // LANES
QS = N_STATE // NQ
S5_T = 1024
OWN_REGION_WINDOW = 2048
VMEM_LIMIT = 56 * 1024 * 1024


def _const_spec(shape):
    nd = len(shape)
    return pl.BlockSpec(shape, lambda *_: (0,) * nd, pipeline_mode=pl.Buffered(1))


def _params(n_grid):
    return pltpu.CompilerParams(dimension_semantics=("arbitrary",) * n_grid,
                                vmem_limit_bytes=VMEM_LIMIT)


def _rms(x, g):
    ms = jnp.mean(x * x, axis=-1, keepdims=True)
    return x * lax.rsqrt(ms + RMS_EPS) * g


def _dot(a, b):
    return jnp.dot(a, b, preferred_element_type=F32)


def _dot_nt(a, b):
    return lax.dot_general(a, b, (((1,), (1,)), ((), ())), preferred_element_type=F32)


def _split_dot(x, m):
    hi = x.astype(BF16)
    lo = (x - hi.astype(F32)).astype(BF16)
    return _dot(hi, m) + _dot(lo, m)


class _CastStream:
    def __init__(self, items, n_steps, step_of):
        self.cols = [(c0, cw) for _, c0, cw in items]
        self.arrays = [a for a, _, _ in items]
        self.in_specs, self.out_specs, self.out_shapes = [], [], []
        for a, c0, cw in items:
            rows, width = a.shape
            chunks = n_steps
            while rows % chunks or (rows // chunks) % 16:
                assert chunks % 2 == 0, (a.shape, n_steps)
                chunks //= 2
            index = lambda *g, hold=n_steps // chunks: (step_of(*g) // hold, 0)
            if c0 == 0:
                width = cw
            self.in_specs.append(pl.BlockSpec((rows // chunks, width), index))
            self.out_specs.append(pl.BlockSpec((rows // chunks, cw), index))
            self.out_shapes.append(jax.ShapeDtypeStruct((rows, cw), BF16))

    def __len__(self):
        return len(self.cols)

    def run(self, refs):
        n = len(self.cols)
        for i_ref, o_ref, (c0, cw) in zip(refs[:n], refs[n:], self.cols):
            o_ref[...] = i_ref[:, c0:c0 + cw].astype(BF16)


def _sigmoid(x):
    return 0.5 * jnp.tanh(0.5 * x) + 0.5


def _ffn(x, g_ref, wg_ref, wu_ref, wd_ref):
    xn = _rms(x, g_ref[...]).astype(BF16)
    acc = None
    for f0, fw in FF_CHUNKS:
        hg = _dot(xn, wg_ref[:, f0:f0 + fw])
        hu = _dot(xn, wu_ref[:, f0:f0 + fw])
        a = (hg * _sigmoid(hg) * hu).astype(BF16)
        d = _dot(a, wd_ref[f0:f0 + fw, :])
        acc = d if acc is None else acc + d
    return x + 0.5 * acc


def _split3_dot_nt(a, b):
    a_hi = a.astype(BF16)
    a_lo = (a - a_hi.astype(F32)).astype(BF16)
    b_hi = b.astype(BF16)
    b_lo = (b - b_hi.astype(F32)).astype(BF16)
    return _dot_nt(a_hi, b_hi) + _dot_nt(a_hi, b_lo) + _dot_nt(a_lo, b_hi)


def _prep_kernel(are_ref, aim_ref, ldt_ref, bre_ref, bim_ref, cre_ref, cim_ref, *rest, casts):
    n = len(casts)
    pw_ref, k8_ref, w7_ref, gq_ref = rest[n:n + 4]
    casts.run(rest[:n] + rest[n + 4:])
    _prep_quarter(are_ref, aim_ref, ldt_ref, bre_ref, bim_ref, cre_ref, cim_ref,
                  pw_ref, k8_ref, w7_ref, gq_ref)


def _prep_quarter(are_ref, aim_ref, ldt_ref, bre_ref, bim_ref, cre_ref, cim_ref,
                  pw_ref, k8_ref, w7_ref, gq_ref):
    a_re = are_ref[...]
    a_im = aim_ref[...]
    dt = jnp.exp(ldt_ref[...])
    mag = jnp.exp(a_re * dt)
    ab_re = mag * jnp.cos(a_im * dt)
    ab_im = mag * jnp.sin(a_im * dt)
    inv = 1.0 / (a_re * a_re + a_im * a_im)
    f_re = ((ab_re - 1.0) * a_re + ab_im * a_im) * inv
    f_im = (ab_im * a_re - (ab_re - 1.0) * a_im) * inv
    b_re = bre_ref[...]
    b_im = bim_ref[...]
    bb_re = f_re * b_re - f_im * b_im
    bb_im = f_re * b_im + f_im * b_re
    c_re = cre_ref[...]
    c_im = cim_ref[...]
    same_group = (lax.broadcasted_iota(jnp.int32, (LANES, LANES), 0) // SSM_CH
                  == lax.broadcasted_iota(jnp.int32, (LANES, LANES), 1) // SSM_CH)
    spread = (lax.broadcasted_iota(jnp.int32, (SSM_P, QS), 1) % SSM_P
              == lax.broadcasted_iota(jnp.int32, (SSM_P, QS), 0)).astype(BF16)
    spread_t = (lax.broadcasted_iota(jnp.int32, (QS, SSM_P), 0) % SSM_P
                == lax.broadcasted_iota(jnp.int32, (QS, SSM_P), 1)).astype(BF16)
    mask_w = (lax.broadcasted_iota(jnp.int32, (LANES, QS), 0) // SSM_CH
              == lax.broadcasted_iota(jnp.int32, (LANES, QS), 1) // SSM_P)
    mask_g = (lax.broadcasted_iota(jnp.int32, (QS, LANES), 0) // SSM_P
              == lax.broadcasted_iota(jnp.int32, (QS, LANES), 1) // SSM_CH)
    p_re = jnp.ones_like(ab_re)
    p_im = jnp.zeros_like(ab_re)
    for i in range(BLK + 1):
        if i == 1:
            pw_ref[0, 0] = p_re
            pw_ref[0, 1] = p_im
        if i == BLK:
            pw_ref[1, 0] = p_re
            pw_ref[1, 1] = p_im
        if i >= 1:
            r = i - 1
            gcs = (c_re * p_re - c_im * p_im, -(c_re * p_im + c_im * p_re))
            for part in range(2):
                blk = jnp.where(mask_g, _dot_nt(spread_t, gcs[part].astype(BF16)), 0.0)
                gq_ref[0, part * QS:(part + 1) * QS, r * LANES:(r + 1) * LANES] = blk.astype(BF16)
        if i < BLK:
            r = BLK - 1 - i
            xs = (p_re * bb_re - p_im * bb_im, p_re * bb_im + p_im * bb_re)
            kk = jnp.where(same_group, _split3_dot_nt(xs[0], c_re) - _split3_dot_nt(xs[1], c_im), 0.0)
            k8_ref[0, :, i * LANES:(i + 1) * LANES] = kk.astype(BF16)
            for part in range(2):
                blk = jnp.where(mask_w, _dot(xs[part].astype(BF16), spread), 0.0)
                w7_ref[0, r * LANES:(r + 1) * LANES, part * QS:(part + 1) * QS] = blk.astype(BF16)
        p_re, p_im = p_re * ab_re - p_im * ab_im, p_re * ab_im + p_im * ab_re


def _ssm_prep(a_re, a_im, log_dt, b_re, b_im, c_re, c_im, cast_items):
    gc = SSM_GROUPS * SSM_CH
    casts = _CastStream(cast_items, NQ, lambda q: q)
    rep = lambda a: jnp.repeat(a.astype(F32), SSM_CH, axis=0)
    b_rows = lambda b: jnp.transpose(b.astype(F32), (0, 2, 1)).reshape(gc, SSM_P)
    c_rows = lambda c: c.astype(F32).reshape(gc, SSM_P)
    rows = lambda width: pl.BlockSpec((LANES, width), lambda q: (q, 0))
    per_q = lambda d1, d2: pl.BlockSpec((1, d1, d2), lambda q: (q, 0, 0))
    outs = pl.pallas_call(
        functools.partial(_prep_kernel, casts=casts),
        grid=(NQ,),
        in_specs=[rows(SSM_P), rows(SSM_P), rows(1)] + [rows(SSM_P)] * 4 + casts.in_specs,
        out_specs=(pl.BlockSpec((2, 2, LANES, SSM_P), lambda q: (0, 0, q, 0)),
                   per_q(LANES, BLK * LANES), per_q(BLK * LANES, 2 * QS), per_q(2 * QS, BLK * LANES))
                  + tuple(casts.out_specs),
        out_shape=(jax.ShapeDtypeStruct((2, 2, gc, SSM_P), F32),
                   jax.ShapeDtypeStruct((NQ, LANES, BLK * LANES), BF16),
                   jax.ShapeDtypeStruct((NQ, BLK * LANES, 2 * QS), BF16),
                   jax.ShapeDtypeStruct((NQ, 2 * QS, BLK * LANES), BF16)) + tuple(casts.out_shapes),
        compiler_params=_params(1),
        name="ssm_prep",
    )(rep(a_re), rep(a_im), rep(log_dt.reshape(SSM_GROUPS, 1)), b_rows(b_re), b_rows(b_im),
      c_rows(c_re), c_rows(c_im), *casts.arrays)
    return outs[:4], outs[4:]


def _head_norm(t, g, seg):
    sq = (t * t).astype(BF16)
    ms = jnp.concatenate(
        [_dot(sq[:, s * GROUP_W:(s + 1) * GROUP_W], seg) for s in range(len(GROUPS))], axis=1)
    return t * lax.rsqrt(ms + RMS_EPS) * g


def _store_attn_rows(val, out_ref, perm_ref, slab0):
    tm = val.shape[0]
    out_ref[:, :GROUP_W] = val[:, :GROUP_W].astype(BF16)
    for g in range(1, len(GROUPS)):
        dil = GROUPS[g][1]
        n = tm // dil
        for half in range(GROUP_W // LANES):
            c0 = g * GROUP_W + half * LANES
            col = val[:, c0:c0 + LANES]
            if perm_ref is None:
                out_ref[:, c0:c0 + LANES] = col.astype(BF16)
                continue
            slab = slab0 + (g - 1) * (GROUP_W // LANES) + half
            perm_ref[slab] = col
            for r in range(dil):
                out_ref[r * n:(r + 1) * n, c0:c0 + LANES] = (
                    perm_ref[slab, pl.ds(r, n, stride=dil), :].astype(BF16))


def _ffn_in_kernel(x_ref, g1_ref, wg_ref, wu_ref, wd_ref, gm_ref, win_ref, gq_ref, gk_ref, seg_ref,
                   x1_ref, u_ref, q_ref, k_ref, v_ref, *rest, tiles_per_seq, keeps):
    perm_ref = rest[-1] if len(rest) > 1 else None
    x1 = _ffn(x_ref[...], g1_ref, wg_ref, wu_ref, wd_ref)
    x1_ref[...] = x1
    h = _rms(x1, gm_ref[...]).astype(BF16)
    proj = _dot(h, win_ref[...])
    u_ref[...] = proj[:, :SSM_W]
    seg = seg_ref[...]
    q = _head_norm(proj[:, SSM_W:SSM_W + ATTN_W], gq_ref[...], seg)
    k = _head_norm(proj[:, SSM_W + ATTN_W:SSM_W + 2 * ATTN_W], gk_ref[...], seg)
    v = proj[:, SSM_W + 2 * ATTN_W:]
    slabs = (len(GROUPS) - 1) * (GROUP_W // LANES)
    _store_attn_rows(q * HEAD_DIM ** -0.5, q_ref, perm_ref, 0)
    _store_attn_rows(k, k_ref, perm_ref, slabs)
    _store_attn_rows(v, v_ref, perm_ref, 2 * slabs)
    if perm_ref is None:
        rest[0][:, :ATTN_W] = k
        rest[0][:, ATTN_W:] = v
        return
    tm = x1.shape[0]
    j = pl.program_id(0) % tiles_per_seq
    for g, keep in enumerate(keeps):
        lanes = slice(g * GROUP_W, (g + 1) * GROUP_W)
        cols = min(keep, tm)

        @pl.when(j >= tiles_per_seq - pl.cdiv(keep, tm))
        def _(g=g, lanes=lanes, cols=cols):
            rest[g][0, :GROUP_W, :] = k[:, lanes].T[:, tm - cols:]
            rest[g][0, GROUP_W:, :] = v[:, lanes].T[:, tm - cols:]


def _ffn_in(x, w, tm, tiles_per_seq, prompt_form):
    rows = x.shape[0]
    n_tiles = rows // tm
    n_seq = n_tiles // tiles_per_seq
    row_spec = lambda width: pl.BlockSpec((tm, width), lambda i: (i, 0))
    if prompt_form:
        keeps = tuple(min(window, tiles_per_seq * tm) for window, _ in GROUPS)
        assert all(keep % tm == 0 or keep < tm for keep in keeps)

        def tail_spec(keep):
            first = tiles_per_seq - pl.cdiv(keep, tm)
            return pl.BlockSpec((1, 2 * GROUP_W, min(keep, tm)),
                                lambda i: (i // tiles_per_seq, 0, jnp.maximum(i % tiles_per_seq - first, 0)))

        tail_shapes = tuple(jax.ShapeDtypeStruct((n_seq, 2 * GROUP_W, keep), F32) for keep in keeps)
        tail_specs = tuple(tail_spec(keep) for keep in keeps)
        scratch = [pltpu.VMEM((3 * (len(GROUPS) - 1) * (GROUP_W // LANES), tm, LANES), F32)]
    else:
        assert tiles_per_seq == 1
        keeps = ()
        tail_shapes = (jax.ShapeDtypeStruct((rows, 2 * ATTN_W), F32),)
        tail_specs = (row_spec(2 * ATTN_W),)
        scratch = []
    out_shape = (jax.ShapeDtypeStruct((rows, D_MODEL), F32),
                 jax.ShapeDtypeStruct((rows, SSM_W), F32),
                 jax.ShapeDtypeStruct((rows, ATTN_W), BF16),
                 jax.ShapeDtypeStruct((rows, ATTN_W), BF16),
                 jax.ShapeDtypeStruct((rows, ATTN_W), BF16)) + tail_shapes
    outs = pl.pallas_call(
        functools.partial(_ffn_in_kernel, tiles_per_seq=tiles_per_seq, keeps=keeps),
        grid=(n_tiles,),
        in_specs=[row_spec(D_MODEL), _const_spec((1, D_MODEL)),
                  _const_spec((D_MODEL, D_FF)), _const_spec((D_MODEL, D_FF)), _const_spec((D_FF, D_MODEL)),
                  _const_spec((1, D_MODEL)), _const_spec((D_MODEL, IN_A)),
                  _const_spec((1, ATTN_W)), _const_spec((1, ATTN_W)), _const_spec((GROUP_W, GROUP_W))],
        out_specs=(row_spec(D_MODEL), row_spec(SSM_W), row_spec(ATTN_W), row_spec(ATTN_W), row_spec(ATTN_W))
                  + tail_specs,
        out_shape=out_shape,
        scratch_shapes=scratch,
        compiler_params=_params(1),
        name="ffn_in",
    )(x, w["g_ffn1"], w["w1_gate"], w["w1_up"], w["w1_down"], w["g_mix"], w["w_in_a"],
      w["g_q"], w["g_k"], w["seg_mean"])
    return outs[:5], (list(outs[5:]) if prompt_form else outs[5])


def _glu_out(y, u, dsk_ref, wglu_ref, bglu_ref):
    y = jax.nn.gelu(y + dsk_ref[...] * u)
    z = _dot(y.astype(BF16), wglu_ref[...]) + bglu_ref[...]
    return (y * _sigmoid(z)).astype(BF16)


def _s5_kernel(u0_ref, u1_ref, u2_ref, u3_ref, are_ref, aim_ref, k8_ref, w7_ref, gq_ref,
               dsk_ref, wglu_ref, bglu_ref, y_ref, hre_ref, him_ref,
               lsp_ref, yq_ref, y2s_ref, sp_ref):
    u_refs = (u0_ref, u1_ref, u2_ref, u3_ref)
    nb, t_blk, _ = u0_ref.shape
    nblk = t_blk // BLK
    tiles = QS // LANES

    @pl.when(pl.program_id(0) == 0)
    def _():
        sp_ref[...] = jnp.zeros_like(sp_ref)

    sub = jnp.bitwise_and(lax.broadcasted_iota(jnp.int32, (t_blk // SUBLANES, SUBLANES, LANES), 1), BLK - 1)
    for q in range(NQ):
        ublks = []
        for b in range(nb):
            uq = u_refs[q][b]
            z3 = _dot(uq.astype(BF16), k8_ref[q]).reshape(t_blk // SUBLANES, SUBLANES, BLK * LANES)
            acc = z3[:, :, :LANES]
            for i in range(1, BLK):
                zi = z3[:, :, LANES * i:LANES * (i + 1)]
                acc = acc + jnp.where(sub >= i, pltpu.roll(zi, i, axis=1), 0.0)
            yq_ref[b * NQ + q] = acc.reshape(t_blk, LANES)
            ublks.append(jnp.concatenate(
                [u_refs[q][b, pl.ds(r, nblk, stride=BLK), :] for r in range(BLK)], axis=1))
        ls = _dot(jnp.concatenate(ublks, axis=0).astype(BF16), w7_ref[q])
        for b in range(nb):
            for t in range(2 * tiles):
                dst = pl.ds((q % 2) * tiles + t % tiles, nblk, stride=SUBLANES)
                lsp_ref[b, t // tiles, q // 2, dst, :] = ls[b * nblk:(b + 1) * nblk, LANES * t:LANES * (t + 1)]

    chains = [(b, half) for b in range(nb) for half in range(2)]
    coef = {half: (are_ref[half], aim_ref[half]) for half in range(2)}
    state = {c: (sp_ref[c[0], 0, c[1]], sp_ref[c[0], 1, c[1]]) for c in chains}
    for k in range(nblk):
        rows = slice(SUBLANES * k, SUBLANES * (k + 1))
        for b, half in chains:
            a_re, a_im = coef[half]
            s_re, s_im = state[b, half]
            l_re = lsp_ref[b, 0, half, rows, :]
            l_im = lsp_ref[b, 1, half, rows, :]
            lsp_ref[b, 0, half, rows, :] = s_re
            lsp_ref[b, 1, half, rows, :] = s_im
            state[b, half] = (a_re * s_re - a_im * s_im + l_re, a_re * s_im + a_im * s_re + l_im)
    for (b, half), (s_re, s_im) in state.items():
        sp_ref[b, 0, half] = s_re
        sp_ref[b, 1, half] = s_im
        for j in range(SUBLANES):
            lanes = slice((half * SUBLANES + j) * LANES, (half * SUBLANES + j + 1) * LANES)
            hre_ref[b:b + 1, lanes] = s_re[j:j + 1]
            him_ref[b:b + 1, lanes] = s_im[j:j + 1]

    for q in range(NQ):
        sc = jnp.concatenate(
            [jnp.concatenate(
                [lsp_ref[b, t // tiles, q // 2, pl.ds((q % 2) * tiles + t % tiles, nblk, stride=SUBLANES), :]
                 for t in range(2 * tiles)], axis=1) for b in range(nb)], axis=0)
        y2 = _dot(sc.astype(BF16), gq_ref[q])
        for b in range(nb):
            for r in range(BLK):
                y2s_ref[pl.ds(r, nblk, stride=BLK), :] = y2[b * nblk:(b + 1) * nblk, LANES * r:LANES * (r + 1)]
            yq_ref[b * NQ + q] = yq_ref[b * NQ + q] + y2s_ref[...]

    for b in range(nb):
        y = jnp.concatenate([yq_ref[b * NQ + q] for q in range(NQ)], axis=1)
        u = jnp.concatenate([u_refs[q][b] for q in range(NQ)], axis=1)
        y_ref[b] = _glu_out(y, u, dsk_ref, wglu_ref, bglu_ref)


def _s5(u, w, t_blk):
    nb, seq, _ = u.shape
    nblk = t_blk // BLK
    u_specs = [pl.BlockSpec((nb, t_blk, LANES), lambda c, q=q: (0, c, q)) for q in range(NQ)]
    state = pl.BlockSpec((nb, N_STATE), lambda c: (0, 0))
    packed = lambda a: a.reshape(2, SUBLANES, LANES)
    return pl.pallas_call(
        _s5_kernel,
        grid=(seq // t_blk,),
        in_specs=u_specs + [_const_spec((2, SUBLANES, LANES)), _const_spec((2, SUBLANES, LANES)),
                            _const_spec((NQ, LANES, BLK * LANES)),
                            _const_spec((NQ, BLK * LANES, 2 * QS)), _const_spec((NQ, 2 * QS, BLK * LANES)),
                            _const_spec((1, SSM_W)), _const_spec((SSM_W, SSM_W)), _const_spec((1, SSM_W))],
        out_specs=(pl.BlockSpec((nb, t_blk, SSM_W), lambda c: (0, c, 0)), state, state),
        out_shape=(jax.ShapeDtypeStruct((nb, seq, SSM_W), BF16),
                   jax.ShapeDtypeStruct((nb, N_STATE), F32), jax.ShapeDtypeStruct((nb, N_STATE), F32)),
        scratch_shapes=[pltpu.VMEM((nb, 2, 2, nblk * SUBLANES, LANES), F32),
                        pltpu.VMEM((nb * NQ, t_blk, LANES), F32),
                        pltpu.VMEM((t_blk, LANES), F32),
                        pltpu.VMEM((nb, 2, 2, SUBLANES, LANES), F32)],
        compiler_params=_params(1),
        name="s5",
    )(u, u, u, u, packed(w["a8_re"]), packed(w["a8_im"]), w["k8"], w["w7"], w["gq"],
      w["ssm_d"], w["w_glu"], w["b_glu"])


def _s5_step_kernel(u_ref, hre_ref, him_ref, are_ref, aim_ref, k8_ref, w7_ref, gq_ref,
                    dsk_ref, wglu_ref, bglu_ref, y_ref, nre_ref, nim_ref):
    u = u_ref[...]
    ys = []
    for q in range(NQ):
        ch = slice(q * QS, (q + 1) * QS)
        uq = u[:, q * LANES:(q + 1) * LANES].astype(BF16)
        x = _dot(uq, w7_ref[q, (BLK - 1) * LANES:, :])
        a_re, a_im = are_ref[:, ch], aim_ref[:, ch]
        s_re, s_im = hre_ref[:, ch], him_ref[:, ch]
        nre_ref[:, ch] = a_re * s_re - a_im * s_im + x[:, :QS]
        nim_ref[:, ch] = a_re * s_im + a_im * s_re + x[:, QS:]
        s = jnp.concatenate([s_re, s_im], axis=1).astype(BF16)
        ys.append(_dot(s, gq_ref[q, :, :LANES]) + _dot(uq, k8_ref[q, :, :LANES]))
    y_ref[...] = _glu_out(jnp.concatenate(ys, axis=1), u, dsk_ref, wglu_ref, bglu_ref)


def _s5_step(u, h_re, h_im, w):
    rows = u.shape[0]
    return pl.pallas_call(
        _s5_step_kernel,
        out_shape=(jax.ShapeDtypeStruct((rows, SSM_W), BF16),
                   jax.ShapeDtypeStruct((rows, N_STATE), F32), jax.ShapeDtypeStruct((rows, N_STATE), F32)),
        compiler_params=pltpu.CompilerParams(vmem_limit_bytes=VMEM_LIMIT),
        name="s5_step",
    )(u, h_re, h_im, w["ab_re"], w["ab_im"], w["k8"], w["w7"], w["gq"], w["ssm_d"], w["w_glu"], w["b_glu"])


def _step_attention(q, kn, vn, cache, dil):
    n_feat, window = cache.shape
    row8 = lax.broadcasted_iota(jnp.int32, (8, GROUP_W), 0)
    head8 = lax.broadcasted_iota(jnp.int32, (8, GROUP_W), 1) // HEAD_DIM == row8
    diag8 = (lax.broadcasted_iota(jnp.int32, (8, LANES), 0)
             == lax.broadcasted_iota(jnp.int32, (8, LANES), 1))
    q8 = jnp.where(head8, jnp.broadcast_to(q, (8, GROUP_W)), 0.0)
    pos = lax.broadcasted_iota(jnp.int32, (8, window), 1)
    s = _dot(q8.astype(BF16), cache[:GROUP_W].astype(BF16))
    s = jnp.where(jnp.bitwise_and(pos, dil - 1) == 0, s, NEG)
    s_new = jnp.sum(q8 * kn, axis=-1, keepdims=True)
    m = jnp.maximum(jnp.max(s, axis=-1, keepdims=True), s_new)
    p = jnp.exp(s - m)
    p_new = jnp.exp(s_new - m)
    den = jnp.sum(p, axis=-1, keepdims=True) + p_new
    o8 = (_dot_nt(p.astype(BF16), cache[GROUP_W:].astype(BF16)) + p_new * vn) / den
    o = jnp.sum(jnp.where(head8, o8, 0.0), axis=0, keepdims=True)
    lse = jnp.sum(jnp.where(diag8, m + jnp.log(den), 0.0), axis=0, keepdims=True)
    eye = (lax.broadcasted_iota(jnp.int32, (n_feat, n_feat), 0)
           == lax.broadcasted_iota(jnp.int32, (n_feat, n_feat), 1))
    new_col = jnp.sum(jnp.where(eye, jnp.concatenate([kn, vn], axis=1), 0.0), axis=-1, keepdims=True)
    rolled = pltpu.roll(cache, window - 1, axis=1)
    last = lax.broadcasted_iota(jnp.int32, (n_feat, window), 1) == window - 1
    return o, lse, jnp.where(last, new_col, rolled)


def _swa_kernel(q_ref, k_ref, v_ref, kp_ref, vp_ref, qs_ref, ks_ref, vs_ref, c_ref, *rest,
                qb, chunk, dil, casts):
    n = len(casts)
    o_ref, l_ref, os_ref, ls_ref, n_ref = rest[n:n + 5]
    casts.run(rest[:n] + rest[n + 5:])
    first = pl.program_id(2) == 0

    def rows(ref):
        return jnp.concatenate([ref[t] for t in range(ref.shape[0])], axis=0)

    q = rows(q_ref)
    kk = jnp.concatenate([rows(kp_ref), rows(k_ref)], axis=0)
    vv = jnp.concatenate([rows(vp_ref), rows(v_ref)], axis=0)
    nk = 2 * KEYS_PER_QUERY
    row = lax.broadcasted_iota(jnp.int32, (128, nk), 0)
    col = lax.broadcasted_iota(jnp.int32, (128, nk), 1)
    band = jnp.where(col >= row, jnp.where(col <= row + KEYS_PER_QUERY, 0.0, NEG), NEG)
    band_first = band + jnp.where(col < KEYS_PER_QUERY, jnp.where(first, NEG, 0.0), 0.0)
    lane = lax.broadcasted_iota(jnp.int32, (1, GROUP_W), 1) // HEAD_DIM
    lane_l = lax.broadcasted_iota(jnp.int32, (1, LANES), 1)
    for j in range(qb // 128):
        qj = q[128 * j:128 * (j + 1)]
        kj = kk[128 * j:128 * j + nk]
        vj = vv[128 * j:128 * j + nk]
        bias = band_first if j == 0 else band
        qs = jnp.concatenate([jnp.where(lane == h, qj, jnp.zeros_like(qj)) for h in range(HEADS)], axis=0)
        s = (_dot_nt(qs, kj).reshape(HEADS, 128, nk) + bias[None]).reshape(HEADS * 128, nk)
        m = jnp.max(s, axis=-1, keepdims=True)
        p = jnp.exp(s - m)
        den = jnp.sum(p, axis=-1, keepdims=True)
        on = _dot(p.astype(BF16), vj) * (1.0 / den)
        lse = m + jnp.log(den)
        o_acc = jnp.zeros((128, GROUP_W), F32)
        l_acc = jnp.zeros((128, LANES), F32)
        for h in range(HEADS):
            rows_h = slice(128 * h, 128 * (h + 1))
            o_acc = jnp.where(lane == h, on[rows_h], o_acc)
            l_acc = jnp.where(lane_l == h, lse[rows_h], l_acc)
        o_out = o_acc.astype(BF16)
        if chunk >= 128:
            t, r0 = divmod(128 * j, chunk)
            o_ref[t, r0:r0 + 128, :] = o_out
            l_ref[t, r0:r0 + 128, :] = l_acc
        else:
            for s_ in range(128 // chunk):
                t = (128 * j) // chunk + s_
                o_ref[t] = o_out[s_ * chunk:(s_ + 1) * chunk]
                l_ref[t] = l_acc[s_ * chunk:(s_ + 1) * chunk]

    def sample_step():
        o, lse, new_cache = _step_attention(qs_ref[0].astype(F32), ks_ref[0], vs_ref[0], c_ref[0], dil)
        os_ref[0] = o.astype(BF16)
        ls_ref[0] = lse
        n_ref[0] = new_cache

    if c_ref.shape[-1] >= OWN_REGION_WINDOW:
        pl.when(pl.program_id(0) >= 0)(sample_step)
    else:
        sample_step()


def _swa(q, k, v, g, tile, q_s, kv_s, cache, cast_items):
    window, dil = GROUPS[g]
    nb, seq, _ = q.shape
    n_sample = q_s.shape[0]
    n_tiles = seq // tile
    chunk = tile // dil
    n = seq // dil
    qb = min(512, n)
    tpb = qb // chunk
    n_i = n // qb
    assert nb * dil * n_i >= n_sample and cache.shape == (n_sample, 2 * GROUP_W, window)
    view = lambda a: a.reshape(nb, n_tiles, dil, chunk, a.shape[-1])
    cur = pl.BlockSpec((None, tpb, None, chunk, GROUP_W), lambda b, r, i: (b, i, r, 0, g))
    if chunk >= 128:
        prev = pl.BlockSpec((None, 1, None, 128, GROUP_W),
                            lambda b, r, i: (b, jnp.maximum(i * tpb - 1, 0), r, chunk // 128 - 1, g))
    else:
        ptiles = 128 // chunk
        prev = pl.BlockSpec((None, ptiles, None, chunk, GROUP_W),
                            lambda b, r, i: (b, jnp.maximum(i * (tpb // ptiles) - 1, 0), r, 0, g))
    out_block = lambda width: pl.BlockSpec((None, tpb, None, chunk, width), lambda b, r, i: (b, i, r, 0, 0))
    step_of = lambda b, r, i: (b * dil + r) * n_i + i
    sample_of = lambda b, r, i: jnp.minimum(step_of(b, r, i), n_sample - 1)
    sample_block = lambda *dims: pl.BlockSpec((1,) + dims, lambda b, r, i: (sample_of(b, r, i),) + (0,) * len(dims))
    group_lanes = lambda col: pl.BlockSpec((1, 1, GROUP_W), lambda b, r, i: (sample_of(b, r, i), 0, col))
    casts = _CastStream(cast_items, nb * dil * n_i, step_of)
    o, lse, o_s, l_s, new_cache, *cast_outs = pl.pallas_call(
        functools.partial(_swa_kernel, qb=qb, chunk=chunk, dil=dil, casts=casts),
        grid=(nb, dil, n_i),
        in_specs=[cur, cur, cur, prev, prev,
                  group_lanes(g), group_lanes(g), group_lanes(ATTN_W // GROUP_W + g),
                  sample_block(2 * GROUP_W, window)] + casts.in_specs,
        out_specs=(out_block(GROUP_W), out_block(LANES),
                   sample_block(1, GROUP_W), sample_block(1, LANES), sample_block(2 * GROUP_W, window))
                  + tuple(casts.out_specs),
        out_shape=(jax.ShapeDtypeStruct((nb, n_tiles, dil, chunk, GROUP_W), BF16),
                   jax.ShapeDtypeStruct((nb, n_tiles, dil, chunk, LANES), F32),
                   jax.ShapeDtypeStruct((n_sample, 1, GROUP_W), BF16),
                   jax.ShapeDtypeStruct((n_sample, 1, LANES), F32),
                   jax.ShapeDtypeStruct(cache.shape, F32)) + tuple(casts.out_shapes),
        compiler_params=_params(3),
        name="swa_g%d" % g,
    )(view(q), view(k), view(v), view(k), view(v),
      q_s.reshape(n_sample, 1, ATTN_W), kv_s.reshape(n_sample, 1, 2 * ATTN_W),
      kv_s.reshape(n_sample, 1, 2 * ATTN_W), cache, *casts.arrays)
    return (o.reshape(nb * seq, GROUP_W), lse.reshape(nb * seq, LANES),
            o_s.reshape(n_sample, GROUP_W), l_s.reshape(n_sample, LANES), new_cache, cast_outs)


def _position_order(o_ref, l_ref, dil, nat_ref, slab0):
    tm = o_ref.shape[0]
    n = tm // dil
    for r in range(dil):
        rows = slice(r * n, (r + 1) * n)
        dst = pl.ds(r, n, stride=dil)
        nat_ref[slab0, dst, :] = o_ref[rows, :LANES].astype(F32)
        nat_ref[slab0 + 1, dst, :] = o_ref[rows, LANES:].astype(F32)
        nat_ref[slab0 + 2, dst, :] = l_ref[rows, :]
    o = jnp.concatenate([nat_ref[slab0], nat_ref[slab0 + 1]], axis=1)
    return o, nat_ref[slab0 + 2]


def _mix_ffn_kernel(x1_ref, ys_ref, o0_ref, o1_ref, o2_ref, l0_ref, l1_ref, l2_ref,
                    gm_ref, wgt_ref, wsp_ref, wap_ref, wo_ref, seget_ref,
                    g2_ref, wg_ref, wu_ref, wd_ref, y_ref, *scratch):
    x1 = x1_ref[...]
    h = _rms(x1, gm_ref[...]).astype(BF16)
    gates = _sigmoid(_dot(h, wgt_ref[...]))
    os_ = [o0_ref[...].astype(F32)]
    ls = [l0_ref[...]]
    for g, (o_ref, l_ref) in enumerate(((o1_ref, l1_ref), (o2_ref, l2_ref)), start=1):
        if scratch:
            o, l = _position_order(o_ref, l_ref, GROUPS[g][1], scratch[0], 3 * (g - 1))
        else:
            o, l = o_ref[...].astype(F32), l_ref[...]
        os_.append(o)
        ls.append(l)
    l_top = jnp.maximum(jnp.maximum(ls[0], ls[1]), ls[2])
    es = [jnp.exp(l - l_top) for l in ls]
    inv = 1.0 / (es[0] + es[1] + es[2])
    seget = seget_ref[...]
    y_attn = None
    for e, o in zip(es, os_):
        t = _split_dot(e * inv, seget) * o
        y_attn = t if y_attn is None else y_attn + t
    mixed = (gates[:, :D_MODEL] * _dot(ys_ref[...], wsp_ref[...])
             + gates[:, D_MODEL:] * _dot(y_attn.astype(BF16), wap_ref[...]))
    x2 = x1 + _dot(mixed.astype(BF16), wo_ref[...])
    y_ref[...] = _ffn(x2, g2_ref, wg_ref, wu_ref, wd_ref)


def _mix_ffn(x1, y_ssm, os_, ls_, w, tm, residue_major):
    rows = x1.shape[0]
    row_spec = lambda width: pl.BlockSpec((tm, width), lambda i: (i, 0))
    n_slabs = 3 * (len(GROUPS) - 1)
    scratch = [pltpu.VMEM((n_slabs, tm, LANES), F32)] if residue_major else []
    return pl.pallas_call(
        _mix_ffn_kernel,
        grid=(rows // tm,),
        in_specs=[row_spec(D_MODEL), row_spec(SSM_W)] + [row_spec(GROUP_W)] * 3 + [row_spec(LANES)] * 3
                 + [_const_spec((1, D_MODEL)), _const_spec((D_MODEL, 2 * D_MODEL)),
                    _const_spec((SSM_W, D_MODEL)), _const_spec((GROUP_W, D_MODEL)),
                    _const_spec((D_MODEL, D_MODEL)), _const_spec((LANES, GROUP_W)),
                    _const_spec((1, D_MODEL)), _const_spec((D_MODEL, D_FF)), _const_spec((D_MODEL, D_FF)),
                    _const_spec((D_FF, D_MODEL))],
        out_specs=row_spec(D_MODEL),
        out_shape=jax.ShapeDtypeStruct((rows, D_MODEL), F32),
        scratch_shapes=scratch,
        compiler_params=_params(1),
        name="mix_ffn",
    )(x1, y_ssm, *os_, *ls_, w["g_mix"], w["w_gates"], w["w_ssm_proj"], w["w_attn_proj"], w["w_o"],
      w["seg_et"], w["g_ffn2"], w["w2_gate"], w["w2_up"], w["w2_down"])


def _prepare_weights(g_ffn1, w1_gate, w1_up, w1_down, g_mix, w_in, g_q, g_k,
                     ssm_a_re, ssm_a_im, ssm_log_dt, ssm_b_re, ssm_b_im, ssm_c_re, ssm_c_im,
                     ssm_d, w_glu, b_glu, w_ssm_proj, w_attn_proj, w_o, g_ffn2, w2_gate, w2_up, w2_down):
    w = {}
    row = lambda a: a.reshape(1, -1).astype(F32)
    w["g_ffn1"], w["g_mix"], w["g_ffn2"] = row(g_ffn1), row(g_mix), row(g_ffn2)
    w["w_glu"] = w_glu.astype(BF16)
    f32 = lambda a: a.astype(F32)
    w["late"] = (
        (("w2_gate", f32(w2_gate), 0, D_FF), ("w2_up", f32(w2_up), 0, D_FF)),
        (("w2_down", f32(w2_down), 0, D_MODEL), ("w_o", f32(w_o), 0, D_MODEL),
         ("w_ssm_proj", f32(w_ssm_proj), 0, D_MODEL), ("w_attn_proj", f32(w_attn_proj), 0, D_MODEL),
         ("w_gates", f32(w_in), IN_A, 2 * D_MODEL)),
        ())
    per_head = lambda g: jnp.broadcast_to(g[:, None, :], (len(GROUPS), HEADS, HEAD_DIM)).reshape(1, ATTN_W)
    w["g_q"], w["g_k"] = per_head(g_q.astype(F32)), per_head(g_k.astype(F32))
    head_of_lane = jnp.arange(GROUP_W) // HEAD_DIM
    w["seg_mean"] = ((head_of_lane[:, None] == head_of_lane[None, :]) / HEAD_DIM).astype(BF16)
    w["seg_et"] = (jnp.arange(LANES)[:, None] == head_of_lane[None, :]).astype(BF16)
    w["ssm_d"], w["b_glu"] = row(ssm_d), row(b_glu)

    early = (("w1_gate", f32(w1_gate), 0, D_FF), ("w1_up", f32(w1_up), 0, D_FF),
             ("w1_down", f32(w1_down), 0, D_MODEL), ("w_in_a", f32(w_in), 0, IN_A))
    (pw, w["k8"], w["w7"], w["gq"]), converted = _ssm_prep(
        ssm_a_re, ssm_a_im, ssm_log_dt, ssm_b_re, ssm_b_im, ssm_c_re, ssm_c_im, [e[1:] for e in early])
    w.update({e[0]: c for e, c in zip(early, converted)})
    per_group = lambda a: a.reshape(SSM_GROUPS, SSM_CH, SSM_P)[:, 0].reshape(1, N_STATE)
    w["ab_re"], w["ab_im"] = per_group(pw[0, 0]), per_group(pw[0, 1])
    w["a8_re"], w["a8_im"] = per_group(pw[1, 0]), per_group(pw[1, 1])
    return w


def _forward(x_p, x_s, state_s, caches_s, w):
    nb, seq, _ = x_p.shape
    n_s = x_s.shape[0]
    assert seq % TILE == 0 and x_s.shape[1] == 1
    (x1, u, q, k, v), caches_p = _ffn_in(x_p.reshape(nb * seq, D_MODEL), w, TILE, seq // TILE, True)
    (x1_s, u_s, q_s, _, _), kv_s = _ffn_in(x_s.reshape(n_s, D_MODEL), w, n_s, 1, False)
    y_ssm, *h_p = _s5(u.reshape(nb, seq, SSM_W), w, min(S5_T, seq))
    y_ssm_s, *h_s = _s5_step(u_s, state_s[0], state_s[1], w)
    q3, k3, v3 = (a.reshape(nb, seq, ATTN_W) for a in (q, k, v))
    os_p, ls_p, os_s, ls_s, new_caches_s = [], [], [], [], []
    for g in range(len(GROUPS)):
        late = w["late"][g]
        o, l, o_s, l_s, c_new, converted = _swa(q3, k3, v3, g, TILE, q_s, kv_s, caches_s[g],
                                                [e[1:] for e in late])
        w = {**w, **{e[0]: c for e, c in zip(late, converted)}}
        for lst, a in zip((os_p, ls_p, os_s, ls_s, new_caches_s), (o, l, o_s, l_s, c_new)):
            lst.append(a)
    y_p = _mix_ffn(x1, y_ssm.reshape(nb * seq, SSM_W), os_p, ls_p, w, TILE, True)
    y_s = _mix_ffn(x1_s, y_ssm_s, os_s, ls_s, w, n_s, False)
    return (y_p.reshape(nb, seq, D_MODEL), y_s.reshape(n_s, 1, D_MODEL), h_p, h_s, caches_p, new_caches_s)


def _to_features_major(c):
    nb, win = c.shape[:2]
    return jnp.transpose(c, (0, 2, 3, 4, 1)).reshape(nb, 2 * GROUP_W, win)


def _to_window_buffer(c):
    nb, _, win = c.shape
    return jnp.transpose(c.reshape(nb, 2, HEADS, HEAD_DIM, win), (0, 4, 1, 2, 3))[None]


def kernel(x_prompt, x_sample, cache_kv_w128, cache_kv_w512, cache_kv_w2048, state_ssm_re, state_ssm_im, g_ffn1, w1_gate, w1_up, w1_down, g_mix, w_in, g_q, g_k, ssm_a_re, ssm_a_im, ssm_log_dt, ssm_b_re, ssm_b_im, ssm_c_re, ssm_c_im, ssm_d, w_glu, b_glu, w_ssm_proj, w_attn_proj, w_o, g_ffn2, w2_gate, w2_up, w2_down):
    layer_weights = (g_ffn1, w1_gate, w1_up, w1_down, g_mix, w_in, g_q, g_k,
                     ssm_a_re, ssm_a_im, ssm_log_dt, ssm_b_re, ssm_b_im, ssm_c_re, ssm_c_im,
                     ssm_d, w_glu, b_glu, w_ssm_proj, w_attn_proj, w_o, g_ffn2, w2_gate, w2_up, w2_down)
    depth = g_ffn1.shape[0]
    assert depth == 1, "window caches of deeper layers would need the previous layer's outputs"
    w = _prepare_weights(*(a[0] for a in layer_weights))
    nb_s = x_sample.shape[0]
    sdt = state_ssm_re.dtype
    as_state = lambda h: h.reshape(1, -1, SSM_GROUPS, SSM_P).astype(sdt)

    caches = [_to_features_major(c[0]) for c in (cache_kv_w128, cache_kv_w512, cache_kv_w2048)]
    state_s = (state_ssm_re[0].reshape(nb_s, N_STATE).astype(F32),
               state_ssm_im[0].reshape(nb_s, N_STATE).astype(F32))
    y_p, y_s, h_p, h_s, kv_p, kv_s = _forward(x_prompt, x_sample, state_s, caches, w)
    return ((y_p, y_s) + tuple(_to_window_buffer(c) for c in kv_p) + (as_state(h_p[0]), as_state(h_p[1]))
            + tuple(_to_window_buffer(c) for c in kv_s) + (as_state(h_s[0]), as_state(h_s[1])))
```

```python
import functools

import jax
import jax.numpy as jnp
from jax import lax
from jax.experimental import pallas as pl
from jax.experimental.pallas import tpu as pltpu

F32 = jnp.float32
BF16 = jnp.bfloat16

D_MODEL = 1024
D_FF = 2816
HEAD_DIM = 64
HEADS = 4
GROUPS = ((128, 1), (512, 4), (2048, 16))
KEYS_PER_QUERY = 128
GROUP_W = HEADS * HEAD_DIM
ATTN_W = len(GROUPS) * GROUP_W
SSM_W = 512
SSM_GROUPS = 32
SSM_CH = 16
SSM_P = 64
N_STATE = SSM_GROUPS * SSM_P
IN_A = SSM_W + 3 * ATTN_W
RMS_EPS = 1e-6
NEG = -1e30
LANES = 128
SUBLANES = 8

TILE = 512
FF_CHUNKS = ((0, 1024), (1024, 1024), (2048, 768))
BLK = 4
NQ = SSM_W // LANES
QS = N_STATE // NQ
S5_T = 1024
OWN_REGION_WINDOW = 2048
VMEM_LIMIT = 56 * 1024 * 1024


def _const_spec(shape):
    nd = len(shape)
    return pl.BlockSpec(shape, lambda *_: (0,) * nd, pipeline_mode=pl.Buffered(1))


def _params(n_grid):
    return pltpu.CompilerParams(dimension_semantics=("arbitrary",) * n_grid,
                                vmem_limit_bytes=VMEM_LIMIT)


def _rms(x, g):
    ms = jnp.mean(x * x, axis=-1, keepdims=True)
    return x * lax.rsqrt(ms + RMS_EPS) * g


def _dot(a, b):
    return jnp.dot(a, b, preferred_element_type=F32)


def _dot_nt(a, b):
    return lax.dot_general(a, b, (((1,), (1,)), ((), ())), preferred_element_type=F32)


class _CastStream:
    def __init__(self, items, n_steps, step_of):
        self.cols = [(c0, cw) for _, c0, cw in items]
        self.arrays = [a for a, _, _ in items]
        self.in_specs, self.out_specs, self.out_shapes = [], [], []
        for a, c0, cw in items:
            rows, width = a.shape
            chunks = n_steps
            while rows % chunks or (rows // chunks) % 16:
                assert chunks % 2 == 0, (a.shape, n_steps)
                chunks //= 2
            index = lambda *g, hold=n_steps // chunks: (step_of(*g) // hold, 0)
            if c0 == 0:
                width = cw
            self.in_specs.append(pl.BlockSpec((rows // chunks, width), index))
            self.out_specs.append(pl.BlockSpec((rows // chunks, cw), index))
            self.out_shapes.append(jax.ShapeDtypeStruct((rows, cw), BF16))

    def __len__(self):
        return len(self.cols)

    def run(self, refs):
        n = len(self.cols)
        for i_ref, o_ref, (c0, cw) in zip(refs[:n], refs[n:], self.cols):
            o_ref[...] = i_ref[:, c0:c0 + cw].astype(BF16)


def _sigmoid(x):
    return 0.5 * jnp.tanh(0.5 * x) + 0.5


def _ffn(x, g_ref, wg_ref, wu_ref, wd_ref):
    xn = _rms(x, g_ref[...]).astype(BF16)
    acc = None
    for f0, fw in FF_CHUNKS:
        hg = _dot(xn, wg_ref[:, f0:f0 + fw])
        hu = _dot(xn, wu_ref[:, f0:f0 + fw])
        a = (hg * _sigmoid(hg) * hu).astype(BF16)
        d = _dot(a, wd_ref[f0:f0 + fw, :])
        acc = d if acc is None else acc + d
    return x + 0.5 * acc


def _split3_dot_nt(a, b):
    a_hi = a.astype(BF16)
    a_lo = (a - a_hi.astype(F32)).astype(BF16)
    b_hi = b.astype(BF16)
    b_lo = (b - b_hi.astype(F32)).astype(BF16)
    return _dot_nt(a_hi, b_hi) + _dot_nt(a_hi, b_lo) + _dot_nt(a_lo, b_hi)


def _prep_kernel(are_ref, aim_ref, ldt_ref, bre_ref, bim_ref, cre_ref, cim_ref, *rest, casts):
    n = len(casts)
    pw_ref, k8_ref, w7_ref, gq_ref = rest[n:n + 4]
    casts.run(rest[:n] + rest[n + 4:])
    _prep_quarter(are_ref, aim_ref, ldt_ref, bre_ref, bim_ref, cre_ref, cim_ref,
                  pw_ref, k8_ref, w7_ref, gq_ref)


def _prep_quarter(are_ref, aim_ref, ldt_ref, bre_ref, bim_ref, cre_ref, cim_ref,
                  pw_ref, k8_ref, w7_ref, gq_ref):
    a_re = are_ref[...]
    a_im = aim_ref[...]
    dt = jnp.exp(ldt_ref[...])
    mag = jnp.exp(a_re * dt)
    ab_re = mag * jnp.cos(a_im * dt)
    ab_im = mag * jnp.sin(a_im * dt)
    inv = 1.0 / (a_re * a_re + a_im * a_im)
    f_re = ((ab_re - 1.0) * a_re + ab_im * a_im) * inv
    f_im = (ab_im * a_re - (ab_re - 1.0) * a_im) * inv
    b_re = bre_ref[...]
    b_im = bim_ref[...]
    bb_re = f_re * b_re - f_im * b_im
    bb_im = f_re * b_im + f_im * b_re
    c_re = cre_ref[...]
    c_im = cim_ref[...]
    same_group = (lax.broadcasted_iota(jnp.int32, (LANES, LANES), 0) // SSM_CH
                  == lax.broadcasted_iota(jnp.int32, (LANES, LANES), 1) // SSM_CH)
    spread = (lax.broadcasted_iota(jnp.int32, (SSM_P, QS), 1) % SSM_P
              == lax.broadcasted_iota(jnp.int32, (SSM_P, QS), 0)).astype(BF16)
    spread_t = (lax.broadcasted_iota(jnp.int32, (QS, SSM_P), 0) % SSM_P
                == lax.broadcasted_iota(jnp.int32, (QS, SSM_P), 1)).astype(BF16)
    mask_w = (lax.broadcasted_iota(jnp.int32, (LANES, QS), 0) // SSM_CH
              == lax.broadcasted_iota(jnp.int32, (LANES, QS), 1) // SSM_P)
    mask_g = (lax.broadcasted_iota(jnp.int32, (QS, LANES), 0) // SSM_P
              == lax.broadcasted_iota(jnp.int32, (QS, LANES), 1) // SSM_CH)
    p_re = jnp.ones_like(ab_re)
    p_im = jnp.zeros_like(ab_re)
    for i in range(BLK + 1):
        if i == 1:
            pw_ref[0, 0] = p_re
            pw_ref[0, 1] = p_im
        if i == BLK:
            pw_ref[1, 0] = p_re
            pw_ref[1, 1] = p_im
        if i >= 1:
            r = i - 1
            gcs = (c_re * p_re - c_im * p_im, -(c_re * p_im + c_im * p_re))
            for part in range(2):
                blk = jnp.where(mask_g, _dot_nt(spread_t, gcs[part].astype(BF16)), 0.0)
                gq_ref[0, part * QS:(part + 1) * QS, r * LANES:(r + 1) * LANES] = blk.astype(BF16)
        if i < BLK:
            r = BLK - 1 - i
            xs = (p_re * bb_re - p_im * bb_im, p_re * bb_im + p_im * bb_re)
            kk = jnp.where(same_group, _split3_dot_nt(xs[0], c_re) - _split3_dot_nt(xs[1], c_im), 0.0)
            k8_ref[0, :, i * LANES:(i + 1) * LANES] = kk.astype(BF16)
            for part in range(2):
                blk = jnp.where(mask_w, _dot(xs[part].astype(BF16), spread), 0.0)
                w7_ref[0, r * LANES:(r + 1) * LANES, part * QS:(part + 1) * QS] = blk.astype(BF16)
        p_re, p_im = p_re * ab_re - p_im * ab_im, p_re * ab_im + p_im * ab_re


def _ssm_prep(a_re, a_im, log_dt, b_re, b_im, c_re, c_im, cast_items):
    gc = SSM_GROUPS * SSM_CH
    casts = _CastStream(cast_items, NQ, lambda q: q)
    rep = lambda a: jnp.repeat(a.astype(F32), SSM_CH, axis=0)
    b_rows = lambda b: jnp.transpose(b.astype(F32), (0, 2, 1)).reshape(gc, SSM_P)
    c_rows = lambda c: c.astype(F32).reshape(gc, SSM_P)
    rows = lambda width: pl.BlockSpec((LANES, width), lambda q: (q, 0))
    per_q = lambda d1, d2: pl.BlockSpec((1, d1, d2), lambda q: (q, 0, 0))
    outs = pl.pallas_call(
        functools.partial(_prep_kernel, casts=casts),
        grid=(NQ,),
        in_specs=[rows(SSM_P), rows(SSM_P), rows(1)] + [rows(SSM_P)] * 4 + casts.in_specs,
        out_specs=(pl.BlockSpec((2, 2, LANES, SSM_P), lambda q: (0, 0, q, 0)),
                   per_q(LANES, BLK * LANES), per_q(BLK * LANES, 2 * QS), per_q(2 * QS, BLK * LANES))
                  + tuple(casts.out_specs),
        out_shape=(jax.ShapeDtypeStruct((2, 2, gc, SSM_P), F32),
                   jax.ShapeDtypeStruct((NQ, LANES, BLK * LANES), BF16),
                   jax.ShapeDtypeStruct((NQ, BLK * LANES, 2 * QS), BF16),
                   jax.ShapeDtypeStruct((NQ, 2 * QS, BLK * LANES), BF16)) + tuple(casts.out_shapes),
        compiler_params=_params(1),
        name="ssm_prep",
    )(rep(a_re), rep(a_im), rep(log_dt.reshape(SSM_GROUPS, 1)), b_rows(b_re), b_rows(b_im),
      c_rows(c_re), c_rows(c_im), *casts.arrays)
    return outs[:4], outs[4:]


def _head_norm(t, g, seg):
    sq = (t * t).astype(BF16)
    ms = jnp.concatenate(
        [_dot(sq[:, s * GROUP_W:(s + 1) * GROUP_W], seg) for s in range(len(GROUPS))], axis=1)
    return t * lax.rsqrt(ms + RMS_EPS) * g


def _store_attn_rows(val, out_ref, perm_ref, slab0):
    tm = val.shape[0]
    out_ref[:, :GROUP_W] = val[:, :GROUP_W].astype(BF16)
    for g in range(1, len(GROUPS)):
        dil = GROUPS[g][1]
        n = tm // dil
        for half in range(GROUP_W // LANES):
            c0 = g * GROUP_W + half * LANES
            col = val[:, c0:c0 + LANES]
            if perm_ref is None:
                out_ref[:, c0:c0 + LANES] = col.astype(BF16)
                continue
            slab = slab0 + (g - 1) * (GROUP_W // LANES) + half
            perm_ref[slab] = col
            for r in range(dil):
                out_ref[r * n:(r + 1) * n, c0:c0 + LANES] = (
                    perm_ref[slab, pl.ds(r, n, stride=dil), :].astype(BF16))


def _ffn_in_kernel(x_ref, g1_ref, wg_ref, wu_ref, wd_ref, gm_ref, win_ref, gq_ref, gk_ref, seg_ref,
                   x1_ref, u_ref, q_ref, k_ref, v_ref, *rest, tiles_per_seq, keeps):
    perm_ref = rest[-1] if len(rest) > 1 else None
    x1 = _ffn(x_ref[...], g1_ref, wg_ref, wu_ref, wd_ref)
    x1_ref[...] = x1
    h = _rms(x1, gm_ref[...]).astype(BF16)
    proj = _dot(h, win_ref[...])
    u_ref[...] = proj[:, :SSM_W]
    seg = seg_ref[...]
    q = _head_norm(proj[:, SSM_W:SSM_W + ATTN_W], gq_ref[...], seg)
    k = _head_norm(proj[:, SSM_W + ATTN_W:SSM_W + 2 * ATTN_W], gk_ref[...], seg)
    v = proj[:, SSM_W + 2 * ATTN_W:]
    slabs = (len(GROUPS) - 1) * (GROUP_W // LANES)
    _store_attn_rows(q * HEAD_DIM ** -0.5, q_ref, perm_ref, 0)
    _store_attn_rows(k, k_ref, perm_ref, slabs)
    _store_attn_rows(v, v_ref, perm_ref, 2 * slabs)
    if perm_ref is None:
        rest[0][:, :ATTN_W] = k
        rest[0][:, ATTN_W:] = v
        return
    tm = x1.shape[0]
    j = pl.program_id(0) % tiles_per_seq
    for g, keep in enumerate(keeps):
        lanes = slice(g * GROUP_W, (g + 1) * GROUP_W)
        cols = min(keep, tm)

        @pl.when(j >= tiles_per_seq - pl.cdiv(keep, tm))
        def _(g=g, lanes=lanes, cols=cols):
            rest[g][0, :GROUP_W, :] = k[:, lanes].T[:, tm - cols:]
            rest[g][0, GROUP_W:, :] = v[:, lanes].T[:, tm - cols:]


def _ffn_in(x, w, tm, tiles_per_seq, prompt_form):
    rows = x.shape[0]
    n_tiles = rows // tm
    n_seq = n_tiles // tiles_per_seq
    row_spec = lambda width: pl.BlockSpec((tm, width), lambda i: (i, 0))
    if prompt_form:
        keeps = tuple(min(window, tiles_per_seq * tm) for window, _ in GROUPS)
        assert all(keep % tm == 0 or keep < tm for keep in keeps)

        def tail_spec(keep):
            first = tiles_per_seq - pl.cdiv(keep, tm)
            return pl.BlockSpec((1, 2 * GROUP_W, min(keep, tm)),
                                lambda i: (i // tiles_per_seq, 0, jnp.maximum(i % tiles_per_seq - first, 0)))

        tail_shapes = tuple(jax.ShapeDtypeStruct((n_seq, 2 * GROUP_W, keep), F32) for keep in keeps)
        tail_specs = tuple(tail_spec(keep) for keep in keeps)
        scratch = [pltpu.VMEM((3 * (len(GROUPS) - 1) * (GROUP_W // LANES), tm, LANES), F32)]
    else:
        assert tiles_per_seq == 1
        keeps = ()
        tail_shapes = (jax.ShapeDtypeStruct((rows, 2 * ATTN_W), F32),)
        tail_specs = (row_spec(2 * ATTN_W),)
        scratch = []
    out_shape = (jax.ShapeDtypeStruct((rows, D_MODEL), F32),
                 jax.ShapeDtypeStruct((rows, SSM_W), F32),
                 jax.ShapeDtypeStruct((rows, ATTN_W), BF16),
                 jax.ShapeDtypeStruct((rows, ATTN_W), BF16),
                 jax.ShapeDtypeStruct((rows, ATTN_W), BF16)) + tail_shapes
    outs = pl.pallas_call(
        functools.partial(_ffn_in_kernel, tiles_per_seq=tiles_per_seq, keeps=keeps),
        grid=(n_tiles,),
        in_specs=[row_spec(D_MODEL), _const_spec((1, D_MODEL)),
                  _const_spec((D_MODEL, D_FF)), _const_spec((D_MODEL, D_FF)), _const_spec((D_FF, D_MODEL)),
                  _const_spec((1, D_MODEL)), _const_spec((D_MODEL, IN_A)),
                  _const_spec((1, ATTN_W)), _const_spec((1, ATTN_W)), _const_spec((GROUP_W, GROUP_W))],
        out_specs=(row_spec(D_MODEL), row_spec(SSM_W), row_spec(ATTN_W), row_spec(ATTN_W), row_spec(ATTN_W))
                  + tail_specs,
        out_shape=out_shape,
        scratch_shapes=scratch,
        compiler_params=_params(1),
        name="ffn_in",
    )(x, w["g_ffn1"], w["w1_gate"], w["w1_up"], w["w1_down"], w["g_mix"], w["w_in_a"],
      w["g_q"], w["g_k"], w["seg_mean"])
    return outs[:5], (list(outs[5:]) if prompt_form else outs[5])


def _glu_out(y, u, dsk_ref, wglu_ref, bglu_ref):
    y = jax.nn.gelu(y + dsk_ref[...] * u)
    z = _dot(y.astype(BF16), wglu_ref[...]) + bglu_ref[...]
    return (y * _sigmoid(z)).astype(BF16)


def _s5_kernel(u0_ref, u1_ref, u2_ref, u3_ref, are_ref, aim_ref, k8_ref, w7_ref, gq_ref,
               dsk_ref, wglu_ref, bglu_ref, y_ref, hre_ref, him_ref,
               lsp_ref, yq_ref, y2s_ref, sp_ref):
    u_refs = (u0_ref, u1_ref, u2_ref, u3_ref)
    nb, t_blk, _ = u0_ref.shape
    nblk = t_blk // BLK
    tiles = QS // LANES

    @pl.when(pl.program_id(0) == 0)
    def _():
        sp_ref[...] = jnp.zeros_like(sp_ref)

    sub = jnp.bitwise_and(lax.broadcasted_iota(jnp.int32, (t_blk // SUBLANES, SUBLANES, LANES), 1), BLK - 1)
    for q in range(NQ):
        ublks = []
        for b in range(nb):
            uq = u_refs[q][b]
            z3 = _dot(uq.astype(BF16), k8_ref[q]).reshape(t_blk // SUBLANES, SUBLANES, BLK * LANES)
            acc = z3[:, :, :LANES]
            for i in range(1, BLK):
                zi = z3[:, :, LANES * i:LANES * (i + 1)]
                acc = acc + jnp.where(sub >= i, pltpu.roll(zi, i, axis=1), 0.0)
            yq_ref[b * NQ + q] = acc.reshape(t_blk, LANES)
            ublks.append(jnp.concatenate(
                [u_refs[q][b, pl.ds(r, nblk, stride=BLK), :] for r in range(BLK)], axis=1))
        ls = _dot(jnp.concatenate(ublks, axis=0).astype(BF16), w7_ref[q])
        for b in range(nb):
            for t in range(2 * tiles):
                dst = pl.ds((q % 2) * tiles + t % tiles, nblk, stride=SUBLANES)
                lsp_ref[b, t // tiles, q // 2, dst, :] = ls[b * nblk:(b + 1) * nblk, LANES * t:LANES * (t + 1)]

    chains = [(b, half) for b in range(nb) for half in range(2)]
    coef = {half: (are_ref[half], aim_ref[half]) for half in range(2)}
    state = {c: (sp_ref[c[0], 0, c[1]], sp_ref[c[0], 1, c[1]]) for c in chains}
    for k in range(nblk):
        rows = slice(SUBLANES * k, SUBLANES * (k + 1))
        for b, half in chains:
            a_re, a_im = coef[half]
            s_re, s_im = state[b, half]
            l_re = lsp_ref[b, 0, half, rows, :]
            l_im = lsp_ref[b, 1, half, rows, :]
            lsp_ref[b, 0, half, rows, :] = s_re
            lsp_ref[b, 1, half, rows, :] = s_im
            state[b, half] = (a_re * s_re - a_im * s_im + l_re, a_re * s_im + a_im * s_re + l_im)
    for (b, half), (s_re, s_im) in state.items():
        sp_ref[b, 0, half] = s_re
        sp_ref[b, 1, half] = s_im
        for j in range(SUBLANES):
            lanes = slice((half * SUBLANES + j) * LANES, (half * SUBLANES + j + 1) * LANES)
            hre_ref[b:b + 1, lanes] = s_re[j:j + 1]
            him_ref[b:b + 1, lanes] = s_im[j:j + 1]

    for q in range(NQ):
        sc = jnp.concatenate(
            [jnp.concatenate(
                [lsp_ref[b, t // tiles, q // 2, pl.ds((q % 2) * tiles + t % tiles, nblk, stride=SUBLANES), :]
                 for t in range(2 * tiles)], axis=1) for b in range(nb)], axis=0)
        y2 = _dot(sc.astype(BF16), gq_ref[q])
        for b in range(nb):
            for r in range(BLK):
                y2s_ref[pl.ds(r, nblk, stride=BLK), :] = y2[b * nblk:(b + 1) * nblk, LANES * r:LANES * (r + 1)]
            yq_ref[b * NQ + q] = yq_ref[b * NQ + q] + y2s_ref[...]

    for b in range(nb):
        y = jnp.concatenate([yq_ref[b * NQ + q] for q in range(NQ)], axis=1)
        u = jnp.concatenate([u_refs[q][b] for q in range(NQ)], axis=1)
        y_ref[b] = _glu_out(y, u, dsk_ref, wglu_ref, bglu_ref)


def _s5(u, w, t_blk):
    nb, seq, _ = u.shape
    nblk = t_blk // BLK
    u_specs = [pl.BlockSpec((nb, t_blk, LANES), lambda c, q=q: (0, c, q)) for q in range(NQ)]
    state = pl.BlockSpec((nb, N_STATE), lambda c: (0, 0))
    packed = lambda a: a.reshape(2, SUBLANES, LANES)
    return pl.pallas_call(
        _s5_kernel,
        grid=(seq // t_blk,),
        in_specs=u_specs + [_const_spec((2, SUBLANES, LANES)), _const_spec((2, SUBLANES, LANES)),
                            _const_spec((NQ, LANES, BLK * LANES)),
                            _const_spec((NQ, BLK * LANES, 2 * QS)), _const_spec((NQ, 2 * QS, BLK * LANES)),
                            _const_spec((1, SSM_W)), _const_spec((SSM_W, SSM_W)), _const_spec((1, SSM_W))],
        out_specs=(pl.BlockSpec((nb, t_blk, SSM_W), lambda c: (0, c, 0)), state, state),
        out_shape=(jax.ShapeDtypeStruct((nb, seq, SSM_W), BF16),
                   jax.ShapeDtypeStruct((nb, N_STATE), F32), jax.ShapeDtypeStruct((nb, N_STATE), F32)),
        scratch_shapes=[pltpu.VMEM((nb, 2, 2, nblk * SUBLANES, LANES), F32),
                        pltpu.VMEM((nb * NQ, t_blk, LANES), F32),
                        pltpu.VMEM((t_blk, LANES), F32),
                        pltpu.VMEM((nb, 2, 2, SUBLANES, LANES), F32)],
        compiler_params=_params(1),
        name="s5",
    )(u, u, u, u, packed(w["a8_re"]), packed(w["a8_im"]), w["k8"], w["w7"], w["gq"],
      w["ssm_d"], w["w_glu"], w["b_glu"])


def _s5_step_kernel(u_ref, hre_ref, him_ref, are_ref, aim_ref, k8_ref, w7_ref, gq_ref,
                    dsk_ref, wglu_ref, bglu_ref, y_ref, nre_ref, nim_ref):
    u = u_ref[...]
    ys = []
    for q in range(NQ):
        ch = slice(q * QS, (q + 1) * QS)
        uq = u[:, q * LANES:(q + 1) * LANES].astype(BF16)
        x = _dot(uq, w7_ref[q, (BLK - 1) * LANES:, :])
        a_re, a_im = are_ref[:, ch], aim_ref[:, ch]
        s_re, s_im = hre_ref[:, ch], him_ref[:, ch]
        nre_ref[:, ch] = a_re * s_re - a_im * s_im + x[:, :QS]
        nim_ref[:, ch] = a_re * s_im + a_im * s_re + x[:, QS:]
        s = jnp.concatenate([s_re, s_im], axis=1).astype(BF16)
        ys.append(_dot(s, gq_ref[q, :, :LANES]) + _dot(uq, k8_ref[q, :, :LANES]))
    y_ref[...] = _glu_out(jnp.concatenate(ys, axis=1), u, dsk_ref, wglu_ref, bglu_ref)


def _s5_step(u, h_re, h_im, w):
    rows = u.shape[0]
    return pl.pallas_call(
        _s5_step_kernel,
        out_shape=(jax.ShapeDtypeStruct((rows, SSM_W), BF16),
                   jax.ShapeDtypeStruct((rows, N_STATE), F32), jax.ShapeDtypeStruct((rows, N_STATE), F32)),
        compiler_params=pltpu.CompilerParams(vmem_limit_bytes=VMEM_LIMIT),
        name="s5_step",
    )(u, h_re, h_im, w["ab_re"], w["ab_im"], w["k8"], w["w7"], w["gq"], w["ssm_d"], w["w_glu"], w["b_glu"])


def _step_attention(q, kn, vn, cache, dil):
    n_feat, window = cache.shape
    row8 = lax.broadcasted_iota(jnp.int32, (8, GROUP_W), 0)
    head8 = lax.broadcasted_iota(jnp.int32, (8, GROUP_W), 1) // HEAD_DIM == row8
    diag8 = (lax.broadcasted_iota(jnp.int32, (8, LANES), 0)
             == lax.broadcasted_iota(jnp.int32, (8, LANES), 1))
    q8 = jnp.where(head8, jnp.broadcast_to(q, (8, GROUP_W)), 0.0)
    pos = lax.broadcasted_iota(jnp.int32, (8, window), 1)
    s = _dot(q8.astype(BF16), cache[:GROUP_W].astype(BF16))
    s = jnp.where(jnp.bitwise_and(pos, dil - 1) == 0, s, NEG)
    s_new = jnp.sum(q8 * kn, axis=-1, keepdims=True)
    m = jnp.maximum(jnp.max(s, axis=-1, keepdims=True), s_new)
    p = jnp.exp(s - m)
    p_new = jnp.exp(s_new - m)
    den = jnp.sum(p, axis=-1, keepdims=True) + p_new
    o8 = (_dot_nt(p.astype(BF16), cache[GROUP_W:].astype(BF16)) + p_new * vn) / den
    o = jnp.sum(jnp.where(head8, o8, 0.0), axis=0, keepdims=True)
    lse = jnp.sum(jnp.where(diag8, m + jnp.log(den), 0.0), axis=0, keepdims=True)
    eye = (lax.broadcasted_iota(jnp.int32, (n_feat, n_feat), 0)
           == lax.broadcasted_iota(jnp.int32, (n_feat, n_feat), 1))
    new_col = jnp.sum(jnp.where(eye, jnp.concatenate([kn, vn], axis=1), 0.0), axis=-1, keepdims=True)
    rolled = pltpu.roll(cache, window - 1, axis=1)
    last = lax.broadcasted_iota(jnp.int32, (n_feat, window), 1) == window - 1
    return o, lse, jnp.where(last, new_col, rolled)


def _swa_kernel(q_ref, k_ref, v_ref, kp_ref, vp_ref, qs_ref, ks_ref, vs_ref, c_ref, *rest,
                qb, chunk, dil, casts):
    n = len(casts)
    o_ref, l_ref, os_ref, ls_ref, n_ref = rest[n:n + 5]
    casts.run(rest[:n] + rest[n + 5:])
    first = pl.program_id(2) == 0

    def rows(ref):
        return jnp.concatenate([ref[t] for t in range(ref.shape[0])], axis=0)

    q = rows(q_ref)
    kk = jnp.concatenate([rows(kp_ref), rows(k_ref)], axis=0)
    vv = jnp.concatenate([rows(vp_ref), rows(v_ref)], axis=0)
    nk = 2 * KEYS_PER_QUERY
    row = lax.broadcasted_iota(jnp.int32, (128, nk), 0)
    col = lax.broadcasted_iota(jnp.int32, (128, nk), 1)
    band = jnp.where(col >= row, jnp.where(col <= row + KEYS_PER_QUERY, 0.0, NEG), NEG)
    band_first = band + jnp.where(col < KEYS_PER_QUERY, jnp.where(first, NEG, 0.0), 0.0)
    lane = lax.broadcasted_iota(jnp.int32, (1, GROUP_W), 1) // HEAD_DIM
    lane_l = lax.broadcasted_iota(jnp.int32, (1, LANES), 1)
    for j in range(qb // 128):
        qj = q[128 * j:128 * (j + 1)]
        kj = kk[128 * j:128 * j + nk]
        vj = vv[128 * j:128 * j + nk]
        bias = band_first if j == 0 else band
        qs = jnp.concatenate([jnp.where(lane == h, qj, jnp.zeros_like(qj)) for h in range(HEADS)], axis=0)
        s = (_dot_nt(qs, kj).reshape(HEADS, 128, nk) + bias[None]).reshape(HEADS * 128, nk)
        m = jnp.max(s, axis=-1, keepdims=True)
        p = jnp.exp(s - m)
        den = jnp.sum(p, axis=-1, keepdims=True)
        on = _dot(p.astype(BF16), vj) * (1.0 / den)
        lse = m + jnp.log(den)
        o_acc = jnp.zeros((128, GROUP_W), F32)
        l_acc = jnp.zeros((128, LANES), F32)
        for h in range(HEADS):
            rows_h = slice(128 * h, 128 * (h + 1))
            o_acc = jnp.where(lane == h, on[rows_h], o_acc)
            l_acc = jnp.where(lane_l == h, lse[rows_h], l_acc)
        o_out = o_acc.astype(BF16)
        if chunk >= 128:
            t, r0 = divmod(128 * j, chunk)
            o_ref[t, r0:r0 + 128, :] = o_out
            l_ref[t, r0:r0 + 128, :] = l_acc
        else:
            for s_ in range(128 // chunk):
                t = (128 * j) // chunk + s_
                o_ref[t] = o_out[s_ * chunk:(s_ + 1) * chunk]
                l_ref[t] = l_acc[s_ * chunk:(s_ + 1) * chunk]

    def sample_step():
        o, lse, new_cache = _step_attention(qs_ref[0].astype(F32), ks_ref[0], vs_ref[0], c_ref[0], dil)
        os_ref[0] = o.astype(BF16)
        ls_ref[0] = lse
        n_ref[0] = new_cache

    if c_ref.shape[-1] >= OWN_REGION_WINDOW:
        pl.when(pl.program_id(0) >= 0)(sample_step)
    else:
        sample_step()


def _swa(q, k, v, g, tile, q_s, kv_s, cache, cast_items):
    window, dil = GROUPS[g]
    nb, seq, _ = q.shape
    n_sample = q_s.shape[0]
    n_tiles = seq // tile
    chunk = tile // dil
    n = seq // dil
    qb = min(512, n)
    tpb = qb // chunk
    n_i = n // qb
    assert nb * dil * n_i >= n_sample and cache.shape == (n_sample, 2 * GROUP_W, window)
    view = lambda a: a.reshape(nb, n_tiles, dil, chunk, a.shape[-1])
    cur = pl.BlockSpec((None, tpb, None, chunk, GROUP_W), lambda b, r, i: (b, i, r, 0, g))
    if chunk >= 128:
        prev = pl.BlockSpec((None, 1, None, 128, GROUP_W),
                            lambda b, r, i: (b, jnp.maximum(i * tpb - 1, 0), r, chunk // 128 - 1, g))
    else:
        ptiles = 128 // chunk
        prev = pl.BlockSpec((None, ptiles, None, chunk, GROUP_W),
                            lambda b, r, i: (b, jnp.maximum(i * (tpb // ptiles) - 1, 0), r, 0, g))
    out_block = lambda width: pl.BlockSpec((None, tpb, None, chunk, width), lambda b, r, i: (b, i, r, 0, 0))
    step_of = lambda b, r, i: (b * dil + r) * n_i + i
    sample_of = lambda b, r, i: jnp.minimum(step_of(b, r, i), n_sample - 1)
    sample_block = lambda *dims: pl.BlockSpec((1,) + dims, lambda b, r, i: (sample_of(b, r, i),) + (0,) * len(dims))
    group_lanes = lambda col: pl.BlockSpec((1, 1, GROUP_W), lambda b, r, i: (sample_of(b, r, i), 0, col))
    casts = _CastStream(cast_items, nb * dil * n_i, step_of)
    o, lse, o_s, l_s, new_cache, *cast_outs = pl.pallas_call(
        functools.partial(_swa_kernel, qb=qb, chunk=chunk, dil=dil, casts=casts),
        grid=(nb, dil, n_i),
        in_specs=[cur, cur, cur, prev, prev,
                  group_lanes(g), group_lanes(g), group_lanes(ATTN_W // GROUP_W + g),
                  sample_block(2 * GROUP_W, window)] + casts.in_specs,
        out_specs=(out_block(GROUP_W), out_block(LANES),
                   sample_block(1, GROUP_W), sample_block(1, LANES), sample_block(2 * GROUP_W, window))
                  + tuple(casts.out_specs),
        out_shape=(jax.ShapeDtypeStruct((nb, n_tiles, dil, chunk, GROUP_W), BF16),
                   jax.ShapeDtypeStruct((nb, n_tiles, dil, chunk, LANES), F32),
                   jax.ShapeDtypeStruct((n_sample, 1, GROUP_W), BF16),
                   jax.ShapeDtypeStruct((n_sample, 1, LANES), F32),
                   jax.ShapeDtypeStruct(cache.shape, F32)) + tuple(casts.out_shapes),
        compiler_params=_params(3),
        name="swa_g%d" % g,
    )(view(q), view(k), view(v), view(k), view(v),
      q_s.reshape(n_sample, 1, ATTN_W), kv_s.reshape(n_sample, 1, 2 * ATTN_W),
      kv_s.reshape(n_sample, 1, 2 * ATTN_W), cache, *casts.arrays)
    return (o.reshape(nb * seq, GROUP_W), lse.reshape(nb * seq, LANES),
            o_s.reshape(n_sample, GROUP_W), l_s.reshape(n_sample, LANES), new_cache, cast_outs)


def _position_order(o_ref, l_ref, dil, nat_ref, slab0):
    tm = o_ref.shape[0]
    n = tm // dil
    for r in range(dil):
        rows = slice(r * n, (r + 1) * n)
        dst = pl.ds(r, n, stride=dil)
        nat_ref[slab0, dst, :] = o_ref[rows, :LANES].astype(F32)
        nat_ref[slab0 + 1, dst, :] = o_ref[rows, LANES:].astype(F32)
        nat_ref[slab0 + 2, dst, :] = l_ref[rows, :]
    o = jnp.concatenate([nat_ref[slab0], nat_ref[slab0 + 1]], axis=1)
    return o, nat_ref[slab0 + 2]


def _mix_ffn_kernel(x1_ref, ys_ref, o0_ref, o1_ref, o2_ref, l0_ref, l1_ref, l2_ref,
                    gm_ref, wgt_ref, wsp_ref, wap_ref, wo_ref, seget_ref,
                    g2_ref, wg_ref, wu_ref, wd_ref, y_ref, *scratch):
    x1 = x1_ref[...]
    h = _rms(x1, gm_ref[...]).astype(BF16)
    gates = _sigmoid(_dot(h, wgt_ref[...]))
    os_ = [o0_ref[...].astype(F32)]
    ls = [l0_ref[...]]
    for g, (o_ref, l_ref) in enumerate(((o1_ref, l1_ref), (o2_ref, l2_ref)), start=1):
        if scratch:
            o, l = _position_order(o_ref, l_ref, GROUPS[g][1], scratch[0], 3 * (g - 1))
        else:
            o, l = o_ref[...].astype(F32), l_ref[...]
        os_.append(o)
        ls.append(l)
    l_top = jnp.maximum(jnp.maximum(ls[0], ls[1]), ls[2])
    es = [jnp.exp(l - l_top) for l in ls]
    inv = 1.0 / (es[0] + es[1] + es[2])
    seget = seget_ref[...]
    y_attn = None
    for e, o in zip(es, os_):
        t = _dot((e * inv).astype(BF16), seget) * o
        y_attn = t if y_attn is None else y_attn + t
    mixed = (gates[:, :D_MODEL] * _dot(ys_ref[...], wsp_ref[...])
             + gates[:, D_MODEL:] * _dot(y_attn.astype(BF16), wap_ref[...]))
    x2 = x1 + _dot(mixed.astype(BF16), wo_ref[...])
    y_ref[...] = _ffn(x2, g2_ref, wg_ref, wu_ref, wd_ref)


def _mix_ffn(x1, y_ssm, os_, ls_, w, tm, residue_major):
    rows = x1.shape[0]
    row_spec = lambda width: pl.BlockSpec((tm, width), lambda i: (i, 0))
    n_slabs = 3 * (len(GROUPS) - 1)
    scratch = [pltpu.VMEM((n_slabs, tm, LANES), F32)] if residue_major else []
    return pl.pallas_call(
        _mix_ffn_kernel,
        grid=(rows // tm,),
        in_specs=[row_spec(D_MODEL), row_spec(SSM_W)] + [row_spec(GROUP_W)] * 3 + [row_spec(LANES)] * 3
                 + [_const_spec((1, D_MODEL)), _const_spec((D_MODEL, 2 * D_MODEL)),
                    _const_spec((SSM_W, D_MODEL)), _const_spec((GROUP_W, D_MODEL)),
                    _const_spec((D_MODEL, D_MODEL)), _const_spec((LANES, GROUP_W)),
                    _const_spec((1, D_MODEL)), _const_spec((D_MODEL, D_FF)), _const_spec((D_MODEL, D_FF)),
                    _const_spec((D_FF, D_MODEL))],
        out_specs=row_spec(D_MODEL),
        out_shape=jax.ShapeDtypeStruct((rows, D_MODEL), F32),
        scratch_shapes=scratch,
        compiler_params=_params(1),
        name="mix_ffn",
    )(x1, y_ssm, *os_, *ls_, w["g_mix"], w["w_gates"], w["w_ssm_proj"], w["w_attn_proj"], w["w_o"],
      w["seg_et"], w["g_ffn2"], w["w2_gate"], w["w2_up"], w["w2_down"])


def _prepare_weights(g_ffn1, w1_gate, w1_up, w1_down, g_mix, w_in, g_q, g_k,
                     ssm_a_re, ssm_a_im, ssm_log_dt, ssm_b_re, ssm_b_im, ssm_c_re, ssm_c_im,
                     ssm_d, w_glu, b_glu, w_ssm_proj, w_attn_proj, w_o, g_ffn2, w2_gate, w2_up, w2_down):
    w = {}
    row = lambda a: a.reshape(1, -1).astype(F32)
    w["g_ffn1"], w["g_mix"], w["g_ffn2"] = row(g_ffn1), row(g_mix), row(g_ffn2)
    w["w_glu"] = w_glu.astype(BF16)
    f32 = lambda a: a.astype(F32)
    w["late"] = (
        (("w2_gate", f32(w2_gate), 0, D_FF), ("w2_up", f32(w2_up), 0, D_FF)),
        (("w2_down", f32(w2_down), 0, D_MODEL), ("w_o", f32(w_o), 0, D_MODEL),
         ("w_ssm_proj", f32(w_ssm_proj), 0, D_MODEL), ("w_attn_proj", f32(w_attn_proj), 0, D_MODEL),
         ("w_gates", f32(w_in), IN_A, 2 * D_MODEL)),
        ())
    per_head = lambda g: jnp.broadcast_to(g[:, None, :], (len(GROUPS), HEADS, HEAD_DIM)).reshape(1, ATTN_W)
    w["g_q"], w["g_k"] = per_head(g_q.astype(F32)), per_head(g_k.astype(F32))
    head_of_lane = jnp.arange(GROUP_W) // HEAD_DIM
    w["seg_mean"] = ((head_of_lane[:, None] == head_of_lane[None, :]) / HEAD_DIM).astype(BF16)
    w["seg_et"] = (jnp.arange(LANES)[:, None] == head_of_lane[None, :]).astype(BF16)
    w["ssm_d"], w["b_glu"] = row(ssm_d), row(b_glu)

    early = (("w1_gate", f32(w1_gate), 0, D_FF), ("w1_up", f32(w1_up), 0, D_FF),
             ("w1_down", f32(w1_down), 0, D_MODEL), ("w_in_a", f32(w_in), 0, IN_A))
    (pw, w["k8"], w["w7"], w["gq"]), converted = _ssm_prep(
        ssm_a_re, ssm_a_im, ssm_log_dt, ssm_b_re, ssm_b_im, ssm_c_re, ssm_c_im, [e[1:] for e in early])
    w.update({e[0]: c for e, c in zip(early, converted)})
    per_group = lambda a: a.reshape(SSM_GROUPS, SSM_CH, SSM_P)[:, 0].reshape(1, N_STATE)
    w["ab_re"], w["ab_im"] = per_group(pw[0, 0]), per_group(pw[0, 1])
    w["a8_re"], w["a8_im"] = per_group(pw[1, 0]), per_group(pw[1, 1])
    return w


def _forward(x_p, x_s, state_s, caches_s, w):
    nb, seq, _ = x_p.shape
    n_s = x_s.shape[0]
    assert seq % TILE == 0 and x_s.shape[1] == 1
    (x1, u, q, k, v), caches_p = _ffn_in(x_p.reshape(nb * seq, D_MODEL), w, TILE, seq // TILE, True)
    (x1_s, u_s, q_s, _, _), kv_s = _ffn_in(x_s.reshape(n_s, D_MODEL), w, n_s, 1, False)
    y_ssm, *h_p = _s5(u.reshape(nb, seq, SSM_W), w, min(S5_T, seq))
    y_ssm_s, *h_s = _s5_step(u_s, state_s[0], state_s[1], w)
    q3, k3, v3 = (a.reshape(nb, seq, ATTN_W) for a in (q, k, v))
    os_p, ls_p, os_s, ls_s, new_caches_s = [], [], [], [], []
    for g in range(len(GROUPS)):
        late = w["late"][g]
        o, l, o_s, l_s, c_new, converted = _swa(q3, k3, v3, g, TILE, q_s, kv_s, caches_s[g],
                                                [e[1:] for e in late])
        w = {**w, **{e[0]: c for e, c in zip(late, converted)}}
        for lst, a in zip((os_p, ls_p, os_s, ls_s, new_caches_s), (o, l, o_s, l_s, c_new)):
            lst.append(a)
    y_p = _mix_ffn(x1, y_ssm.reshape(nb * seq, SSM_W), os_p, ls_p, w, TILE, True)
    y_s = _mix_ffn(x1_s, y_ssm_s, os_s, ls_s, w, n_s, False)
    return (y_p.reshape(nb, seq, D_MODEL), y_s.reshape(n_s, 1, D_MODEL), h_p, h_s, caches_p, new_caches_s)


def _to_features_major(c):
    nb, win = c.shape[:2]
    return jnp.transpose(c, (0, 2, 3, 4, 1)).reshape(nb, 2 * GROUP_W, win)


def _to_window_buffer(c):
    nb, _, win = c.shape
    return jnp.transpose(c.reshape(nb, 2, HEADS, HEAD_DIM, win), (0, 4, 1, 2, 3))[None]


def kernel(x_prompt, x_sample, cache_kv_w128, cache_kv_w512, cache_kv_w2048, state_ssm_re, state_ssm_im, g_ffn1, w1_gate, w1_up, w1_down, g_mix, w_in, g_q, g_k, ssm_a_re, ssm_a_im, ssm_log_dt, ssm_b_re, ssm_b_im, ssm_c_re, ssm_c_im, ssm_d, w_glu, b_glu, w_ssm_proj, w_attn_proj, w_o, g_ffn2, w2_gate, w2_up, w2_down):
    layer_weights = (g_ffn1, w1_gate, w1_up, w1_down, g_mix, w_in, g_q, g_k,
                     ssm_a_re, ssm_a_im, ssm_log_dt, ssm_b_re, ssm_b_im, ssm_c_re, ssm_c_im,
                     ssm_d, w_glu, b_glu, w_ssm_proj, w_attn_proj, w_o, g_ffn2, w2_gate, w2_up, w2_down)
    depth = g_ffn1.shape[0]
    assert depth == 1, "window caches of deeper layers would need the previous layer's outputs"
    w = _prepare_weights(*(a[0] for a in layer_weights))
    nb_s = x_sample.shape[0]
    sdt = state_ssm_re.dtype
    as_state = lambda h: h.reshape(1, -1, SSM_GROUPS, SSM_P).astype(sdt)

    caches = [_to_features_major(c[0]) for c in (cache_kv_w128, cache_kv_w512, cache_kv_w2048)]
    state_s = (state_ssm_re[0].reshape(nb_s, N_STATE).astype(F32),
               state_ssm_im[0].reshape(nb_s, N_STATE).astype(F32))
    y_p, y_s, h_p, h_s, kv_p, kv_s = _forward(x_prompt, x_sample, state_s, caches, w)
    return ((y_p, y_s) + tuple(_to_window_buffer(c) for c in kv_p) + (as_state(h_p[0]), as_state(h_p[1]))
            + tuple(_to_window_buffer(c) for c in kv_s) + (as_state(h_s[0]), as_state(h_s[1])))
```

```python
import functools

import jax
import jax.numpy as jnp
from jax import lax
from jax.experimental import pallas as pl
from jax.experimental.pallas import tpu as pltpu

F32 = jnp.float32
BF16 = jnp.bfloat16

D_MODEL = 1024
D_FF = 2816
HEAD_DIM = 64
HEADS = 4
GROUPS = ((128, 1), (512, 4), (2048, 16))
KEYS_PER_QUERY = 128
GROUP_W = HEADS * HEAD_DIM
ATTN_W = len(GROUPS) * GROUP_W
SSM_W = 512
SSM_GROUPS = 32
SSM_CH = 16
SSM_P = 64
N_STATE = SSM_GROUPS * SSM_P
IN_A = SSM_W + 3 * ATTN_W
RMS_EPS = 1e-6
NEG = -1e30
LANES = 128
SUBLANES = 8

TILE = 512
FF_CHUNKS = ((0, 1024), (1024, 1024), (2048, 768))
BLK = 4
NQ = SSM_W // LANES
QS = N_STATE // NQ
S5_T = 1024
OWN_REGION_WINDOW = 2048
VMEM_LIMIT = 56 * 1024 * 1024


def _const_spec(shape):
    nd = len(shape)
    return pl.BlockSpec(shape, lambda *_: (0,) * nd, pipeline_mode=pl.Buffered(1))


def _params(n_grid):
    return pltpu.CompilerParams(dimension_semantics=("arbitrary",) * n_grid,
                                vmem_limit_bytes=VMEM_LIMIT)


def _rms(x, g):
    ms = jnp.mean(x * x, axis=-1, keepdims=True)
    return x * lax.rsqrt(ms + RMS_EPS) * g


def _dot(a, b):
    return jnp.dot(a, b, preferred_element_type=F32)


def _dot_nt(a, b):
    return lax.dot_general(a, b, (((1,), (1,)), ((), ())), preferred_element_type=F32)


class _CastStream:
    def __init__(self, items, n_steps, step_of):
        self.cols = [(c0, cw) for _, c0, cw in items]
        self.arrays = [a for a, _, _ in items]
        self.in_specs, self.out_specs, self.out_shapes = [], [], []
        for a, c0, cw in items:
            rows, width = a.shape
            chunks = n_steps
            while rows % chunks or (rows // chunks) % 16:
                assert chunks % 2 == 0, (a.shape, n_steps)
                chunks //= 2
            index = lambda *g, hold=n_steps // chunks: (step_of(*g) // hold, 0)
            if c0 == 0:
                width = cw
            self.in_specs.append(pl.BlockSpec((rows // chunks, width), index))
            self.out_specs.append(pl.BlockSpec((rows // chunks, cw), index))
            self.out_shapes.append(jax.ShapeDtypeStruct((rows, cw), BF16))

    def __len__(self):
        return len(self.cols)

    def run(self, refs):
        n = len(self.cols)
        for i_ref, o_ref, (c0, cw) in zip(refs[:n], refs[n:], self.cols):
            o_ref[...] = i_ref[:, c0:c0 + cw].astype(BF16)


def _sigmoid(x):
    return 0.5 * jnp.tanh(0.5 * x) + 0.5


def _ffn(x, g_ref, wg_ref, wu_ref, wd_ref):
    xn = _rms(x, g_ref[...]).astype(BF16)
    acc = None
    for f0, fw in FF_CHUNKS:
        hg = _dot(xn, wg_ref[:, f0:f0 + fw])
        hu = _dot(xn, wu_ref[:, f0:f0 + fw])
        a = (hg * _sigmoid(hg) * hu).astype(BF16)
        d = _dot(a, wd_ref[f0:f0 + fw, :])
        acc = d if acc is None else acc + d
    return x + 0.5 * acc


def _split3_dot_nt(a, b):
    a_hi = a.astype(BF16)
    a_lo = (a - a_hi.astype(F32)).astype(BF16)
    b_hi = b.astype(BF16)
    b_lo = (b - b_hi.astype(F32)).astype(BF16)
    return _dot_nt(a_hi, b_hi) + _dot_nt(a_hi, b_lo) + _dot_nt(a_lo, b_hi)


def _prep_kernel(are_ref, aim_ref, ldt_ref, bre_ref, bim_ref, cre_ref, cim_ref, *rest, casts):
    n = len(casts)
    pw_ref, k8_ref, w7_ref, gq_ref = rest[n:n + 4]
    casts.run(rest[:n] + rest[n + 4:])
    _prep_quarter(are_ref, aim_ref, ldt_ref, bre_ref, bim_ref, cre_ref, cim_ref,
                  pw_ref, k8_ref, w7_ref, gq_ref)


def _prep_quarter(are_ref, aim_ref, ldt_ref, bre_ref, bim_ref, cre_ref, cim_ref,
                  pw_ref, k8_ref, w7_ref, gq_ref):
    a_re = are_ref[...]
    a_im = aim_ref[...]
    dt = jnp.exp(ldt_ref[...])
    mag = jnp.exp(a_re * dt)
    ab_re = mag * jnp.cos(a_im * dt)
    ab_im = mag * jnp.sin(a_im * dt)
    inv = 1.0 / (a_re * a_re + a_im * a_im)
    f_re = ((ab_re - 1.0) * a_re + ab_im * a_im) * inv
    f_im = (ab_im * a_re - (ab_re - 1.0) * a_im) * inv
    b_re = bre_ref[...]
    b_im = bim_ref[...]
    bb_re = f_re * b_re - f_im * b_im
    bb_im = f_re * b_im + f_im * b_re
    c_re = cre_ref[...]
    c_im = cim_ref[...]
    same_group = (lax.broadcasted_iota(jnp.int32, (LANES, LANES), 0) // SSM_CH
                  == lax.broadcasted_iota(jnp.int32, (LANES, LANES), 1) // SSM_CH)
    spread = (lax.broadcasted_iota(jnp.int32, (SSM_P, QS), 1) % SSM_P
              == lax.broadcasted_iota(jnp.int32, (SSM_P, QS), 0)).astype(BF16)
    spread_t = (lax.broadcasted_iota(jnp.int32, (QS, SSM_P), 0) % SSM_P
                == lax.broadcasted_iota(jnp.int32, (QS, SSM_P), 1)).astype(BF16)
    mask_w = (lax.broadcasted_iota(jnp.int32, (LANES, QS), 0) // SSM_CH
              == lax.broadcasted_iota(jnp.int32, (LANES, QS), 1) // SSM_P)
    mask_g = (lax.broadcasted_iota(jnp.int32, (QS, LANES), 0) // SSM_P
              == lax.broadcasted_iota(jnp.int32, (QS, LANES), 1) // SSM_CH)
    p_re = jnp.ones_like(ab_re)
    p_im = jnp.zeros_like(ab_re)
    for i in range(BLK + 1):
        if i == 1:
            pw_ref[0, 0] = p_re
            pw_ref[0, 1] = p_im
        if i == BLK:
            pw_ref[1, 0] = p_re
            pw_ref[1, 1] = p_im
        if i >= 1:
            r = i - 1
            gcs = (c_re * p_re - c_im * p_im, -(c_re * p_im + c_im * p_re))
            for part in range(2):
                blk = jnp.where(mask_g, _dot_nt(spread_t, gcs[part].astype(BF16)), 0.0)
                gq_ref[0, part * QS:(part + 1) * QS, r * LANES:(r + 1) * LANES] = blk.astype(BF16)
        if i < BLK:
            r = BLK - 1 - i
            xs = (p_re * bb_re - p_im * bb_im, p_re * bb_im + p_im * bb_re)
            kk = jnp.where(same_group, _split3_dot_nt(xs[0], c_re) - _split3_dot_nt(xs[1], c_im), 0.0)
            k8_ref[0, :, i * LANES:(i + 1) * LANES] = kk.astype(BF16)
            for part in range(2):
                blk = jnp.where(mask_w, _dot(xs[part].astype(BF16), spread), 0.0)
                w7_ref[0, r * LANES:(r + 1) * LANES, part * QS:(part + 1) * QS] = blk.astype(BF16)
        p_re, p_im = p_re * ab_re - p_im * ab_im, p_re * ab_im + p_im * ab_re


def _ssm_prep(a_re, a_im, log_dt, b_re, b_im, c_re, c_im, cast_items):
    gc = SSM_GROUPS * SSM_CH
    casts = _CastStream(cast_items, NQ, lambda q: q)
    rep = lambda a: jnp.repeat(a.astype(F32), SSM_CH, axis=0)
    b_rows = lambda b: jnp.transpose(b.astype(F32), (0, 2, 1)).reshape(gc, SSM_P)
    c_rows = lambda c: c.astype(F32).reshape(gc, SSM_P)
    rows = lambda width: pl.BlockSpec((LANES, width), lambda q: (q, 0))
    per_q = lambda d1, d2: pl.BlockSpec((1, d1, d2), lambda q: (q, 0, 0))
    outs = pl.pallas_call(
        functools.partial(_prep_kernel, casts=casts),
        grid=(NQ,),
        in_specs=[rows(SSM_P), rows(SSM_P), rows(1)] + [rows(SSM_P)] * 4 + casts.in_specs,
        out_specs=(pl.BlockSpec((2, 2, LANES, SSM_P), lambda q: (0, 0, q, 0)),
                   per_q(LANES, BLK * LANES), per_q(BLK * LANES, 2 * QS), per_q(2 * QS, BLK * LANES))
                  + tuple(casts.out_specs),
        out_shape=(jax.ShapeDtypeStruct((2, 2, gc, SSM_P), F32),
                   jax.ShapeDtypeStruct((NQ, LANES, BLK * LANES), BF16),
                   jax.ShapeDtypeStruct((NQ, BLK * LANES, 2 * QS), BF16),
                   jax.ShapeDtypeStruct((NQ, 2 * QS, BLK * LANES), BF16)) + tuple(casts.out_shapes),
        compiler_params=_params(1),
        name="ssm_prep",
    )(rep(a_re), rep(a_im), rep(log_dt.reshape(SSM_GROUPS, 1)), b_rows(b_re), b_rows(b_im),
      c_rows(c_re), c_rows(c_im), *casts.arrays)
    return outs[:4], outs[4:]


def _head_norm(t, g, seg):
    sq = (t * t).astype(BF16)
    ms = jnp.concatenate(
        [_dot(sq[:, s * GROUP_W:(s + 1) * GROUP_W], seg) for s in range(len(GROUPS))], axis=1)
    return t * lax.rsqrt(ms + RMS_EPS) * g


def _store_attn_rows(val, out_ref, perm_ref, slab0):
    tm = val.shape[0]
    out_ref[:, :GROUP_W] = val[:, :GROUP_W].astype(BF16)
    for g in range(1, len(GROUPS)):
        dil = GROUPS[g][1]
        n = tm // dil
        for half in range(GROUP_W // LANES):
            c0 = g * GROUP_W + half * LANES
            col = val[:, c0:c0 + LANES]
            if perm_ref is None:
                out_ref[:, c0:c0 + LANES] = col.astype(BF16)
                continue
            slab = slab0 + (g - 1) * (GROUP_W // LANES) + half
            perm_ref[slab] = col
            for r in range(dil):
                out_ref[r * n:(r + 1) * n, c0:c0 + LANES] = (
                    perm_ref[slab, pl.ds(r, n, stride=dil), :].astype(BF16))


def _ffn_in_kernel(x_ref, g1_ref, wg_ref, wu_ref, wd_ref, gm_ref, win_ref, gq_ref, gk_ref, seg_ref,
                   x1_ref, u_ref, q_ref, k_ref, v_ref, *rest, tiles_per_seq, keeps):
    perm_ref = rest[-1] if len(rest) > 1 else None
    x1 = _ffn(x_ref[...], g1_ref, wg_ref, wu_ref, wd_ref)
    x1_ref[...] = x1
    h = _rms(x1, gm_ref[...]).astype(BF16)
    proj = _dot(h, win_ref[...])
    u_ref[...] = proj[:, :SSM_W]
    seg = seg_ref[...]
    q = _head_norm(proj[:, SSM_W:SSM_W + ATTN_W], gq_ref[...], seg)
    k = _head_norm(proj[:, SSM_W + ATTN_W:SSM_W + 2 * ATTN_W], gk_ref[...], seg)
    v = proj[:, SSM_W + 2 * ATTN_W:]
    slabs = (len(GROUPS) - 1) * (GROUP_W // LANES)
    _store_attn_rows(q * HEAD_DIM ** -0.5, q_ref, perm_ref, 0)
    _store_attn_rows(k, k_ref, perm_ref, slabs)
    _store_attn_rows(v, v_ref, perm_ref, 2 * slabs)
    if perm_ref is None:
        rest[0][:, :ATTN_W] = k
        rest[0][:, ATTN_W:] = v
        return
    tm = x1.shape[0]
    j = pl.program_id(0) % tiles_per_seq
    for g, keep in enumerate(keeps):
        lanes = slice(g * GROUP_W, (g + 1) * GROUP_W)
        cols = min(keep, tm)

        @pl.when(j >= tiles_per_seq - pl.cdiv(keep, tm))
        def _(g=g, lanes=lanes, cols=cols):
            rest[g][0, :GROUP_W, :] = k[:, lanes].T[:, tm - cols:]
            rest[g][0, GROUP_W:, :] = v[:, lanes].T[:, tm - cols:]


def _ffn_in(x, w, tm, tiles_per_seq, prompt_form):
    rows = x.shape[0]
    n_tiles = rows // tm
    n_seq = n_tiles // tiles_per_seq
    row_spec = lambda width: pl.BlockSpec((tm, width), lambda i: (i, 0))
    if prompt_form:
        keeps = tuple(min(window, tiles_per_seq * tm) for window, _ in GROUPS)
        assert all(keep % tm == 0 or keep < tm for keep in keeps)

        def tail_spec(keep):
            first = tiles_per_seq - pl.cdiv(keep, tm)
            return pl.BlockSpec((1, 2 * GROUP_W, min(keep, tm)),
                                lambda i: (i // tiles_per_seq, 0, jnp.maximum(i % tiles_per_seq - first, 0)))

        tail_shapes = tuple(jax.ShapeDtypeStruct((n_seq, 2 * GROUP_W, keep), F32) for keep in keeps)
        tail_specs = tuple(tail_spec(keep) for keep in keeps)
        scratch = [pltpu.VMEM((3 * (len(GROUPS) - 1) * (GROUP_W // LANES), tm, LANES), F32)]
    else:
        assert tiles_per_seq == 1
        keeps = ()
        tail_shapes = (jax.ShapeDtypeStruct((rows, 2 * ATTN_W), F32),)
        tail_specs = (row_spec(2 * ATTN_W),)
        scratch = []
    out_shape = (jax.ShapeDtypeStruct((rows, D_MODEL), F32),
                 jax.ShapeDtypeStruct((rows, SSM_W), F32),
                 jax.ShapeDtypeStruct((rows, ATTN_W), BF16),
                 jax.ShapeDtypeStruct((rows, ATTN_W), BF16),
                 jax.ShapeDtypeStruct((rows, ATTN_W), BF16)) + tail_shapes
    outs = pl.pallas_call(
        functools.partial(_ffn_in_kernel, tiles_per_seq=tiles_per_seq, keeps=keeps),
        grid=(n_tiles,),
        in_specs=[row_spec(D_MODEL), _const_spec((1, D_MODEL)),
                  _const_spec((D_MODEL, D_FF)), _const_spec((D_MODEL, D_FF)), _const_spec((D_FF, D_MODEL)),
                  _const_spec((1, D_MODEL)), _const_spec((D_MODEL, IN_A)),
                  _const_spec((1, ATTN_W)), _const_spec((1, ATTN_W)), _const_spec((GROUP_W, GROUP_W))],
        out_specs=(row_spec(D_MODEL), row_spec(SSM_W), row_spec(ATTN_W), row_spec(ATTN_W), row_spec(ATTN_W))
                  + tail_specs,
        out_shape=out_shape,
        scratch_shapes=scratch,
        compiler_params=_params(1),
        name="ffn_in",
    )(x, w["g_ffn1"], w["w1_gate"], w["w1_up"], w["w1_down"], w["g_mix"], w["w_in_a"],
      w["g_q"], w["g_k"], w["seg_mean"])
    return outs[:5], (list(outs[5:]) if prompt_form else outs[5])


def _glu_out(y, u, dsk_ref, wglu_ref, bglu_ref):
    y = jax.nn.gelu(y + dsk_ref[...] * u)
    z = _dot(y.astype(BF16), wglu_ref[...]) + bglu_ref[...]
    return (y * _sigmoid(z)).astype(BF16)


def _s5_kernel(u0_ref, u1_ref, u2_ref, u3_ref, are_ref, aim_ref, k8_ref, w7_ref, gq_ref,
               dsk_ref, wglu_ref, bglu_ref, *rest, casts):
    n = len(casts)
    y_ref, hre_ref, him_ref = rest[n:n + 3]
    lsp_ref, yq_ref, y2s_ref, sp_ref = rest[2 * n + 3:]
    casts.run(rest[:n] + rest[n + 3:2 * n + 3])
    _s5_scan(u0_ref, u1_ref, u2_ref, u3_ref, are_ref, aim_ref, k8_ref, w7_ref, gq_ref,
             dsk_ref, wglu_ref, bglu_ref, y_ref, hre_ref, him_ref, lsp_ref, yq_ref, y2s_ref, sp_ref)


def _s5_scan(u0_ref, u1_ref, u2_ref, u3_ref, are_ref, aim_ref, k8_ref, w7_ref, gq_ref,
             dsk_ref, wglu_ref, bglu_ref, y_ref, hre_ref, him_ref,
             lsp_ref, yq_ref, y2s_ref, sp_ref):
    u_refs = (u0_ref, u1_ref, u2_ref, u3_ref)
    nb, t_blk, _ = u0_ref.shape
    nblk = t_blk // BLK
    tiles = QS // LANES

    @pl.when(pl.program_id(0) == 0)
    def _():
        sp_ref[...] = jnp.zeros_like(sp_ref)

    sub = jnp.bitwise_and(lax.broadcasted_iota(jnp.int32, (t_blk // SUBLANES, SUBLANES, LANES), 1), BLK - 1)
    for q in range(NQ):
        ublks = []
        for b in range(nb):
            uq = u_refs[q][b]
            z3 = _dot(uq.astype(BF16), k8_ref[q]).reshape(t_blk // SUBLANES, SUBLANES, BLK * LANES)
            acc = z3[:, :, :LANES]
            for i in range(1, BLK):
                zi = z3[:, :, LANES * i:LANES * (i + 1)]
                acc = acc + jnp.where(sub >= i, pltpu.roll(zi, i, axis=1), 0.0)
            yq_ref[b * NQ + q] = acc.reshape(t_blk, LANES)
            ublks.append(jnp.concatenate(
                [u_refs[q][b, pl.ds(r, nblk, stride=BLK), :] for r in range(BLK)], axis=1))
        ls = _dot(jnp.concatenate(ublks, axis=0).astype(BF16), w7_ref[q])
        for b in range(nb):
            for t in range(2 * tiles):
                dst = pl.ds((q % 2) * tiles + t % tiles, nblk, stride=SUBLANES)
                lsp_ref[b, t // tiles, q // 2, dst, :] = ls[b * nblk:(b + 1) * nblk, LANES * t:LANES * (t + 1)]

    chains = [(b, half) for b in range(nb) for half in range(2)]
    coef = {half: (are_ref[half], aim_ref[half]) for half in range(2)}
    state = {c: (sp_ref[c[0], 0, c[1]], sp_ref[c[0], 1, c[1]]) for c in chains}
    for k in range(nblk):
        rows = slice(SUBLANES * k, SUBLANES * (k + 1))
        for b, half in chains:
            a_re, a_im = coef[half]
            s_re, s_im = state[b, half]
            l_re = lsp_ref[b, 0, half, rows, :]
            l_im = lsp_ref[b, 1, half, rows, :]
            lsp_ref[b, 0, half, rows, :] = s_re
            lsp_ref[b, 1, half, rows, :] = s_im
            state[b, half] = (a_re * s_re - a_im * s_im + l_re, a_re * s_im + a_im * s_re + l_im)
    for (b, half), (s_re, s_im) in state.items():
        sp_ref[b, 0, half] = s_re
        sp_ref[b, 1, half] = s_im
        for j in range(SUBLANES):
            lanes = slice((half * SUBLANES + j) * LANES, (half * SUBLANES + j + 1) * LANES)
            hre_ref[b:b + 1, lanes] = s_re[j:j + 1]
            him_ref[b:b + 1, lanes] = s_im[j:j + 1]

    for q in range(NQ):
        sc = jnp.concatenate(
            [jnp.concatenate(
                [lsp_ref[b, t // tiles, q // 2, pl.ds((q % 2) * tiles + t % tiles, nblk, stride=SUBLANES), :]
                 for t in range(2 * tiles)], axis=1) for b in range(nb)], axis=0)
        y2 = _dot(sc.astype(BF16), gq_ref[q])
        for b in range(nb):
            for r in range(BLK):
                y2s_ref[pl.ds(r, nblk, stride=BLK), :] = y2[b * nblk:(b + 1) * nblk, LANES * r:LANES * (r + 1)]
            yq_ref[b * NQ + q] = yq_ref[b * NQ + q] + y2s_ref[...]

    for b in range(nb):
        y = jnp.concatenate([yq_ref[b * NQ + q] for q in range(NQ)], axis=1)
        u = jnp.concatenate([u_refs[q][b] for q in range(NQ)], axis=1)
        y_ref[b] = _glu_out(y, u, dsk_ref, wglu_ref, bglu_ref)


def _s5(u, w, t_blk, cast_items):
    nb, seq, _ = u.shape
    nblk = t_blk // BLK
    casts = _CastStream(cast_items, seq // t_blk, lambda c: c)
    u_specs = [pl.BlockSpec((nb, t_blk, LANES), lambda c, q=q: (0, c, q)) for q in range(NQ)]
    state = pl.BlockSpec((nb, N_STATE), lambda c: (0, 0))
    packed = lambda a: a.reshape(2, SUBLANES, LANES)
    y, h_re, h_im, *cast_outs = pl.pallas_call(
        functools.partial(_s5_kernel, casts=casts),
        grid=(seq // t_blk,),
        in_specs=u_specs + [_const_spec((2, SUBLANES, LANES)), _const_spec((2, SUBLANES, LANES)),
                            _const_spec((NQ, LANES, BLK * LANES)),
                            _const_spec((NQ, BLK * LANES, 2 * QS)), _const_spec((NQ, 2 * QS, BLK * LANES)),
                            _const_spec((1, SSM_W)), _const_spec((SSM_W, SSM_W)), _const_spec((1, SSM_W))]
                 + casts.in_specs,
        out_specs=(pl.BlockSpec((nb, t_blk, SSM_W), lambda c: (0, c, 0)), state, state)
                  + tuple(casts.out_specs),
        out_shape=(jax.ShapeDtypeStruct((nb, seq, SSM_W), BF16),
                   jax.ShapeDtypeStruct((nb, N_STATE), F32), jax.ShapeDtypeStruct((nb, N_STATE), F32))
                  + tuple(casts.out_shapes),
        scratch_shapes=[pltpu.VMEM((nb, 2, 2, nblk * SUBLANES, LANES), F32),
                        pltpu.VMEM((nb * NQ, t_blk, LANES), F32),
                        pltpu.VMEM((t_blk, LANES), F32),
                        pltpu.VMEM((nb, 2, 2, SUBLANES, LANES), F32)],
        compiler_params=_params(1),
        name="s5",
    )(u, u, u, u, packed(w["a8_re"]), packed(w["a8_im"]), w["k8"], w["w7"], w["gq"],
      w["ssm_d"], w["w_glu"], w["b_glu"], *casts.arrays)
    return y, h_re, h_im, cast_outs


def _s5_step_kernel(u_ref, hre_ref, him_ref, are_ref, aim_ref, k8_ref, w7_ref, gq_ref,
                    dsk_ref, wglu_ref, bglu_ref, y_ref, nre_ref, nim_ref):
    u = u_ref[...]
    ys = []
    for q in range(NQ):
        ch = slice(q * QS, (q + 1) * QS)
        uq = u[:, q * LANES:(q + 1) * LANES].astype(BF16)
        x = _dot(uq, w7_ref[q, (BLK - 1) * LANES:, :])
        a_re, a_im = are_ref[:, ch], aim_ref[:, ch]
        s_re, s_im = hre_ref[:, ch], him_ref[:, ch]
        nre_ref[:, ch] = a_re * s_re - a_im * s_im + x[:, :QS]
        nim_ref[:, ch] = a_re * s_im + a_im * s_re + x[:, QS:]
        s = jnp.concatenate([s_re, s_im], axis=1).astype(BF16)
        ys.append(_dot(s, gq_ref[q, :, :LANES]) + _dot(uq, k8_ref[q, :, :LANES]))
    y_ref[...] = _glu_out(jnp.concatenate(ys, axis=1), u, dsk_ref, wglu_ref, bglu_ref)


def _s5_step(u, h_re, h_im, w):
    rows = u.shape[0]
    return pl.pallas_call(
        _s5_step_kernel,
        out_shape=(jax.ShapeDtypeStruct((rows, SSM_W), BF16),
                   jax.ShapeDtypeStruct((rows, N_STATE), F32), jax.ShapeDtypeStruct((rows, N_STATE), F32)),
        compiler_params=pltpu.CompilerParams(vmem_limit_bytes=VMEM_LIMIT),
        name="s5_step",
    )(u, h_re, h_im, w["ab_re"], w["ab_im"], w["k8"], w["w7"], w["gq"], w["ssm_d"], w["w_glu"], w["b_glu"])


def _step_attention(q, kn, vn, cache, dil):
    n_feat, window = cache.shape
    row8 = lax.broadcasted_iota(jnp.int32, (8, GROUP_W), 0)
    head8 = lax.broadcasted_iota(jnp.int32, (8, GROUP_W), 1) // HEAD_DIM == row8
    diag8 = (lax.broadcasted_iota(jnp.int32, (8, LANES), 0)
             == lax.broadcasted_iota(jnp.int32, (8, LANES), 1))
    q8 = jnp.where(head8, jnp.broadcast_to(q, (8, GROUP_W)), 0.0)
    pos = lax.broadcasted_iota(jnp.int32, (8, window), 1)
    s = _dot(q8.astype(BF16), cache[:GROUP_W].astype(BF16))
    s = jnp.where(jnp.bitwise_and(pos, dil - 1) == 0, s, NEG)
    s_new = jnp.sum(q8 * kn, axis=-1, keepdims=True)
    m = jnp.maximum(jnp.max(s, axis=-1, keepdims=True), s_new)
    p = jnp.exp(s - m)
    p_new = jnp.exp(s_new - m)
    den = jnp.sum(p, axis=-1, keepdims=True) + p_new
    o8 = (_dot_nt(p.astype(BF16), cache[GROUP_W:].astype(BF16)) + p_new * vn) / den
    o = jnp.sum(jnp.where(head8, o8, 0.0), axis=0, keepdims=True)
    lse = jnp.sum(jnp.where(diag8, m + jnp.log(den), 0.0), axis=0, keepdims=True)
    eye = (lax.broadcasted_iota(jnp.int32, (n_feat, n_feat), 0)
           == lax.broadcasted_iota(jnp.int32, (n_feat, n_feat), 1))
    new_col = jnp.sum(jnp.where(eye, jnp.concatenate([kn, vn], axis=1), 0.0), axis=-1, keepdims=True)
    rolled = pltpu.roll(cache, window - 1, axis=1)
    last = lax.broadcasted_iota(jnp.int32, (n_feat, window), 1) == window - 1
    return o, lse, jnp.where(last, new_col, rolled)


def _swa_kernel(q_ref, k_ref, v_ref, kp_ref, vp_ref, qs_ref, ks_ref, vs_ref, c_ref, *rest,
                qb, chunk, dil, casts):
    n = len(casts)
    o_ref, l_ref, os_ref, ls_ref, n_ref = rest[n:n + 5]
    casts.run(rest[:n] + rest[n + 5:])
    first = pl.program_id(2) == 0

    def rows(ref):
        return jnp.concatenate([ref[t] for t in range(ref.shape[0])], axis=0)

    q = rows(q_ref)
    kk = jnp.concatenate([rows(kp_ref), rows(k_ref)], axis=0)
    vv = jnp.concatenate([rows(vp_ref), rows(v_ref)], axis=0)
    nk = 2 * KEYS_PER_QUERY
    row = lax.broadcasted_iota(jnp.int32, (128, nk), 0)
    col = lax.broadcasted_iota(jnp.int32, (128, nk), 1)
    band = jnp.where(col >= row, jnp.where(col <= row + KEYS_PER_QUERY, 0.0, NEG), NEG)
    band_first = band + jnp.where(col < KEYS_PER_QUERY, jnp.where(first, NEG, 0.0), 0.0)
    lane = lax.broadcasted_iota(jnp.int32, (1, GROUP_W), 1) // HEAD_DIM
    lane_l = lax.broadcasted_iota(jnp.int32, (1, LANES), 1)
    for j in range(qb // 128):
        qj = q[128 * j:128 * (j + 1)]
        kj = kk[128 * j:128 * j + nk]
        vj = vv[128 * j:128 * j + nk]
        bias = band_first if j == 0 else band
        qs = jnp.concatenate([jnp.where(lane == h, qj, jnp.zeros_like(qj)) for h in range(HEADS)], axis=0)
        s = (_dot_nt(qs, kj).reshape(HEADS, 128, nk) + bias[None]).reshape(HEADS * 128, nk)
        m = jnp.max(s, axis=-1, keepdims=True)
        p = jnp.exp(s - m)
        den = jnp.sum(p, axis=-1, keepdims=True)
        on = _dot(p.astype(BF16), vj) * (1.0 / den)
        lse = m + jnp.log(den)
        o_acc = jnp.zeros((128, GROUP_W), F32)
        l_acc = jnp.zeros((128, LANES), F32)
        for h in range(HEADS):
            rows_h = slice(128 * h, 128 * (h + 1))
            o_acc = jnp.where(lane == h, on[rows_h], o_acc)
            l_acc = jnp.where(lane_l == h, lse[rows_h], l_acc)
        o_out = o_acc.astype(BF16)
        if chunk >= 128:
            t, r0 = divmod(128 * j, chunk)
            o_ref[t, r0:r0 + 128, :] = o_out
            l_ref[t, r0:r0 + 128, :] = l_acc
        else:
            for s_ in range(128 // chunk):
                t = (128 * j) // chunk + s_
                o_ref[t] = o_out[s_ * chunk:(s_ + 1) * chunk]
                l_ref[t] = l_acc[s_ * chunk:(s_ + 1) * chunk]

    def sample_step():
        o, lse, new_cache = _step_attention(qs_ref[0].astype(F32), ks_ref[0], vs_ref[0], c_ref[0], dil)
        os_ref[0] = o.astype(BF16)
        ls_ref[0] = lse
        n_ref[0] = new_cache

    if c_ref.shape[-1] >= OWN_REGION_WINDOW:
        pl.when(pl.program_id(0) >= 0)(sample_step)
    else:
        sample_step()


def _swa(q, k, v, g, tile, q_s, kv_s, cache, cast_items):
    window, dil = GROUPS[g]
    nb, seq, _ = q.shape
    n_sample = q_s.shape[0]
    n_tiles = seq // tile
    chunk = tile // dil
    n = seq // dil
    qb = min(512, n)
    tpb = qb // chunk
    n_i = n // qb
    assert nb * dil * n_i >= n_sample and cache.shape == (n_sample, 2 * GROUP_W, window)
    view = lambda a: a.reshape(nb, n_tiles, dil, chunk, a.shape[-1])
    cur = pl.BlockSpec((None, tpb, None, chunk, GROUP_W), lambda b, r, i: (b, i, r, 0, g))
    if chunk >= 128:
        prev = pl.BlockSpec((None, 1, None, 128, GROUP_W),
                            lambda b, r, i: (b, jnp.maximum(i * tpb - 1, 0), r, chunk // 128 - 1, g))
    else:
        ptiles = 128 // chunk
        prev = pl.BlockSpec((None, ptiles, None, chunk, GROUP_W),
                            lambda b, r, i: (b, jnp.maximum(i * (tpb // ptiles) - 1, 0), r, 0, g))
    out_block = lambda width: pl.BlockSpec((None, tpb, None, chunk, width), lambda b, r, i: (b, i, r, 0, 0))
    step_of = lambda b, r, i: (b * dil + r) * n_i + i
    sample_of = lambda b, r, i: jnp.minimum(step_of(b, r, i), n_sample - 1)
    sample_block = lambda *dims: pl.BlockSpec((1,) + dims, lambda b, r, i: (sample_of(b, r, i),) + (0,) * len(dims))
    group_lanes = lambda col: pl.BlockSpec((1, 1, GROUP_W), lambda b, r, i: (sample_of(b, r, i), 0, col))
    casts = _CastStream(cast_items, nb * dil * n_i, step_of)
    o, lse, o_s, l_s, new_cache, *cast_outs = pl.pallas_call(
        functools.partial(_swa_kernel, qb=qb, chunk=chunk, dil=dil, casts=casts),
        grid=(nb, dil, n_i),
        in_specs=[cur, cur, cur, prev, prev,
                  group_lanes(g), group_lanes(g), group_lanes(ATTN_W // GROUP_W + g),
                  sample_block(2 * GROUP_W, window)] + casts.in_specs,
        out_specs=(out_block(GROUP_W), out_block(LANES),
                   sample_block(1, GROUP_W), sample_block(1, LANES), sample_block(2 * GROUP_W, window))
                  + tuple(casts.out_specs),
        out_shape=(jax.ShapeDtypeStruct((nb, n_tiles, dil, chunk, GROUP_W), BF16),
                   jax.ShapeDtypeStruct((nb, n_tiles, dil, chunk, LANES), F32),
                   jax.ShapeDtypeStruct((n_sample, 1, GROUP_W), BF16),
                   jax.ShapeDtypeStruct((n_sample, 1, LANES), F32),
                   jax.ShapeDtypeStruct(cache.shape, F32)) + tuple(casts.out_shapes),
        compiler_params=_params(3),
        name="swa_g%d" % g,
    )(view(q), view(k), view(v), view(k), view(v),
      q_s.reshape(n_sample, 1, ATTN_W), kv_s.reshape(n_sample, 1, 2 * ATTN_W),
      kv_s.reshape(n_sample, 1, 2 * ATTN_W), cache, *casts.arrays)
    return (o.reshape(nb * seq, GROUP_W), lse.reshape(nb * seq, LANES),
            o_s.reshape(n_sample, GROUP_W), l_s.reshape(n_sample, LANES), new_cache, cast_outs)


def _position_order(o_ref, l_ref, dil, nat_ref, slab0):
    tm = o_ref.shape[0]
    n = tm // dil
    for r in range(dil):
        rows = slice(r * n, (r + 1) * n)
        dst = pl.ds(r, n, stride=dil)
        nat_ref[slab0, dst, :] = o_ref[rows, :LANES].astype(F32)
        nat_ref[slab0 + 1, dst, :] = o_ref[rows, LANES:].astype(F32)
        nat_ref[slab0 + 2, dst, :] = l_ref[rows, :]
    o = jnp.concatenate([nat_ref[slab0], nat_ref[slab0 + 1]], axis=1)
    return o, nat_ref[slab0 + 2]


def _mix_ffn_kernel(x1_ref, ys_ref, o0_ref, o1_ref, o2_ref, l0_ref, l1_ref, l2_ref,
                    gm_ref, wgt_ref, wsp_ref, wap_ref, wo_ref, seget_ref,
                    g2_ref, wg_ref, wu_ref, wd_ref, y_ref, *scratch):
    x1 = x1_ref[...]
    h = _rms(x1, gm_ref[...]).astype(BF16)
    gates = _sigmoid(_dot(h, wgt_ref[...]))
    os_ = [o0_ref[...].astype(F32)]
    ls = [l0_ref[...]]
    for g, (o_ref, l_ref) in enumerate(((o1_ref, l1_ref), (o2_ref, l2_ref)), start=1):
        if scratch:
            o, l = _position_order(o_ref, l_ref, GROUPS[g][1], scratch[0], 3 * (g - 1))
        else:
            o, l = o_ref[...].astype(F32), l_ref[...]
        os_.append(o)
        ls.append(l)
    l_top = jnp.maximum(jnp.maximum(ls[0], ls[1]), ls[2])
    es = [jnp.exp(l - l_top) for l in ls]
    inv = 1.0 / (es[0] + es[1] + es[2])
    seget = seget_ref[...]
    y_attn = None
    for e, o in zip(es, os_):
        t = _dot((e * inv).astype(BF16), seget) * o
        y_attn = t if y_attn is None else y_attn + t
    mixed = (gates[:, :D_MODEL] * _dot(ys_ref[...], wsp_ref[...])
             + gates[:, D_MODEL:] * _dot(y_attn.astype(BF16), wap_ref[...]))
    x2 = x1 + _dot(mixed.astype(BF16), wo_ref[...])
    y_ref[...] = _ffn(x2, g2_ref, wg_ref, wu_ref, wd_ref)


def _mix_ffn(x1, y_ssm, os_, ls_, w, tm, residue_major):
    rows = x1.shape[0]
    row_spec = lambda width: pl.BlockSpec((tm, width), lambda i: (i, 0))
    n_slabs = 3 * (len(GROUPS) - 1)
    scratch = [pltpu.VMEM((n_slabs, tm, LANES), F32)] if residue_major else []
    return pl.pallas_call(
        _mix_ffn_kernel,
        grid=(rows // tm,),
        in_specs=[row_spec(D_MODEL), row_spec(SSM_W)] + [row_spec(GROUP_W)] * 3 + [row_spec(LANES)] * 3
                 + [_const_spec((1, D_MODEL)), _const_spec((D_MODEL, 2 * D_MODEL)),
                    _const_spec((SSM_W, D_MODEL)), _const_spec((GROUP_W, D_MODEL)),
                    _const_spec((D_MODEL, D_MODEL)), _const_spec((LANES, GROUP_W)),
                    _const_spec((1, D_MODEL)), _const_spec((D_MODEL, D_FF)), _const_spec((D_MODEL, D_FF)),
                    _const_spec((D_FF, D_MODEL))],
        out_specs=row_spec(D_MODEL),
        out_shape=jax.ShapeDtypeStruct((rows, D_MODEL), F32),
        scratch_shapes=scratch,
        compiler_params=_params(1),
        name="mix_ffn",
    )(x1, y_ssm, *os_, *ls_, w["g_mix"], w["w_gates"], w["w_ssm_proj"], w["w_attn_proj"], w["w_o"],
      w["seg_et"], w["g_ffn2"], w["w2_gate"], w["w2_up"], w["w2_down"])


def _prepare_weights(g_ffn1, w1_gate, w1_up, w1_down, g_mix, w_in, g_q, g_k,
                     ssm_a_re, ssm_a_im, ssm_log_dt, ssm_b_re, ssm_b_im, ssm_c_re, ssm_c_im,
                     ssm_d, w_glu, b_glu, w_ssm_proj, w_attn_proj, w_o, g_ffn2, w2_gate, w2_up, w2_down):
    w = {}
    row = lambda a: a.reshape(1, -1).astype(F32)
    w["g_ffn1"], w["g_mix"], w["g_ffn2"] = row(g_ffn1), row(g_mix), row(g_ffn2)
    w["w_glu"] = w_glu.astype(BF16)
    f32 = lambda a: a.astype(F32)
    w["late"] = {
        "s5": (("w2_down", f32(w2_down), 0, D_MODEL), ("w_o", f32(w_o), 0, D_MODEL),
               ("w_ssm_proj", f32(w_ssm_proj), 0, D_MODEL), ("w_attn_proj", f32(w_attn_proj), 0, D_MODEL),
               ("w_gates", f32(w_in), IN_A, 2 * D_MODEL)),
        "swa": ((("w2_gate", f32(w2_gate), 0, D_FF), ("w2_up", f32(w2_up), 0, D_FF)), (), ())}
    per_head = lambda g: jnp.broadcast_to(g[:, None, :], (len(GROUPS), HEADS, HEAD_DIM)).reshape(1, ATTN_W)
    w["g_q"], w["g_k"] = per_head(g_q.astype(F32)), per_head(g_k.astype(F32))
    head_of_lane = jnp.arange(GROUP_W) // HEAD_DIM
    w["seg_mean"] = ((head_of_lane[:, None] == head_of_lane[None, :]) / HEAD_DIM).astype(BF16)
    w["seg_et"] = (jnp.arange(LANES)[:, None] == head_of_lane[None, :]).astype(BF16)
    w["ssm_d"], w["b_glu"] = row(ssm_d), row(b_glu)

    early = (("w1_gate", f32(w1_gate), 0, D_FF), ("w1_up", f32(w1_up), 0, D_FF),
             ("w1_down", f32(w1_down), 0, D_MODEL), ("w_in_a", f32(w_in), 0, IN_A))
    (pw, w["k8"], w["w7"], w["gq"]), converted = _ssm_prep(
        ssm_a_re, ssm_a_im, ssm_log_dt, ssm_b_re, ssm_b_im, ssm_c_re, ssm_c_im, [e[1:] for e in early])
    w.update({e[0]: c for e, c in zip(early, converted)})
    per_group = lambda a: a.reshape(SSM_GROUPS, SSM_CH, SSM_P)[:, 0].reshape(1, N_STATE)
    w["ab_re"], w["ab_im"] = per_group(pw[0, 0]), per_group(pw[0, 1])
    w["a8_re"], w["a8_im"] = per_group(pw[1, 0]), per_group(pw[1, 1])
    return w


def _forward(x_p, x_s, state_s, caches_s, w):
    nb, seq, _ = x_p.shape
    n_s = x_s.shape[0]
    assert seq % TILE == 0 and x_s.shape[1] == 1
    (x1, u, q, k, v), caches_p = _ffn_in(x_p.reshape(nb * seq, D_MODEL), w, TILE, seq // TILE, True)
    (x1_s, u_s, q_s, _, _), kv_s = _ffn_in(x_s.reshape(n_s, D_MODEL), w, n_s, 1, False)
    late = w["late"]["s5"]
    y_ssm, *h_p, converted = _s5(u.reshape(nb, seq, SSM_W), w, min(S5_T, seq), [e[1:] for e in late])
    w = {**w, **{e[0]: c for e, c in zip(late, converted)}}
    y_ssm_s, *h_s = _s5_step(u_s, state_s[0], state_s[1], w)
    q3, k3, v3 = (a.reshape(nb, seq, ATTN_W) for a in (q, k, v))
    os_p, ls_p, os_s, ls_s, new_caches_s = [], [], [], [], []
    for g in range(len(GROUPS)):
        late = w["late"]["swa"][g]
        o, l, o_s, l_s, c_new, converted = _swa(q3, k3, v3, g, TILE, q_s, kv_s, caches_s[g],
                                                [e[1:] for e in late])
        w = {**w, **{e[0]: c for e, c in zip(late, converted)}}
        for lst, a in zip((os_p, ls_p, os_s, ls_s, new_caches_s), (o, l, o_s, l_s, c_new)):
            lst.append(a)
    y_p = _mix_ffn(x1, y_ssm.reshape(nb * seq, SSM_W), os_p, ls_p, w, TILE, True)
    y_s = _mix_ffn(x1_s, y_ssm_s, os_s, ls_s, w, n_s, False)
    return (y_p.reshape(nb, seq, D_MODEL), y_s.reshape(n_s, 1, D_MODEL), h_p, h_s, caches_p, new_caches_s)


def _to_features_major(c):
    nb, win = c.shape[:2]
    return jnp.transpose(c, (0, 2, 3, 4, 1)).reshape(nb, 2 * GROUP_W, win)


def _to_window_buffer(c):
    nb, _, win = c.shape
    return jnp.transpose(c.reshape(nb, 2, HEADS, HEAD_DIM, win), (0, 4, 1, 2, 3))[None]


def kernel(x_prompt, x_sample, cache_kv_w128, cache_kv_w512, cache_kv_w2048, state_ssm_re, state_ssm_im, g_ffn1, w1_gate, w1_up, w1_down, g_mix, w_in, g_q, g_k, ssm_a_re, ssm_a_im, ssm_log_dt, ssm_b_re, ssm_b_im, ssm_c_re, ssm_c_im, ssm_d, w_glu, b_glu, w_ssm_proj, w_attn_proj, w_o, g_ffn2, w2_gate, w2_up, w2_down):
    layer_weights = (g_ffn1, w1_gate, w1_up, w1_down, g_mix, w_in, g_q, g_k,
                     ssm_a_re, ssm_a_im, ssm_log_dt, ssm_b_re, ssm_b_im, ssm_c_re, ssm_c_im,
                     ssm_d, w_glu, b_glu, w_ssm_proj, w_attn_proj, w_o, g_ffn2, w2_gate, w2_up, w2_down)
    depth = g_ffn1.shape[0]
    assert depth == 1, "window caches of deeper layers would need the previous layer's outputs"
    w = _prepare_weights(*(a[0] for a in layer_weights))
    nb_s = x_sample.shape[0]
    sdt = state_ssm_re.dtype
    as_state = lambda h: h.reshape(1, -1, SSM_GROUPS, SSM_P).astype(sdt)

    caches = [_to_features_major(c[0]) for c in (cache_kv_w128, cache_kv_w512, cache_kv_w2048)]
    state_s = (state_ssm_re[0].reshape(nb_s, N_STATE).astype(F32),
               state_ssm_im[0].reshape(nb_s, N_STATE).astype(F32))
    y_p, y_s, h_p, h_s, kv_p, kv_s = _forward(x_prompt, x_sample, state_s, caches, w)
    return ((y_p, y_s) + tuple(_to_window_buffer(c) for c in kv_p) + (as_state(h_p[0]), as_state(h_p[1]))
            + tuple(_to_window_buffer(c) for c in kv_s) + (as_state(h_s[0]), as_state(h_s[1])))
```

```python
import functools

import jax
import jax.numpy as jnp
from jax import lax
from jax.experimental import pallas as pl
from jax.experimental.pallas import tpu as pltpu

F32 = jnp.float32
BF16 = jnp.bfloat16

D_MODEL = 1024
D_FF = 2816
HEAD_DIM = 64
HEADS = 4
GROUPS = ((128, 1), (512, 4), (2048, 16))
KEYS_PER_QUERY = 128
GROUP_W = HEADS * HEAD_DIM
ATTN_W = len(GROUPS) * GROUP_W
SSM_W = 512
SSM_GROUPS = 32
SSM_CH = 16
SSM_P = 64
N_STATE = SSM_GROUPS * SSM_P
IN_A = SSM_W + 3 * ATTN_W
RMS_EPS = 1e-6
NEG = -1e30
LANES = 128
SUBLANES = 8

TILE = 512
FF_CHUNKS = ((0, 1024), (1024, 1024), (2048, 768))
BLK = 4
NQ = SSM_W // LANES
QS = N_STATE // NQ
S5_T = 1024
OWN_REGION_WINDOW = 2048
CACHE_BUFS = 3
VMEM_LIMIT = 56 * 1024 * 1024


def _const_spec(shape):
    nd = len(shape)
    return pl.BlockSpec(shape, lambda *_: (0,) * nd, pipeline_mode=pl.Buffered(1))


def _params(n_grid):
    return pltpu.CompilerParams(dimension_semantics=("arbitrary",) * n_grid,
                                vmem_limit_bytes=VMEM_LIMIT)


def _rms(x, g):
    ms = jnp.mean(x * x, axis=-1, keepdims=True)
    return x * lax.rsqrt(ms + RMS_EPS) * g


def _dot(a, b):
    return jnp.dot(a, b, preferred_element_type=F32)


def _dot_nt(a, b):
    return lax.dot_general(a, b, (((1,), (1,)), ((), ())), preferred_element_type=F32)


class _CastStream:
    def __init__(self, items, n_steps, step_of):
        self.cols = [(c0, cw) for _, c0, cw in items]
        self.arrays = [a for a, _, _ in items]
        self.in_specs, self.out_specs, self.out_shapes = [], [], []
        for a, c0, cw in items:
            rows, width = a.shape
            chunks = n_steps
            while rows % chunks or (rows // chunks) % 16:
                assert chunks % 2 == 0, (a.shape, n_steps)
                chunks //= 2
            index = lambda *g, hold=n_steps // chunks: (step_of(*g) // hold, 0)
            if c0 == 0:
                width = cw
            self.in_specs.append(pl.BlockSpec((rows // chunks, width), index))
            self.out_specs.append(pl.BlockSpec((rows // chunks, cw), index))
            self.out_shapes.append(jax.ShapeDtypeStruct((rows, cw), BF16))

    def __len__(self):
        return len(self.cols)

    def run(self, refs):
        n = len(self.cols)
        for i_ref, o_ref, (c0, cw) in zip(refs[:n], refs[n:], self.cols):
            o_ref[...] = i_ref[:, c0:c0 + cw].astype(BF16)


def _sigmoid(x):
    return 0.5 * jnp.tanh(0.5 * x) + 0.5


def _ffn(x, g_ref, wg_ref, wu_ref, wd_ref):
    xn = _rms(x, g_ref[...]).astype(BF16)
    acc = None
    for f0, fw in FF_CHUNKS:
        hg = _dot(xn, wg_ref[:, f0:f0 + fw])
        hu = _dot(xn, wu_ref[:, f0:f0 + fw])
        a = (hg * _sigmoid(hg) * hu).astype(BF16)
        d = _dot(a, wd_ref[f0:f0 + fw, :])
        acc = d if acc is None else acc + d
    return x + 0.5 * acc


def _split3_dot_nt(a, b):
    a_hi = a.astype(BF16)
    a_lo = (a - a_hi.astype(F32)).astype(BF16)
    b_hi = b.astype(BF16)
    b_lo = (b - b_hi.astype(F32)).astype(BF16)
    return _dot_nt(a_hi, b_hi) + _dot_nt(a_hi, b_lo) + _dot_nt(a_lo, b_hi)


def _prep_kernel(are_ref, aim_ref, ldt_ref, bre_ref, bim_ref, cre_ref, cim_ref, *rest, casts):
    n = len(casts)
    pw_ref, k8_ref, w7_ref, gq_ref = rest[n:n + 4]
    casts.run(rest[:n] + rest[n + 4:])
    _prep_quarter(are_ref, aim_ref, ldt_ref, bre_ref, bim_ref, cre_ref, cim_ref,
                  pw_ref, k8_ref, w7_ref, gq_ref)


def _prep_quarter(are_ref, aim_ref, ldt_ref, bre_ref, bim_ref, cre_ref, cim_ref,
                  pw_ref, k8_ref, w7_ref, gq_ref):
    a_re = are_ref[...]
    a_im = aim_ref[...]
    dt = jnp.exp(ldt_ref[...])
    mag = jnp.exp(a_re * dt)
    ab_re = mag * jnp.cos(a_im * dt)
    ab_im = mag * jnp.sin(a_im * dt)
    inv = 1.0 / (a_re * a_re + a_im * a_im)
    f_re = ((ab_re - 1.0) * a_re + ab_im * a_im) * inv
    f_im = (ab_im * a_re - (ab_re - 1.0) * a_im) * inv
    b_re = bre_ref[...]
    b_im = bim_ref[...]
    bb_re = f_re * b_re - f_im * b_im
    bb_im = f_re * b_im + f_im * b_re
    c_re = cre_ref[...]
    c_im = cim_ref[...]
    same_group = (lax.broadcasted_iota(jnp.int32, (LANES, LANES), 0) // SSM_CH
                  == lax.broadcasted_iota(jnp.int32, (LANES, LANES), 1) // SSM_CH)
    spread = (lax.broadcasted_iota(jnp.int32, (SSM_P, QS), 1) % SSM_P
              == lax.broadcasted_iota(jnp.int32, (SSM_P, QS), 0)).astype(BF16)
    spread_t = (lax.broadcasted_iota(jnp.int32, (QS, SSM_P), 0) % SSM_P
                == lax.broadcasted_iota(jnp.int32, (QS, SSM_P), 1)).astype(BF16)
    mask_w = (lax.broadcasted_iota(jnp.int32, (LANES, QS), 0) // SSM_CH
              == lax.broadcasted_iota(jnp.int32, (LANES, QS), 1) // SSM_P)
    mask_g = (lax.broadcasted_iota(jnp.int32, (QS, LANES), 0) // SSM_P
              == lax.broadcasted_iota(jnp.int32, (QS, LANES), 1) // SSM_CH)
    p_re = jnp.ones_like(ab_re)
    p_im = jnp.zeros_like(ab_re)
    for i in range(BLK + 1):
        if i == 1:
            pw_ref[0, 0] = p_re
            pw_ref[0, 1] = p_im
        if i == BLK:
            pw_ref[1, 0] = p_re
            pw_ref[1, 1] = p_im
        if i >= 1:
            r = i - 1
            gcs = (c_re * p_re - c_im * p_im, -(c_re * p_im + c_im * p_re))
            for part in range(2):
                blk = jnp.where(mask_g, _dot_nt(spread_t, gcs[part].astype(BF16)), 0.0)
                gq_ref[0, part * QS:(part + 1) * QS, r * LANES:(r + 1) * LANES] = blk.astype(BF16)
        if i < BLK:
            r = BLK - 1 - i
            xs = (p_re * bb_re - p_im * bb_im, p_re * bb_im + p_im * bb_re)
            kk = jnp.where(same_group, _split3_dot_nt(xs[0], c_re) - _split3_dot_nt(xs[1], c_im), 0.0)
            k8_ref[0, :, i * LANES:(i + 1) * LANES] = kk.astype(BF16)
            for part in range(2):
                blk = jnp.where(mask_w, _dot(xs[part].astype(BF16), spread), 0.0)
                w7_ref[0, r * LANES:(r + 1) * LANES, part * QS:(part + 1) * QS] = blk.astype(BF16)
        p_re, p_im = p_re * ab_re - p_im * ab_im, p_re * ab_im + p_im * ab_re


def _ssm_prep(a_re, a_im, log_dt, b_re, b_im, c_re, c_im, cast_items):
    gc = SSM_GROUPS * SSM_CH
    casts = _CastStream(cast_items, NQ, lambda q: q)
    rep = lambda a: jnp.repeat(a.astype(F32), SSM_CH, axis=0)
    b_rows = lambda b: jnp.transpose(b.astype(F32), (0, 2, 1)).reshape(gc, SSM_P)
    c_rows = lambda c: c.astype(F32).reshape(gc, SSM_P)
    rows = lambda width: pl.BlockSpec((LANES, width), lambda q: (q, 0))
    per_q = lambda d1, d2: pl.BlockSpec((1, d1, d2), lambda q: (q, 0, 0))
    outs = pl.pallas_call(
        functools.partial(_prep_kernel, casts=casts),
        grid=(NQ,),
        in_specs=[rows(SSM_P), rows(SSM_P), rows(1)] + [rows(SSM_P)] * 4 + casts.in_specs,
        out_specs=(pl.BlockSpec((2, 2, LANES, SSM_P), lambda q: (0, 0, q, 0)),
                   per_q(LANES, BLK * LANES), per_q(BLK * LANES, 2 * QS), per_q(2 * QS, BLK * LANES))
                  + tuple(casts.out_specs),
        out_shape=(jax.ShapeDtypeStruct((2, 2, gc, SSM_P), F32),
                   jax.ShapeDtypeStruct((NQ, LANES, BLK * LANES), BF16),
                   jax.ShapeDtypeStruct((NQ, BLK * LANES, 2 * QS), BF16),
                   jax.ShapeDtypeStruct((NQ, 2 * QS, BLK * LANES), BF16)) + tuple(casts.out_shapes),
        compiler_params=_params(1),
        name="ssm_prep",
    )(rep(a_re), rep(a_im), rep(log_dt.reshape(SSM_GROUPS, 1)), b_rows(b_re), b_rows(b_im),
      c_rows(c_re), c_rows(c_im), *casts.arrays)
    return outs[:4], outs[4:]


def _head_norm(t, g, seg):
    sq = (t * t).astype(BF16)
    ms = jnp.concatenate(
        [_dot(sq[:, s * GROUP_W:(s + 1) * GROUP_W], seg) for s in range(len(GROUPS))], axis=1)
    return t * lax.rsqrt(ms + RMS_EPS) * g


def _store_attn_rows(val, out_ref, perm_ref, slab0):
    tm = val.shape[0]
    out_ref[:, :GROUP_W] = val[:, :GROUP_W].astype(BF16)
    for g in range(1, len(GROUPS)):
        dil = GROUPS[g][1]
        n = tm // dil
        for half in range(GROUP_W // LANES):
            c0 = g * GROUP_W + half * LANES
            col = val[:, c0:c0 + LANES]
            if perm_ref is None:
                out_ref[:, c0:c0 + LANES] = col.astype(BF16)
                continue
            slab = slab0 + (g - 1) * (GROUP_W // LANES) + half
            perm_ref[slab] = col
            for r in range(dil):
                out_ref[r * n:(r + 1) * n, c0:c0 + LANES] = (
                    perm_ref[slab, pl.ds(r, n, stride=dil), :].astype(BF16))


def _ffn_in_kernel(x_ref, g1_ref, wg_ref, wu_ref, wd_ref, gm_ref, win_ref, gq_ref, gk_ref, seg_ref,
                   x1_ref, u_ref, q_ref, k_ref, v_ref, *rest, tiles_per_seq, keeps):
    perm_ref = rest[-1] if len(rest) > 1 else None
    x1 = _ffn(x_ref[...], g1_ref, wg_ref, wu_ref, wd_ref)
    x1_ref[...] = x1
    h = _rms(x1, gm_ref[...]).astype(BF16)
    proj = _dot(h, win_ref[...])
    u_ref[...] = proj[:, :SSM_W]
    seg = seg_ref[...]
    q = _head_norm(proj[:, SSM_W:SSM_W + ATTN_W], gq_ref[...], seg)
    k = _head_norm(proj[:, SSM_W + ATTN_W:SSM_W + 2 * ATTN_W], gk_ref[...], seg)
    v = proj[:, SSM_W + 2 * ATTN_W:]
    slabs = (len(GROUPS) - 1) * (GROUP_W // LANES)
    _store_attn_rows(q * HEAD_DIM ** -0.5, q_ref, perm_ref, 0)
    _store_attn_rows(k, k_ref, perm_ref, slabs)
    _store_attn_rows(v, v_ref, perm_ref, 2 * slabs)
    if perm_ref is None:
        rest[0][:, :ATTN_W] = k
        rest[0][:, ATTN_W:] = v
        return
    tm = x1.shape[0]
    j = pl.program_id(0) % tiles_per_seq
    for g, keep in enumerate(keeps):
        lanes = slice(g * GROUP_W, (g + 1) * GROUP_W)
        cols = min(keep, tm)

        @pl.when(j >= tiles_per_seq - pl.cdiv(keep, tm))
        def _(g=g, lanes=lanes, cols=cols):
            rest[g][0, :GROUP_W, :] = k[:, lanes].T[:, tm - cols:]
            rest[g][0, GROUP_W:, :] = v[:, lanes].T[:, tm - cols:]


def _ffn_in(x, w, tm, tiles_per_seq, prompt_form):
    rows = x.shape[0]
    n_tiles = rows // tm
    n_seq = n_tiles // tiles_per_seq
    row_spec = lambda width: pl.BlockSpec((tm, width), lambda i: (i, 0))
    if prompt_form:
        keeps = tuple(min(window, tiles_per_seq * tm) for window, _ in GROUPS)
        assert all(keep % tm == 0 or keep < tm for keep in keeps)

        def tail_spec(keep):
            first = tiles_per_seq - pl.cdiv(keep, tm)
            return pl.BlockSpec((1, 2 * GROUP_W, min(keep, tm)),
                                lambda i: (i // tiles_per_seq, 0, jnp.maximum(i % tiles_per_seq - first, 0)))

        tail_shapes = tuple(jax.ShapeDtypeStruct((n_seq, 2 * GROUP_W, keep), F32) for keep in keeps)
        tail_specs = tuple(tail_spec(keep) for keep in keeps)
        scratch = [pltpu.VMEM((3 * (len(GROUPS) - 1) * (GROUP_W // LANES), tm, LANES), F32)]
    else:
        assert tiles_per_seq == 1
        keeps = ()
        tail_shapes = (jax.ShapeDtypeStruct((rows, 2 * ATTN_W), F32),)
        tail_specs = (row_spec(2 * ATTN_W),)
        scratch = []
    out_shape = (jax.ShapeDtypeStruct((rows, D_MODEL), F32),
                 jax.ShapeDtypeStruct((rows, SSM_W), F32),
                 jax.ShapeDtypeStruct((rows, ATTN_W), BF16),
                 jax.ShapeDtypeStruct((rows, ATTN_W), BF16),
                 jax.ShapeDtypeStruct((rows, ATTN_W), BF16)) + tail_shapes
    outs = pl.pallas_call(
        functools.partial(_ffn_in_kernel, tiles_per_seq=tiles_per_seq, keeps=keeps),
        grid=(n_tiles,),
        in_specs=[row_spec(D_MODEL), _const_spec((1, D_MODEL)),
                  _const_spec((D_MODEL, D_FF)), _const_spec((D_MODEL, D_FF)), _const_spec((D_FF, D_MODEL)),
                  _const_spec((1, D_MODEL)), _const_spec((D_MODEL, IN_A)),
                  _const_spec((1, ATTN_W)), _const_spec((1, ATTN_W)), _const_spec((GROUP_W, GROUP_W))],
        out_specs=(row_spec(D_MODEL), row_spec(SSM_W), row_spec(ATTN_W), row_spec(ATTN_W), row_spec(ATTN_W))
                  + tail_specs,
        out_shape=out_shape,
        scratch_shapes=scratch,
        compiler_params=_params(1),
        name="ffn_in",
    )(x, w["g_ffn1"], w["w1_gate"], w["w1_up"], w["w1_down"], w["g_mix"], w["w_in_a"],
      w["g_q"], w["g_k"], w["seg_mean"])
    return outs[:5], (list(outs[5:]) if prompt_form else outs[5])


def _glu_out(y, u, dsk_ref, wglu_ref, bglu_ref):
    y = jax.nn.gelu(y + dsk_ref[...] * u)
    z = _dot(y.astype(BF16), wglu_ref[...]) + bglu_ref[...]
    return (y * _sigmoid(z)).astype(BF16)


def _s5_kernel(u0_ref, u1_ref, u2_ref, u3_ref, are_ref, aim_ref, k8_ref, w7_ref, gq_ref,
               dsk_ref, wglu_ref, bglu_ref, *rest, casts):
    n = len(casts)
    y_ref, hre_ref, him_ref = rest[n:n + 3]
    lsp_ref, yq_ref, y2s_ref, sp_ref = rest[2 * n + 3:]
    casts.run(rest[:n] + rest[n + 3:2 * n + 3])
    _s5_scan(u0_ref, u1_ref, u2_ref, u3_ref, are_ref, aim_ref, k8_ref, w7_ref, gq_ref,
             dsk_ref, wglu_ref, bglu_ref, y_ref, hre_ref, him_ref, lsp_ref, yq_ref, y2s_ref, sp_ref)


def _s5_scan(u0_ref, u1_ref, u2_ref, u3_ref, are_ref, aim_ref, k8_ref, w7_ref, gq_ref,
             dsk_ref, wglu_ref, bglu_ref, y_ref, hre_ref, him_ref,
             lsp_ref, yq_ref, y2s_ref, sp_ref):
    u_refs = (u0_ref, u1_ref, u2_ref, u3_ref)
    nb, t_blk, _ = u0_ref.shape
    nblk = t_blk // BLK
    tiles = QS // LANES

    @pl.when(pl.program_id(0) == 0)
    def _():
        sp_ref[...] = jnp.zeros_like(sp_ref)

    sub = jnp.bitwise_and(lax.broadcasted_iota(jnp.int32, (t_blk // SUBLANES, SUBLANES, LANES), 1), BLK - 1)
    for q in range(NQ):
        ublks = []
        for b in range(nb):
            uq = u_refs[q][b]
            z3 = _dot(uq.astype(BF16), k8_ref[q]).reshape(t_blk // SUBLANES, SUBLANES, BLK * LANES)
            acc = z3[:, :, :LANES]
            for i in range(1, BLK):
                zi = z3[:, :, LANES * i:LANES * (i + 1)]
                acc = acc + jnp.where(sub >= i, pltpu.roll(zi, i, axis=1), 0.0)
            yq_ref[b * NQ + q] = acc.reshape(t_blk, LANES)
            ublks.append(jnp.concatenate(
                [u_refs[q][b, pl.ds(r, nblk, stride=BLK), :] for r in range(BLK)], axis=1))
        ls = _dot(jnp.concatenate(ublks, axis=0).astype(BF16), w7_ref[q])
        for b in range(nb):
            for t in range(2 * tiles):
                dst = pl.ds((q % 2) * tiles + t % tiles, nblk, stride=SUBLANES)
                lsp_ref[b, t // tiles, q // 2, dst, :] = ls[b * nblk:(b + 1) * nblk, LANES * t:LANES * (t + 1)]

    chains = [(b, half) for b in range(nb) for half in range(2)]
    coef = {half: (are_ref[half], aim_ref[half]) for half in range(2)}
    state = {c: (sp_ref[c[0], 0, c[1]], sp_ref[c[0], 1, c[1]]) for c in chains}
    for k in range(nblk):
        rows = slice(SUBLANES * k, SUBLANES * (k + 1))
        for b, half in chains:
            a_re, a_im = coef[half]
            s_re, s_im = state[b, half]
            l_re = lsp_ref[b, 0, half, rows, :]
            l_im = lsp_ref[b, 1, half, rows, :]
            lsp_ref[b, 0, half, rows, :] = s_re
            lsp_ref[b, 1, half, rows, :] = s_im
            state[b, half] = (a_re * s_re - a_im * s_im + l_re, a_re * s_im + a_im * s_re + l_im)
    for (b, half), (s_re, s_im) in state.items():
        sp_ref[b, 0, half] = s_re
        sp_ref[b, 1, half] = s_im
        for j in range(SUBLANES):
            lanes = slice((half * SUBLANES + j) * LANES, (half * SUBLANES + j + 1) * LANES)
            hre_ref[b:b + 1, lanes] = s_re[j:j + 1]
            him_ref[b:b + 1, lanes] = s_im[j:j + 1]

    for q in range(NQ):
        sc = jnp.concatenate(
            [jnp.concatenate(
                [lsp_ref[b, t // tiles, q // 2, pl.ds((q % 2) * tiles + t % tiles, nblk, stride=SUBLANES), :]
                 for t in range(2 * tiles)], axis=1) for b in range(nb)], axis=0)
        y2 = _dot(sc.astype(BF16), gq_ref[q])
        for b in range(nb):
            for r in range(BLK):
                y2s_ref[pl.ds(r, nblk, stride=BLK), :] = y2[b * nblk:(b + 1) * nblk, LANES * r:LANES * (r + 1)]
            yq_ref[b * NQ + q] = yq_ref[b * NQ + q] + y2s_ref[...]

    for b in range(nb):
        y = jnp.concatenate([yq_ref[b * NQ + q] for q in range(NQ)], axis=1)
        u = jnp.concatenate([u_refs[q][b] for q in range(NQ)], axis=1)
        y_ref[b] = _glu_out(y, u, dsk_ref, wglu_ref, bglu_ref)


def _s5(u, w, t_blk, cast_items):
    nb, seq, _ = u.shape
    nblk = t_blk // BLK
    casts = _CastStream(cast_items, seq // t_blk, lambda c: c)
    u_specs = [pl.BlockSpec((nb, t_blk, LANES), lambda c, q=q: (0, c, q)) for q in range(NQ)]
    state = pl.BlockSpec((nb, N_STATE), lambda c: (0, 0))
    packed = lambda a: a.reshape(2, SUBLANES, LANES)
    y, h_re, h_im, *cast_outs = pl.pallas_call(
        functools.partial(_s5_kernel, casts=casts),
        grid=(seq // t_blk,),
        in_specs=u_specs + [_const_spec((2, SUBLANES, LANES)), _const_spec((2, SUBLANES, LANES)),
                            _const_spec((NQ, LANES, BLK * LANES)),
                            _const_spec((NQ, BLK * LANES, 2 * QS)), _const_spec((NQ, 2 * QS, BLK * LANES)),
                            _const_spec((1, SSM_W)), _const_spec((SSM_W, SSM_W)), _const_spec((1, SSM_W))]
                 + casts.in_specs,
        out_specs=(pl.BlockSpec((nb, t_blk, SSM_W), lambda c: (0, c, 0)), state, state)
                  + tuple(casts.out_specs),
        out_shape=(jax.ShapeDtypeStruct((nb, seq, SSM_W), BF16),
                   jax.ShapeDtypeStruct((nb, N_STATE), F32), jax.ShapeDtypeStruct((nb, N_STATE), F32))
                  + tuple(casts.out_shapes),
        scratch_shapes=[pltpu.VMEM((nb, 2, 2, nblk * SUBLANES, LANES), F32),
                        pltpu.VMEM((nb * NQ, t_blk, LANES), F32),
                        pltpu.VMEM((t_blk, LANES), F32),
                        pltpu.VMEM((nb, 2, 2, SUBLANES, LANES), F32)],
        compiler_params=_params(1),
        name="s5",
    )(u, u, u, u, packed(w["a8_re"]), packed(w["a8_im"]), w["k8"], w["w7"], w["gq"],
      w["ssm_d"], w["w_glu"], w["b_glu"], *casts.arrays)
    return y, h_re, h_im, cast_outs


def _s5_step_kernel(u_ref, hre_ref, him_ref, are_ref, aim_ref, k8_ref, w7_ref, gq_ref,
                    dsk_ref, wglu_ref, bglu_ref, y_ref, nre_ref, nim_ref):
    u = u_ref[...]
    ys = []
    for q in range(NQ):
        ch = slice(q * QS, (q + 1) * QS)
        uq = u[:, q * LANES:(q + 1) * LANES].astype(BF16)
        x = _dot(uq, w7_ref[q, (BLK - 1) * LANES:, :])
        a_re, a_im = are_ref[:, ch], aim_ref[:, ch]
        s_re, s_im = hre_ref[:, ch], him_ref[:, ch]
        nre_ref[:, ch] = a_re * s_re - a_im * s_im + x[:, :QS]
        nim_ref[:, ch] = a_re * s_im + a_im * s_re + x[:, QS:]
        s = jnp.concatenate([s_re, s_im], axis=1).astype(BF16)
        ys.append(_dot(s, gq_ref[q, :, :LANES]) + _dot(uq, k8_ref[q, :, :LANES]))
    y_ref[...] = _glu_out(jnp.concatenate(ys, axis=1), u, dsk_ref, wglu_ref, bglu_ref)


def _s5_step(u, h_re, h_im, w):
    rows = u.shape[0]
    return pl.pallas_call(
        _s5_step_kernel,
        out_shape=(jax.ShapeDtypeStruct((rows, SSM_W), BF16),
                   jax.ShapeDtypeStruct((rows, N_STATE), F32), jax.ShapeDtypeStruct((rows, N_STATE), F32)),
        compiler_params=pltpu.CompilerParams(vmem_limit_bytes=VMEM_LIMIT),
        name="s5_step",
    )(u, h_re, h_im, w["ab_re"], w["ab_im"], w["k8"], w["w7"], w["gq"], w["ssm_d"], w["w_glu"], w["b_glu"])


def _step_attention(q, kn, vn, cache, dil):
    n_feat, window = cache.shape
    row8 = lax.broadcasted_iota(jnp.int32, (8, GROUP_W), 0)
    head8 = lax.broadcasted_iota(jnp.int32, (8, GROUP_W), 1) // HEAD_DIM == row8
    diag8 = (lax.broadcasted_iota(jnp.int32, (8, LANES), 0)
             == lax.broadcasted_iota(jnp.int32, (8, LANES), 1))
    q8 = jnp.where(head8, jnp.broadcast_to(q, (8, GROUP_W)), 0.0)
    pos = lax.broadcasted_iota(jnp.int32, (8, window), 1)
    s = _dot(q8.astype(BF16), cache[:GROUP_W].astype(BF16))
    s = jnp.where(jnp.bitwise_and(pos, dil - 1) == 0, s, NEG)
    s_new = jnp.sum(q8 * kn, axis=-1, keepdims=True)
    m = jnp.maximum(jnp.max(s, axis=-1, keepdims=True), s_new)
    p = jnp.exp(s - m)
    p_new = jnp.exp(s_new - m)
    den = jnp.sum(p, axis=-1, keepdims=True) + p_new
    o8 = (_dot_nt(p.astype(BF16), cache[GROUP_W:].astype(BF16)) + p_new * vn) / den
    o = jnp.sum(jnp.where(head8, o8, 0.0), axis=0, keepdims=True)
    lse = jnp.sum(jnp.where(diag8, m + jnp.log(den), 0.0), axis=0, keepdims=True)
    eye = (lax.broadcasted_iota(jnp.int32, (n_feat, n_feat), 0)
           == lax.broadcasted_iota(jnp.int32, (n_feat, n_feat), 1))
    new_col = jnp.sum(jnp.where(eye, jnp.concatenate([kn, vn], axis=1), 0.0), axis=-1, keepdims=True)
    rolled = pltpu.roll(cache, window - 1, axis=1)
    last = lax.broadcasted_iota(jnp.int32, (n_feat, window), 1) == window - 1
    return o, lse, jnp.where(last, new_col, rolled)


def _swa_kernel(q_ref, k_ref, v_ref, kp_ref, vp_ref, qs_ref, ks_ref, vs_ref, c_ref, *rest,
                qb, chunk, dil, casts):
    n = len(casts)
    o_ref, l_ref, os_ref, ls_ref, n_ref = rest[n:n + 5]
    casts.run(rest[:n] + rest[n + 5:2 * n + 5])
    prefetch = rest[2 * n + 5:]
    first = pl.program_id(2) == 0

    def rows(ref):
        return jnp.concatenate([ref[t] for t in range(ref.shape[0])], axis=0)

    q = rows(q_ref)
    kk = jnp.concatenate([rows(kp_ref), rows(k_ref)], axis=0)
    vv = jnp.concatenate([rows(vp_ref), rows(v_ref)], axis=0)
    nk = 2 * KEYS_PER_QUERY
    row = lax.broadcasted_iota(jnp.int32, (128, nk), 0)
    col = lax.broadcasted_iota(jnp.int32, (128, nk), 1)
    band = jnp.where(col >= row, jnp.where(col <= row + KEYS_PER_QUERY, 0.0, NEG), NEG)
    band_first = band + jnp.where(col < KEYS_PER_QUERY, jnp.where(first, NEG, 0.0), 0.0)
    lane = lax.broadcasted_iota(jnp.int32, (1, GROUP_W), 1) // HEAD_DIM
    lane_l = lax.broadcasted_iota(jnp.int32, (1, LANES), 1)
    for j in range(qb // 128):
        qj = q[128 * j:128 * (j + 1)]
        kj = kk[128 * j:128 * j + nk]
        vj = vv[128 * j:128 * j + nk]
        bias = band_first if j == 0 else band
        qs = jnp.concatenate([jnp.where(lane == h, qj, jnp.zeros_like(qj)) for h in range(HEADS)], axis=0)
        s = (_dot_nt(qs, kj).reshape(HEADS, 128, nk) + bias[None]).reshape(HEADS * 128, nk)
        m = jnp.max(s, axis=-1, keepdims=True)
        p = jnp.exp(s - m)
        den = jnp.sum(p, axis=-1, keepdims=True)
        on = _dot(p.astype(BF16), vj) * (1.0 / den)
        lse = m + jnp.log(den)
        o_acc = jnp.zeros((128, GROUP_W), F32)
        l_acc = jnp.zeros((128, LANES), F32)
        for h in range(HEADS):
            rows_h = slice(128 * h, 128 * (h + 1))
            o_acc = jnp.where(lane == h, on[rows_h], o_acc)
            l_acc = jnp.where(lane_l == h, lse[rows_h], l_acc)
        o_out = o_acc.astype(BF16)
        if chunk >= 128:
            t, r0 = divmod(128 * j, chunk)
            o_ref[t, r0:r0 + 128, :] = o_out
            l_ref[t, r0:r0 + 128, :] = l_acc
        else:
            for s_ in range(128 // chunk):
                t = (128 * j) // chunk + s_
                o_ref[t] = o_out[s_ * chunk:(s_ + 1) * chunk]
                l_ref[t] = l_acc[s_ * chunk:(s_ + 1) * chunk]

    def sample_step(cache):
        o, lse, new_cache = _step_attention(qs_ref[0].astype(F32), ks_ref[0], vs_ref[0], cache, dil)
        os_ref[0] = o.astype(BF16)
        ls_ref[0] = lse
        n_ref[0] = new_cache

    if not prefetch:
        sample_step(c_ref[0])
        return
    cbuf, sems = prefetch
    n_steps = pl.num_programs(0) * pl.num_programs(1) * pl.num_programs(2)
    step = (pl.program_id(0) * pl.num_programs(1) + pl.program_id(1)) * pl.num_programs(2) + pl.program_id(2)
    last_sample = c_ref.shape[0] - 1

    def fetch(s):
        slot = s % CACHE_BUFS
        return pltpu.make_async_copy(c_ref.at[jnp.minimum(s, last_sample)], cbuf.at[slot], sems.at[slot])

    @pl.when(step == 0)
    def _():
        for s in range(CACHE_BUFS - 1):
            fetch(jnp.int32(s)).start()

    @pl.when(step + CACHE_BUFS - 1 < n_steps)
    def _():
        fetch(step + CACHE_BUFS - 1).start()

    @pl.when(step >= 0)
    def _():
        fetch(step).wait()
        sample_step(cbuf[step % CACHE_BUFS])


def _swa(q, k, v, g, tile, q_s, kv_s, cache, cast_items):
    window, dil = GROUPS[g]
    nb, seq, _ = q.shape
    n_sample = q_s.shape[0]
    n_tiles = seq // tile
    chunk = tile // dil
    n = seq // dil
    qb = min(512, n)
    tpb = qb // chunk
    n_i = n // qb
    assert nb * dil * n_i >= n_sample and cache.shape == (n_sample, 2 * GROUP_W, window)
    view = lambda a: a.reshape(nb, n_tiles, dil, chunk, a.shape[-1])
    cur = pl.BlockSpec((None, tpb, None, chunk, GROUP_W), lambda b, r, i: (b, i, r, 0, g))
    if chunk >= 128:
        prev = pl.BlockSpec((None, 1, None, 128, GROUP_W),
                            lambda b, r, i: (b, jnp.maximum(i * tpb - 1, 0), r, chunk // 128 - 1, g))
    else:
        ptiles = 128 // chunk
        prev = pl.BlockSpec((None, ptiles, None, chunk, GROUP_W),
                            lambda b, r, i: (b, jnp.maximum(i * (tpb // ptiles) - 1, 0), r, 0, g))
    out_block = lambda width: pl.BlockSpec((None, tpb, None, chunk, width), lambda b, r, i: (b, i, r, 0, 0))
    step_of = lambda b, r, i: (b * dil + r) * n_i + i
    sample_of = lambda b, r, i: jnp.minimum(step_of(b, r, i), n_sample - 1)
    sample_block = lambda *dims: pl.BlockSpec((1,) + dims, lambda b, r, i: (sample_of(b, r, i),) + (0,) * len(dims))
    group_lanes = lambda col: pl.BlockSpec((1, 1, GROUP_W), lambda b, r, i: (sample_of(b, r, i), 0, col))
    casts = _CastStream(cast_items, nb * dil * n_i, step_of)
    if window >= OWN_REGION_WINDOW:
        assert nb * dil * n_i >= CACHE_BUFS - 1
        cache_in = pl.BlockSpec(memory_space=pl.ANY)
        scratch = [pltpu.VMEM((CACHE_BUFS, 2 * GROUP_W, window), F32), pltpu.SemaphoreType.DMA((CACHE_BUFS,))]
    else:
        cache_in, scratch = sample_block(2 * GROUP_W, window), []
    o, lse, o_s, l_s, new_cache, *cast_outs = pl.pallas_call(
        functools.partial(_swa_kernel, qb=qb, chunk=chunk, dil=dil, casts=casts),
        grid=(nb, dil, n_i),
        in_specs=[cur, cur, cur, prev, prev,
                  group_lanes(g), group_lanes(g), group_lanes(ATTN_W // GROUP_W + g), cache_in]
                 + casts.in_specs,
        out_specs=(out_block(GROUP_W), out_block(LANES),
                   sample_block(1, GROUP_W), sample_block(1, LANES), sample_block(2 * GROUP_W, window))
                  + tuple(casts.out_specs),
        out_shape=(jax.ShapeDtypeStruct((nb, n_tiles, dil, chunk, GROUP_W), BF16),
                   jax.ShapeDtypeStruct((nb, n_tiles, dil, chunk, LANES), F32),
                   jax.ShapeDtypeStruct((n_sample, 1, GROUP_W), BF16),
                   jax.ShapeDtypeStruct((n_sample, 1, LANES), F32),
                   jax.ShapeDtypeStruct(cache.shape, F32)) + tuple(casts.out_shapes),
        scratch_shapes=scratch,
        compiler_params=_params(3),
        name="swa_g%d" % g,
    )(view(q), view(k), view(v), view(k), view(v),
      q_s.reshape(n_sample, 1, ATTN_W), kv_s.reshape(n_sample, 1, 2 * ATTN_W),
      kv_s.reshape(n_sample, 1, 2 * ATTN_W), cache, *casts.arrays)
    return (o.reshape(nb * seq, GROUP_W), lse.reshape(nb * seq, LANES),
            o_s.reshape(n_sample, GROUP_W), l_s.reshape(n_sample, LANES), new_cache, cast_outs)


def _position_order(o_ref, l_ref, dil, nat_ref, slab0):
    tm = o_ref.shape[0]
    n = tm // dil
    for r in range(dil):
        rows = slice(r * n, (r + 1) * n)
        dst = pl.ds(r, n, stride=dil)
        nat_ref[slab0, dst, :] = o_ref[rows, :LANES].astype(F32)
        nat_ref[slab0 + 1, dst, :] = o_ref[rows, LANES:].astype(F32)
        nat_ref[slab0 + 2, dst, :] = l_ref[rows, :]
    o = jnp.concatenate([nat_ref[slab0], nat_ref[slab0 + 1]], axis=1)
    return o, nat_ref[slab0 + 2]


def _mix_ffn_kernel(x1_ref, ys_ref, o0_ref, o1_ref, o2_ref, l0_ref, l1_ref, l2_ref,
                    gm_ref, wgt_ref, wsp_ref, wap_ref, wo_ref, seget_ref,
                    g2_ref, wg_ref, wu_ref, wd_ref, y_ref, *scratch):
    x1 = x1_ref[...]
    h = _rms(x1, gm_ref[...]).astype(BF16)
    gates = _sigmoid(_dot(h, wgt_ref[...]))
    os_ = [o0_ref[...].astype(F32)]
    ls = [l0_ref[...]]
    for g, (o_ref, l_ref) in enumerate(((o1_ref, l1_ref), (o2_ref, l2_ref)), start=1):
        if scratch:
            o, l = _position_order(o_ref, l_ref, GROUPS[g][1], scratch[0], 3 * (g - 1))
        else:
            o, l = o_ref[...].astype(F32), l_ref[...]
        os_.append(o)
        ls.append(l)
    l_top = jnp.maximum(jnp.maximum(ls[0], ls[1]), ls[2])
    es = [jnp.exp(l - l_top) for l in ls]
    inv = 1.0 / (es[0] + es[1] + es[2])
    seget = seget_ref[...]
    y_attn = None
    for e, o in zip(es, os_):
        t = _dot((e * inv).astype(BF16), seget) * o
        y_attn = t if y_attn is None else y_attn + t
    mixed = (gates[:, :D_MODEL] * _dot(ys_ref[...], wsp_ref[...])
             + gates[:, D_MODEL:] * _dot(y_attn.astype(BF16), wap_ref[...]))
    x2 = x1 + _dot(mixed.astype(BF16), wo_ref[...])
    y_ref[...] = _ffn(x2, g2_ref, wg_ref, wu_ref, wd_ref)


def _mix_ffn(x1, y_ssm, os_, ls_, w, tm, residue_major):
    rows = x1.shape[0]
    row_spec = lambda width: pl.BlockSpec((tm, width), lambda i: (i, 0))
    n_slabs = 3 * (len(GROUPS) - 1)
    scratch = [pltpu.VMEM((n_slabs, tm, LANES), F32)] if residue_major else []
    return pl.pallas_call(
        _mix_ffn_kernel,
        grid=(rows // tm,),
        in_specs=[row_spec(D_MODEL), row_spec(SSM_W)] + [row_spec(GROUP_W)] * 3 + [row_spec(LANES)] * 3
                 + [_const_spec((1, D_MODEL)), _const_spec((D_MODEL, 2 * D_MODEL)),
                    _const_spec((SSM_W, D_MODEL)), _const_spec((GROUP_W, D_MODEL)),
                    _const_spec((D_MODEL, D_MODEL)), _const_spec((LANES, GROUP_W)),
                    _const_spec((1, D_MODEL)), _const_spec((D_MODEL, D_FF)), _const_spec((D_MODEL, D_FF)),
                    _const_spec((D_FF, D_MODEL))],
        out_specs=row_spec(D_MODEL),
        out_shape=jax.ShapeDtypeStruct((rows, D_MODEL), F32),
        scratch_shapes=scratch,
        compiler_params=_params(1),
        name="mix_ffn",
    )(x1, y_ssm, *os_, *ls_, w["g_mix"], w["w_gates"], w["w_ssm_proj"], w["w_attn_proj"], w["w_o"],
      w["seg_et"], w["g_ffn2"], w["w2_gate"], w["w2_up"], w["w2_down"])


def _prepare_weights(g_ffn1, w1_gate, w1_up, w1_down, g_mix, w_in, g_q, g_k,
                     ssm_a_re, ssm_a_im, ssm_log_dt, ssm_b_re, ssm_b_im, ssm_c_re, ssm_c_im,
                     ssm_d, w_glu, b_glu, w_ssm_proj, w_attn_proj, w_o, g_ffn2, w2_gate, w2_up, w2_down):
    w = {}
    row = lambda a: a.reshape(1, -1).astype(F32)
    w["g_ffn1"], w["g_mix"], w["g_ffn2"] = row(g_ffn1), row(g_mix), row(g_ffn2)
    w["w_glu"] = w_glu.astype(BF16)
    f32 = lambda a: a.astype(F32)
    w["late"] = {
        "s5": (("w2_down", f32(w2_down), 0, D_MODEL), ("w_o", f32(w_o), 0, D_MODEL),
               ("w_ssm_proj", f32(w_ssm_proj), 0, D_MODEL), ("w_attn_proj", f32(w_attn_proj), 0, D_MODEL),
               ("w_gates", f32(w_in), IN_A, 2 * D_MODEL)),
        "swa": ((("w2_gate", f32(w2_gate), 0, D_FF), ("w2_up", f32(w2_up), 0, D_FF)), (), ())}
    per_head = lambda g: jnp.broadcast_to(g[:, None, :], (len(GROUPS), HEADS, HEAD_DIM)).reshape(1, ATTN_W)
    w["g_q"], w["g_k"] = per_head(g_q.astype(F32)), per_head(g_k.astype(F32))
    head_of_lane = jnp.arange(GROUP_W) // HEAD_DIM
    w["seg_mean"] = ((head_of_lane[:, None] == head_of_lane[None, :]) / HEAD_DIM).astype(BF16)
    w["seg_et"] = (jnp.arange(LANES)[:, None] == head_of_lane[None, :]).astype(BF16)
    w["ssm_d"], w["b_glu"] = row(ssm_d), row(b_glu)

    early = (("w1_gate", f32(w1_gate), 0, D_FF), ("w1_up", f32(w1_up), 0, D_FF),
             ("w1_down", f32(w1_down), 0, D_MODEL), ("w_in_a", f32(w_in), 0, IN_A))
    (pw, w["k8"], w["w7"], w["gq"]), converted = _ssm_prep(
        ssm_a_re, ssm_a_im, ssm_log_dt, ssm_b_re, ssm_b_im, ssm_c_re, ssm_c_im, [e[1:] for e in early])
    w.update({e[0]: c for e, c in zip(early, converted)})
    per_group = lambda a: a.reshape(SSM_GROUPS, SSM_CH, SSM_P)[:, 0].reshape(1, N_STATE)
    w["ab_re"], w["ab_im"] = per_group(pw[0, 0]), per_group(pw[0, 1])
    w["a8_re"], w["a8_im"] = per_group(pw[1, 0]), per_group(pw[1, 1])
    return w


def _forward(x_p, x_s, state_s, caches_s, w):
    nb, seq, _ = x_p.shape
    n_s = x_s.shape[0]
    assert seq % TILE == 0 and x_s.shape[1] == 1
    (x1, u, q, k, v), caches_p = _ffn_in(x_p.reshape(nb * seq, D_MODEL), w, TILE, seq // TILE, True)
    (x1_s, u_s, q_s, _, _), kv_s = _ffn_in(x_s.reshape(n_s, D_MODEL), w, n_s, 1, False)
    late = w["late"]["s5"]
    y_ssm, *h_p, converted = _s5(u.reshape(nb, seq, SSM_W), w, min(S5_T, seq), [e[1:] for e in late])
    w = {**w, **{e[0]: c for e, c in zip(late, converted)}}
    y_ssm_s, *h_s = _s5_step(u_s, state_s[0], state_s[1], w)
    q3, k3, v3 = (a.reshape(nb, seq, ATTN_W) for a in (q, k, v))
    os_p, ls_p, os_s, ls_s, new_caches_s = [], [], [], [], []
    for g in range(len(GROUPS)):
        late = w["late"]["swa"][g]
        o, l, o_s, l_s, c_new, converted = _swa(q3, k3, v3, g, TILE, q_s, kv_s, caches_s[g],
                                                [e[1:] for e in late])
        w = {**w, **{e[0]: c for e, c in zip(late, converted)}}
        for lst, a in zip((os_p, ls_p, os_s, ls_s, new_caches_s), (o, l, o_s, l_s, c_new)):
            lst.append(a)
    y_p = _mix_ffn(x1, y_ssm.reshape(nb * seq, SSM_W), os_p, ls_p, w, TILE, True)
    y_s = _mix_ffn(x1_s, y_ssm_s, os_s, ls_s, w, n_s, False)
    return (y_p.reshape(nb, seq, D_MODEL), y_s.reshape(n_s, 1, D_MODEL), h_p, h_s, caches_p, new_caches_s)


def _to_features_major(c):
    nb, win = c.shape[:2]
    return jnp.transpose(c, (0, 2, 3, 4, 1)).reshape(nb, 2 * GROUP_W, win)


def _to_window_buffer(c):
    nb, _, win = c.shape
    return jnp.transpose(c.reshape(nb, 2, HEADS, HEAD_DIM, win), (0, 4, 1, 2, 3))[None]


def kernel(x_prompt, x_sample, cache_kv_w128, cache_kv_w512, cache_kv_w2048, state_ssm_re, state_ssm_im, g_ffn1, w1_gate, w1_up, w1_down, g_mix, w_in, g_q, g_k, ssm_a_re, ssm_a_im, ssm_log_dt, ssm_b_re, ssm_b_im, ssm_c_re, ssm_c_im, ssm_d, w_glu, b_glu, w_ssm_proj, w_attn_proj, w_o, g_ffn2, w2_gate, w2_up, w2_down):
    layer_weights = (g_ffn1, w1_gate, w1_up, w1_down, g_mix, w_in, g_q, g_k,
                     ssm_a_re, ssm_a_im, ssm_log_dt, ssm_b_re, ssm_b_im, ssm_c_re, ssm_c_im,
                     ssm_d, w_glu, b_glu, w_ssm_proj, w_attn_proj, w_o, g_ffn2, w2_gate, w2_up, w2_down)
    depth = g_ffn1.shape[0]
    assert depth == 1, "window caches of deeper layers would need the previous layer's outputs"
    w = _prepare_weights(*(a[0] for a in layer_weights))
    nb_s = x_sample.shape[0]
    sdt = state_ssm_re.dtype
    as_state = lambda h: h.reshape(1, -1, SSM_GROUPS, SSM_P).astype(sdt)

    caches = [_to_features_major(c[0]) for c in (cache_kv_w128, cache_kv_w512, cache_kv_w2048)]
    state_s = (state_ssm_re[0].reshape(nb_s, N_STATE).astype(F32),
               state_ssm_im[0].reshape(nb_s, N_STATE).astype(F32))
    y_p, y_s, h_p, h_s, kv_p, kv_s = _forward(x_prompt, x_sample, state_s, caches, w)
    return ((y_p, y_s) + tuple(_to_window_buffer(c) for c in kv_p) + (as_state(h_p[0]), as_state(h_p[1]))
            + tuple(_to_window_buffer(c) for c in kv_s) + (as_state(h_s[0]), as_state(h_s[1])))
```
